```python
import math
import jax, jax.numpy as jnp
from jax import lax
import numpy as np

D_MODEL = 1024
BATCH = 4
SEQ = 4096
DEPTH = 2

N_A = DEPTH // 2
N_B = DEPTH - N_A
CONV_W = 3
HEAD_DIM = 64
N_HEADS = D_MODEL // HEAD_DIM
N_KV_HEADS = 4
GROUP = N_HEADS // N_KV_HEADS
WINDOW = 128
BLOCK = 128
ROT_DIM = HEAD_DIM // 4
ROPE_THETA = 500000.0
D_FF = ((8 * D_MODEL // 3 + 255) // 256) * 256
EPS = 1e-6
NEG = -1e30

kernel_name = "yoco_shortconv_swa_sink_hybrid"


def rmsnorm(x, g):
    xf = x.astype(jnp.float32)
    r = lax.rsqrt(jnp.mean(xf * xf, axis=-1, keepdims=True) + EPS)
    return (xf * r).astype(x.dtype) * g


def rope_tables(seq_len, dtype):
    inv_freq = ROPE_THETA ** (-jnp.arange(0, ROT_DIM, 2, dtype=jnp.float32) / ROT_DIM)
    ang = jnp.arange(seq_len, dtype=jnp.float32)[:, None] * inv_freq[None, :]
    return jnp.cos(ang)[:, None, :].astype(dtype), jnp.sin(ang)[:, None, :].astype(dtype)


def partial_rotary(t, cos, sin):
    half = ROT_DIM // 2
    t1, t2, rest = t[..., :half], t[..., half:ROT_DIM], t[..., ROT_DIM:]
    return jnp.concatenate([t1 * cos - t2 * sin, t2 * cos + t1 * sin, rest], axis=-1)


def causal_depthwise_conv3(u, w):
    s = u.shape[1]
    up = jnp.pad(u, ((0, 0), (CONV_W - 1, 0), (0, 0)))
    return up[:, 0:s] * w[0] + up[:, 1:s + 1] * w[1] + up[:, 2:s + 2] * w[2]


def short_conv_mixer(h, w_in, conv_w, w_out):
    bcx = h @ w_in
    b_gate, c_gate, u = jnp.split(bcx, 3, axis=-1)
    y = b_gate * causal_depthwise_conv3(c_gate * u, conv_w)
    return y @ w_out


def swiglu(h, w_gate_up, w_down):
    g, u = jnp.split(h @ w_gate_up, 2, axis=-1)
    return (jax.nn.silu(g) * u) @ w_down


def sliding_window_sink_attention(q, k, v, sinks):
    bsz, s = q.shape[0], q.shape[1]
    nb = s // BLOCK
    qb = q.reshape(bsz, nb, BLOCK, N_KV_HEADS, GROUP, HEAD_DIM)

    def with_prev(t):
        tb = t.reshape(bsz, nb, BLOCK, N_KV_HEADS, HEAD_DIM)
        prev = jnp.concatenate([jnp.zeros_like(tb[:, :1]), tb[:, :-1]], axis=1)
        return jnp.concatenate([prev, tb], axis=2)

    kk, vv = with_prev(k), with_prev(v)
    scale = 1.0 / math.sqrt(HEAD_DIM)
    scores = jnp.einsum('bnqhgd,bnkhd->bnhgqk', qb, kk).astype(jnp.float32) * scale

    qi = jnp.arange(BLOCK)[:, None]
    kj = jnp.arange(2 * BLOCK)[None, :]
    diff = BLOCK + qi - kj
    band = (diff >= 0) & (diff < WINDOW)
    not_pad = (jnp.arange(nb)[:, None, None] > 0) | (kj[None] >= BLOCK)
    valid = band[None] & not_pad
    scores = jnp.where(valid[None, :, None, None], scores, NEG)

    sink = jnp.broadcast_to(
        sinks.astype(jnp.float32).reshape(N_KV_HEADS, GROUP)[None, None, :, :, None, None],
        scores.shape[:-1] + (1,))
    probs = jax.nn.softmax(jnp.concatenate([scores, sink], axis=-1), axis=-1)[..., :-1]
    out = jnp.einsum('bnhgqk,bnkhd->bnqhgd', probs.astype(v.dtype), vv)
    return out.reshape(bsz, s, N_HEADS * HEAD_DIM)


def setup_inputs(seed: int = 0) -> dict:
    key = jax.random.key(seed)
    ks = jax.random.split(key, 24)
    f32 = jnp.float32
    D, F = D_MODEL, D_FF
    QD = N_HEADS * HEAD_DIM
    KVD = N_KV_HEADS * HEAD_DIM

    def nrm(k, shape, fan_in):
        return jax.random.normal(k, shape, f32) * (fan_in ** -0.5)

    def gain(k, shape):
        return 1.0 + 0.05 * jax.random.normal(k, shape, f32)

    return {
        "x": jax.random.normal(ks[0], (BATCH, SEQ, D), f32),
        "a_pre_norm": gain(ks[1], (N_A, D)),
        "a_w_in": nrm(ks[2], (N_A, D, 3 * D), D),
        "a_conv_w": nrm(ks[3], (N_A, CONV_W, D), CONV_W),
        "a_w_out": nrm(ks[4], (N_A, D, D), D),
        "a_post_norm": gain(ks[5], (N_A, D)),
        "ffn_pre_norm": gain(ks[6], (DEPTH, D)),
        "ffn_w_gate_up": nrm(ks[7], (DEPTH, D, 2 * F), D),
        "ffn_w_down": nrm(ks[8], (DEPTH, F, D), F),
        "ffn_post_norm": gain(ks[9], (DEPTH, D)),
        "kv_norm": gain(ks[10], (D,)),
        "w_kv": nrm(ks[11], (D, 2 * KVD), D),
        "b_pre_norm": gain(ks[12], (N_B, D)),
        "b_w_q": nrm(ks[13], (N_B, D, QD), D),
        "b_sinks": 0.5 * jax.random.normal(ks[14], (N_B, N_HEADS), f32),
        "b_w_o": nrm(ks[15], (N_B, QD, D), QD),
        "b_post_norm": gain(ks[16], (N_B, D)),
    }


def reference(x, a_pre_norm, a_w_in, a_conv_w, a_w_out, a_post_norm,
              ffn_pre_norm, ffn_w_gate_up, ffn_w_down, ffn_post_norm,
              kv_norm, w_kv,
              b_pre_norm, b_w_q, b_sinks, b_w_o, b_post_norm):
    bsz, s, _ = x.shape
    cos, sin = rope_tables(s, x.dtype)
    h = x
    for l in range(DEPTH):
        if l < N_A:
            mix = short_conv_mixer(rmsnorm(h, a_pre_norm[l]), a_w_in[l], a_conv_w[l], a_w_out[l])
            h = h + rmsnorm(mix, a_post_norm[l])
        else:
            j = l - N_A
            if j == 0:
                kv = rmsnorm(h, kv_norm) @ w_kv
                k_sh, v_sh = jnp.split(kv, 2, axis=-1)
                k_sh = partial_rotary(k_sh.reshape(bsz, s, N_KV_HEADS, HEAD_DIM), cos, sin)
                v_sh = v_sh.reshape(bsz, s, N_KV_HEADS, HEAD_DIM)
            q = (rmsnorm(h, b_pre_norm[j]) @ b_w_q[j]).reshape(bsz, s, N_HEADS, HEAD_DIM)
            q = partial_rotary(q, cos, sin)
            attn = sliding_window_sink_attention(q, k_sh, v_sh, b_sinks[j]) @ b_w_o[j]
            h = h + rmsnorm(attn, b_post_norm[j])
        ff = swiglu(rmsnorm(h, ffn_pre_norm[l]), ffn_w_gate_up[l], ffn_w_down[l])
        h = h + rmsnorm(ff, ffn_post_norm[l])
    return h
```

```python
import functools
import math

import jax
import jax.numpy as jnp
from jax import lax
from jax.experimental import pallas as pl
from jax.experimental.pallas import tpu as pltpu

CONV_W = 3
HEAD_DIM = 64
N_KV_HEADS = 4
WINDOW = 128
BLOCK = 128
ROT_DIM = HEAD_DIM // 4
ROPE_THETA = 500000.0
EPS = 1e-6
NEG = -1e30

LANES = 128
SUBLANES = 8
ROW_TILE = 512
VMEM_LIMIT = 56 * 1024 * 1024

F32 = jnp.float32
BF16 = jnp.bfloat16


def _rms(x, g):
    r = lax.rsqrt(jnp.mean(x * x, axis=-1, keepdims=True) + EPS)
    return (x * r) * g


def _dot(a, b):
    return jnp.dot(a, b, preferred_element_type=F32)


def _const_spec(shape):
    return pl.BlockSpec(shape, lambda *_: (0,) * len(shape), pipeline_mode=pl.Buffered(1))


def _params(semantics):
    return pltpu.CompilerParams(dimension_semantics=semantics, vmem_limit_bytes=VMEM_LIMIT)


def _conv_mixer_kernel(h_ref, pre_ref, win_ref, cw_ref, wout_ref, post_ref, o_ref, cu_ref,
                       *, tiles_per_seq):
    tm, d = h_ref.shape
    i = pl.program_id(0)
    h = h_ref[...]
    hn = _rms(h, pre_ref[...]).astype(BF16)
    bcx = _dot(hn, win_ref[...])
    b_gate, c_gate, u = bcx[:, :d], bcx[:, d:2 * d], bcx[:, 2 * d:]
    cu = c_gate * u

    @pl.when(i % tiles_per_seq == 0)
    def _():
        cu_ref[0:SUBLANES, :] = jnp.zeros((SUBLANES, d), F32)

    @pl.when(i % tiles_per_seq != 0)
    def _():
        cu_ref[0:SUBLANES, :] = cu_ref[tm:tm + SUBLANES, :]

    cu_ref[SUBLANES:tm + SUBLANES, :] = cu
    cw = cw_ref[...]
    conv = (cu_ref[SUBLANES - 2:tm + SUBLANES - 2, :] * cw[0:1, :]
            + cu_ref[SUBLANES - 1:tm + SUBLANES - 1, :] * cw[1:2, :]
            + cu * cw[2:3, :])
    y = (b_gate * conv).astype(BF16)
    mix = _dot(y, wout_ref[...])
    o_ref[...] = h + _rms(mix, post_ref[...])


def _conv_mixer(h, pre, w_in, conv_w, w_out, post, seq_len):
    t, d = h.shape
    tm = ROW_TILE
    kern = functools.partial(_conv_mixer_kernel, tiles_per_seq=seq_len // tm)
    return pl.pallas_call(
        kern,
        grid=(t // tm,),
        in_specs=[
            pl.BlockSpec((tm, d), lambda i: (i, 0)),
            _const_spec((1, d)),
            _const_spec((d, 3 * d)),
            _const_spec((CONV_W, d)),
            _const_spec((d, d)),
            _const_spec((1, d)),
        ],
        out_specs=pl.BlockSpec((tm, d), lambda i: (i, 0)),
        out_shape=jax.ShapeDtypeStruct((t, d), F32),
        scratch_shapes=[pltpu.VMEM((tm + SUBLANES, d), F32)],
        compiler_params=_params(("arbitrary",)),
        name="conv_mixer",
    )(h, pre, w_in, conv_w, w_out, post)


def _ffn_kernel(h_ref, pre_ref, wg_ref, wu_ref, wd_ref, post_ref, o_ref, *, chunks):
    h = h_ref[...]
    hn = _rms(h, pre_ref[...]).astype(BF16)
    acc = None
    for lo, hi in chunks:
        g = _dot(hn, wg_ref[:, lo:hi])
        u = _dot(hn, wu_ref[:, lo:hi])
        a = (g * jax.nn.sigmoid(g) * u).astype(BF16)
        part = _dot(a, wd_ref[lo:hi, :])
        acc = part if acc is None else acc + part
    o_ref[...] = h + _rms(acc, post_ref[...])


def _ffn_chunks(f):
    step = 768
    return tuple((lo, min(lo + step, f)) for lo in range(0, f, step))


def _ffn(h, pre, w_gate, w_up, w_down, post):
    t, d = h.shape
    f = w_down.shape[0]
    tm = ROW_TILE
    kern = functools.partial(_ffn_kernel, chunks=_ffn_chunks(f))
    return pl.pallas_call(
        kern,
        grid=(t // tm,),
        in_specs=[
            pl.BlockSpec((tm, d), lambda i: (i, 0)),
            _const_spec((1, d)),
            _const_spec((d, f)),
            _const_spec((d, f)),
            _const_spec((f, d)),
            _const_spec((1, d)),
        ],
        out_specs=pl.BlockSpec((tm, d), lambda i: (i, 0)),
        out_shape=jax.ShapeDtypeStruct((t, d), F32),
        compiler_params=_params(("parallel",)),
        name="swiglu",
    )(h, pre, w_gate, w_up, w_down, post)


def _rope(t, cos_t, sin_up, sin_dn):
    outs = []
    for j in range(t.shape[1] // LANES):
        c = t[:, j * LANES:(j + 1) * LANES]
        outs.append(c * cos_t
                    + pltpu.roll(c, ROT_DIM // 2, axis=1) * sin_up
                    + pltpu.roll(c, LANES - ROT_DIM // 2, axis=1) * sin_dn)
    return jnp.concatenate(outs, axis=1)


def _qkv_kernel(h_ref, kvn_ref, wk_ref, wv_ref, qn_ref, wq_ref, cos_ref, sup_ref, sdn_ref,
                q_ref, k_ref, v_ref):
    h = h_ref[...]
    cos_t, sin_up, sin_dn = cos_ref[...], sup_ref[...], sdn_ref[...]
    hkv = _rms(h, kvn_ref[...]).astype(BF16)
    k = _dot(hkv, wk_ref[...])
    k_ref[...] = _rope(k, cos_t, sin_up, sin_dn).astype(BF16)
    v_ref[...] = _dot(hkv, wv_ref[...]).astype(BF16)
    hq = _rms(h, qn_ref[...]).astype(BF16)
    q = _dot(hq, wq_ref[...])
    q_ref[...] = (_rope(q, cos_t, sin_up, sin_dn) * (1.0 / math.sqrt(HEAD_DIM))).astype(BF16)


def _rope_tables(seq_len):
    half = ROT_DIM // 2
    inv_freq = ROPE_THETA ** (-jnp.arange(0, ROT_DIM, 2, dtype=F32) / ROT_DIM)
    ang = jnp.arange(seq_len, dtype=F32)[:, None] * inv_freq[None, :]
    cos, sin = jnp.cos(ang), jnp.sin(ang)
    ones = jnp.ones((seq_len, HEAD_DIM - ROT_DIM), F32)
    zeros = jnp.zeros((seq_len, HEAD_DIM - ROT_DIM), F32)
    zh = jnp.zeros((seq_len, half), F32)
    reps = LANES // HEAD_DIM
    cos_t = jnp.tile(jnp.concatenate([cos, cos, ones], axis=1), (1, reps))
    sin_up = jnp.tile(jnp.concatenate([zh, sin, zeros], axis=1), (1, reps))
    sin_dn = jnp.tile(jnp.concatenate([-sin, zh, zeros], axis=1), (1, reps))
    return cos_t, sin_up, sin_dn


def _qkv(h, kv_norm, w_k, w_v, q_norm, w_q, tables, seq_len):
    t, d = h.shape
    kvd = w_k.shape[1]
    qd = w_q.shape[1]
    tm = ROW_TILE
    tps = seq_len // tm
    row = lambda i: (i, 0)
    pos = lambda i: (i % tps, 0)
    return pl.pallas_call(
        _qkv_kernel,
        grid=(t // tm,),
        in_specs=[
            pl.BlockSpec((tm, d), row),
            _const_spec((1, d)),
            _const_spec((d, kvd)),
            _const_spec((d, kvd)),
            _const_spec((1, d)),
            _const_spec((d, qd)),
            pl.BlockSpec((tm, LANES), pos),
            pl.BlockSpec((tm, LANES), pos),
            pl.BlockSpec((tm, LANES), pos),
        ],
        out_specs=[
            pl.BlockSpec((tm, qd), row),
            pl.BlockSpec((tm, kvd), row),
            pl.BlockSpec((tm, kvd), row),
        ],
        out_shape=[
            jax.ShapeDtypeStruct((t, qd), BF16),
            jax.ShapeDtypeStruct((t, kvd), BF16),
            jax.ShapeDtypeStruct((t, kvd), BF16),
        ],
        compiler_params=_params(("parallel",)),
        name="qkv_proj",
    )(h, kv_norm, w_k, w_v, q_norm, w_q, *tables)


def _attn_kernel(h_ref, q_ref, kp_ref, kc_ref, vp_ref, vc_ref, sink_ref, wo_ref, post_ref,
                 o_ref, kwin_ref, vwin_ref, att_ref, *, tiles_per_seq):
    tq = q_ref.shape[0]
    n_heads = q_ref.shape[1] // HEAD_DIM
    group = n_heads // N_KV_HEADS
    first_tile = (pl.program_id(0) % tiles_per_seq) == 0

    kwin_ref[0:BLOCK, :] = kp_ref[...]
    kwin_ref[BLOCK:, :] = kc_ref[...]
    vwin_ref[0:BLOCK, :] = vp_ref[...]
    vwin_ref[BLOCK:, :] = vc_ref[...]

    qi = lax.broadcasted_iota(jnp.int32, (BLOCK, 2 * BLOCK), 0)
    kj = lax.broadcasted_iota(jnp.int32, (BLOCK, 2 * BLOCK), 1)
    diff = BLOCK + qi - kj
    band = (diff >= 0) & (diff < WINDOW)
    band_first = band & (jnp.logical_not(first_tile) | (kj >= BLOCK))

    for j in range(tq // BLOCK):
        valid = band_first if j == 0 else band
        r0 = j * BLOCK
        outs = []
        for g in range(N_KV_HEADS):
            kg = kwin_ref[r0:r0 + 2 * BLOCK, g * HEAD_DIM:(g + 1) * HEAD_DIM]
            vg = vwin_ref[r0:r0 + 2 * BLOCK, g * HEAD_DIM:(g + 1) * HEAD_DIM]
            for hh in range(group):
                head = g * group + hh
                qh = q_ref[r0:r0 + BLOCK, head * HEAD_DIM:(head + 1) * HEAD_DIM]
                s = lax.dot_general(qh, kg, (((1,), (1,)), ((), ())),
                                    preferred_element_type=F32)
                s = jnp.where(valid, s, NEG)
                sink = sink_ref[0:1, head:head + 1]
                m = jnp.maximum(jnp.max(s, axis=-1, keepdims=True), sink)
                e = jnp.exp(s - m)
                denom = jnp.sum(e, axis=-1, keepdims=True) + jnp.exp(sink - m)
                p = (e / denom).astype(BF16)
                outs.append(_dot(p, vg))
        att_ref[r0:r0 + BLOCK, :] = jnp.concatenate(outs, axis=1).astype(BF16)

    proj = _dot(att_ref[...], wo_ref[...])
    o_ref[...] = h_ref[...] + _rms(proj, post_ref[...])


def _attention(h, q, k, v, sinks, w_o, post, seq_len):
    t, d = h.shape
    qd = q.shape[1]
    kvd = k.shape[1]
    tq = ROW_TILE
    tps = seq_len // tq
    bpt = tq // BLOCK
    row = lambda i: (i, 0)
    prev = lambda i: (jnp.maximum(i * bpt - 1, (i // tps) * tps * bpt), 0)
    kern = functools.partial(_attn_kernel, tiles_per_seq=tps)
    return pl.pallas_call(
        kern,
        grid=(t // tq,),
        in_specs=[
            pl.BlockSpec((tq, d), row),
            pl.BlockSpec((tq, qd), row),
            pl.BlockSpec((BLOCK, kvd), prev),
            pl.BlockSpec((tq, kvd), row),
            pl.BlockSpec((BLOCK, kvd), prev),
            pl.BlockSpec((tq, kvd), row),
            _const_spec((1, sinks.shape[1])),
            _const_spec((qd, d)),
            _const_spec((1, d)),
        ],
        out_specs=pl.BlockSpec((tq, d), row),
        out_shape=jax.ShapeDtypeStruct((t, d), F32),
        scratch_shapes=[
            pltpu.VMEM((tq + BLOCK, kvd), BF16),
            pltpu.VMEM((tq + BLOCK, kvd), BF16),
            pltpu.VMEM((tq, qd), BF16),
        ],
        compiler_params=_params(("parallel",)),
        name="swa_attention",
    )(h, q, k, k, v, v, sinks, w_o, post)


def kernel(x, a_pre_norm, a_w_in, a_conv_w, a_w_out, a_post_norm, ffn_pre_norm, ffn_w_gate_up,
           ffn_w_down, ffn_post_norm, kv_norm, w_kv, b_pre_norm, b_w_q, b_sinks, b_w_o,
           b_post_norm):
    bsz, s, d = x.shape
    n_a = a_w_in.shape[0]
    depth = ffn_w_gate_up.shape[0]
    f = ffn_w_down.shape[1]
    kvd = w_kv.shape[1] // 2
    tables = _rope_tables(s)
    vec = lambda g: g.reshape(1, -1)

    h = x.reshape(bsz * s, d)
    kv = None
    for l in range(depth):
        if l < n_a:
            h = _conv_mixer(h, vec(a_pre_norm[l]), a_w_in[l].astype(BF16), a_conv_w[l],
                            a_w_out[l].astype(BF16), vec(a_post_norm[l]), s)
        else:
            j = l - n_a
            w_k = w_kv[:, :kvd].astype(BF16)
            w_v = w_kv[:, kvd:].astype(BF16)
            q, k_new, v_new = _qkv(h, vec(kv_norm), w_k, w_v, vec(b_pre_norm[j]),
                                   b_w_q[j].astype(BF16), tables, s)
            if kv is None:
                kv = (k_new, v_new)
            h = _attention(h, q, kv[0], kv[1], vec(b_sinks[j]), b_w_o[j].astype(BF16),
                           vec(b_post_norm[j]), s)
        w_gu = ffn_w_gate_up[l]
        h = _ffn(h, vec(ffn_pre_norm[l]), w_gu[:, :f].astype(BF16), w_gu[:, f:].astype(BF16),
                 ffn_w_down[l].astype(BF16), vec(ffn_post_norm[l]))
    return h.reshape(bsz, s, d)
```

```python
import functools
import math

import jax
import jax.numpy as jnp
import numpy as np
from jax import lax
from jax.experimental import pallas as pl
from jax.experimental.pallas import tpu as pltpu

CONV_W = 3
HEAD_DIM = 64
N_KV_HEADS = 4
WINDOW = 128
BLOCK = 128
ROT_DIM = HEAD_DIM // 4
ROPE_THETA = 500000.0
EPS = 1e-6
NEG = -1e30

LANES = 128
SUBLANES = 8
BF16_ROWS = 16
ROW_TILE = 512
VMEM_LIMIT = 56 * 1024 * 1024

F32 = jnp.float32
BF16 = jnp.bfloat16

assert WINDOW == BLOCK and LANES == 2 * HEAD_DIM


def _rms(x, g):
    r = lax.rsqrt(jnp.mean(x * x, axis=-1, keepdims=True) + EPS)
    return (x * r) * g


def _dot(a, b):
    return jnp.dot(a, b, preferred_element_type=F32)


def _dot_nt(a, b):
    return lax.dot_general(a, b, (((1,), (1,)), ((), ())), preferred_element_type=F32)


def _const_spec(shape):
    return pl.BlockSpec(shape, lambda *_: (0,) * len(shape), pipeline_mode=pl.Buffered(1))


def _params(semantics):
    return pltpu.CompilerParams(dimension_semantics=semantics, vmem_limit_bytes=VMEM_LIMIT)


def _conv_mixer_kernel(h_ref, pre_ref, win_ref, cw_ref, wout_ref, post_ref, o_ref, cu_ref,
                       *, tiles_per_seq):
    tm, d = h_ref.shape
    i = pl.program_id(0)
    h = h_ref[...]
    hn = _rms(h, pre_ref[...]).astype(BF16)
    bcx = _dot(hn, win_ref[...])
    b_gate, c_gate, u = bcx[:, :d], bcx[:, d:2 * d], bcx[:, 2 * d:]
    cu = c_gate * u

    @pl.when(i % tiles_per_seq == 0)
    def _():
        cu_ref[0:SUBLANES, :] = jnp.zeros((SUBLANES, d), F32)

    @pl.when(i % tiles_per_seq != 0)
    def _():
        cu_ref[0:SUBLANES, :] = cu_ref[tm:tm + SUBLANES, :]

    cu_ref[SUBLANES:tm + SUBLANES, :] = cu
    cw = cw_ref[...]
    conv = (cu_ref[SUBLANES - 2:tm + SUBLANES - 2, :] * cw[0:1, :]
            + cu_ref[SUBLANES - 1:tm + SUBLANES - 1, :] * cw[1:2, :]
            + cu * cw[2:3, :])
    y = (b_gate * conv).astype(BF16)
    mix = _dot(y, wout_ref[...])
    o_ref[...] = h + _rms(mix, post_ref[...])


def _conv_mixer(h, pre, w_in, conv_w, w_out, post, seq_len):
    t, d = h.shape
    tm = ROW_TILE
    kern = functools.partial(_conv_mixer_kernel, tiles_per_seq=seq_len // tm)
    return pl.pallas_call(
        kern,
        grid=(t // tm,),
        in_specs=[
            pl.BlockSpec((tm, d), lambda i: (i, 0)),
            _const_spec((1, d)),
            _const_spec((d, 3 * d)),
            _const_spec((CONV_W, d)),
            _const_spec((d, d)),
            _const_spec((1, d)),
        ],
        out_specs=pl.BlockSpec((tm, d), lambda i: (i, 0)),
        out_shape=jax.ShapeDtypeStruct((t, d), F32),
        scratch_shapes=[pltpu.VMEM((tm + SUBLANES, d), F32)],
        compiler_params=_params(("arbitrary",)),
        name="conv_mixer",
    )(h, pre, w_in, conv_w, w_out, post)


def _ffn_kernel(h_ref, pre_ref, wg_ref, wu_ref, wd_ref, post_ref, o_ref, *, chunks):
    h = h_ref[...]
    hn = _rms(h, pre_ref[...]).astype(BF16)
    acc = None
    for lo, hi in chunks:
        g = _dot(hn, wg_ref[:, lo:hi])
        u = _dot(hn, wu_ref[:, lo:hi])
        a = (g * jax.nn.sigmoid(g) * u).astype(BF16)
        part = _dot(a, wd_ref[lo:hi, :])
        acc = part if acc is None else acc + part
    o_ref[...] = h + _rms(acc, post_ref[...])


def _ffn_chunks(f):
    step = 768
    return tuple((lo, min(lo + step, f)) for lo in range(0, f, step))


def _ffn(h, pre, w_gate, w_up, w_down, post):
    t, d = h.shape
    f = w_down.shape[0]
    tm = ROW_TILE
    kern = functools.partial(_ffn_kernel, chunks=_ffn_chunks(f))
    return pl.pallas_call(
        kern,
        grid=(t // tm,),
        in_specs=[
            pl.BlockSpec((tm, d), lambda i: (i, 0)),
            _const_spec((1, d)),
            _const_spec((d, f)),
            _const_spec((d, f)),
            _const_spec((f, d)),
            _const_spec((1, d)),
        ],
        out_specs=pl.BlockSpec((tm, d), lambda i: (i, 0)),
        out_shape=jax.ShapeDtypeStruct((t, d), F32),
        compiler_params=_params(("parallel",)),
        name="swiglu",
    )(h, pre, w_gate, w_up, w_down, post)


def _rope(t, cos_t, sin_up, sin_dn):
    outs = []
    for j in range(t.shape[1] // LANES):
        c = t[:, j * LANES:(j + 1) * LANES]
        outs.append(c * cos_t
                    + pltpu.roll(c, ROT_DIM // 2, axis=1) * sin_up
                    + pltpu.roll(c, LANES - ROT_DIM // 2, axis=1) * sin_dn)
    return jnp.concatenate(outs, axis=1)


def _qkv_kernel(h_ref, kvn_ref, wk_ref, wvt_ref, qn_ref, wq_ref, cos_ref, sup_ref, sdn_ref,
                q_ref, k_ref, vt_ref):
    h = h_ref[...]
    cos_t, sin_up, sin_dn = cos_ref[...], sup_ref[...], sdn_ref[...]
    hkv = _rms(h, kvn_ref[...]).astype(BF16)
    k = _dot(hkv, wk_ref[...])
    k_ref[...] = _rope(k, cos_t, sin_up, sin_dn).astype(BF16)
    vt_ref[...] = _dot_nt(wvt_ref[...], hkv).astype(BF16)
    hq = _rms(h, qn_ref[...]).astype(BF16)
    q = _dot(hq, wq_ref[...])
    q_ref[...] = (_rope(q, cos_t, sin_up, sin_dn) * (1.0 / math.sqrt(HEAD_DIM))).astype(BF16)


def _rope_tables(seq_len):
    half = ROT_DIM // 2
    inv_freq = ROPE_THETA ** (-jnp.arange(0, ROT_DIM, 2, dtype=F32) / ROT_DIM)
    ang = jnp.arange(seq_len, dtype=F32)[:, None] * inv_freq[None, :]
    cos, sin = jnp.cos(ang), jnp.sin(ang)
    ones = jnp.ones((seq_len, HEAD_DIM - ROT_DIM), F32)
    zeros = jnp.zeros((seq_len, HEAD_DIM - ROT_DIM), F32)
    zh = jnp.zeros((seq_len, half), F32)
    reps = LANES // HEAD_DIM
    cos_t = jnp.tile(jnp.concatenate([cos, cos, ones], axis=1), (1, reps))
    sin_up = jnp.tile(jnp.concatenate([zh, sin, zeros], axis=1), (1, reps))
    sin_dn = jnp.tile(jnp.concatenate([-sin, zh, zeros], axis=1), (1, reps))
    return cos_t, sin_up, sin_dn


def _q_head_order(n_heads):
    group = n_heads // N_KV_HEADS
    order = []
    for p in range(N_KV_HEADS // 2):
        for i in range(group):
            order += [(2 * p) * group + i, (2 * p + 1) * group + i]
    return order


def _qkv(h, kv_norm, w_k, w_vt, q_norm, w_q, tables, seq_len):
    t, d = h.shape
    kvd = w_k.shape[1]
    qd = w_q.shape[1]
    tm = ROW_TILE
    tps = seq_len // tm
    row = lambda i: (i, 0)
    pos = lambda i: (i % tps, 0)
    return pl.pallas_call(
        _qkv_kernel,
        grid=(t // tm,),
        in_specs=[
            pl.BlockSpec((tm, d), row),
            _const_spec((1, d)),
            _const_spec((d, kvd)),
            _const_spec((kvd, d)),
            _const_spec((1, d)),
            _const_spec((d, qd)),
            pl.BlockSpec((tm, LANES), pos),
            pl.BlockSpec((tm, LANES), pos),
            pl.BlockSpec((tm, LANES), pos),
        ],
        out_specs=[
            pl.BlockSpec((tm, qd), row),
            pl.BlockSpec((tm, kvd), row),
            pl.BlockSpec((kvd, tm), lambda i: (0, i)),
        ],
        out_shape=[
            jax.ShapeDtypeStruct((t, qd), BF16),
            jax.ShapeDtypeStruct((t, kvd), BF16),
            jax.ShapeDtypeStruct((kvd, t), BF16),
        ],
        compiler_params=_params(("parallel",)),
        name="qkv_proj",
    )(h, kv_norm, w_k, w_vt, q_norm, w_q, *tables)


def _attn_kernel(sink_ref, h_ref, q_ref, kp_ref, kc_ref, vtp_ref, vtc_ref, wo_ref, post_ref,
                 o_ref, att_t_ref, *, tiles_per_seq):
    tq = q_ref.shape[0]
    n_heads = q_ref.shape[1] // HEAD_DIM
    group = n_heads // N_KV_HEADS
    first_tile = (pl.program_id(0) % tiles_per_seq) == 0

    key_j = lax.broadcasted_iota(jnp.int32, (BLOCK, 2 * BLOCK), 0)
    lane = lax.broadcasted_iota(jnp.int32, (BLOCK, 2 * BLOCK), 1)
    from_prev = key_j > (lane % BLOCK)
    lo_half = lax.broadcasted_iota(jnp.int32, (BLOCK, LANES), 1) < HEAD_DIM
    left = lax.broadcasted_iota(jnp.int32, (1, 2 * BLOCK), 1) < BLOCK
    ones_rows = jnp.ones((BF16_ROWS, 2 * BLOCK), BF16)

    for blk in range(tq // BLOCK):
        r0 = blk * BLOCK
        if blk == 0:
            prev_bias = jnp.where(first_tile, NEG, 0.0).astype(F32)
        for p in range(N_KV_HEADS // 2):
            kv_lanes = slice(p * LANES, (p + 1) * LANES)
            if blk == 0:
                k_win = jnp.concatenate([kp_ref[:, kv_lanes], kc_ref[0:BLOCK, kv_lanes]], axis=0)
            else:
                k_win = kc_ref[r0 - BLOCK:r0 + BLOCK, kv_lanes]
            p_t, m_row = [], []
            for i in range(group):
                c = p * group + i
                qc = q_ref[r0:r0 + BLOCK, c * LANES:(c + 1) * LANES]
                zero = jnp.zeros_like(qc)
                q_split = jnp.concatenate([jnp.where(lo_half, qc, zero),
                                           jnp.where(lo_half, zero, qc)], axis=0)
                s_t = _dot_nt(k_win, q_split)
                s_prev = s_t[0:BLOCK]
                if blk == 0:
                    s_prev = s_prev + prev_bias
                s_win = jnp.where(from_prev, s_prev, s_t[BLOCK:])
                m = jnp.max(s_win, axis=0, keepdims=True)
                e = jnp.exp(s_win - m)
                p_t.append(jnp.concatenate([jnp.where(from_prev, e, 0.0),
                                            jnp.where(from_prev, 0.0, e)], axis=0).astype(BF16))
                m_row.append(m)
            for side in range(2):
                g = 2 * p + side
                lanes = slice(side * BLOCK, (side + 1) * BLOCK)
                rows = slice(g * HEAD_DIM, (g + 1) * HEAD_DIM)
                if blk == 0:
                    v_win = jnp.concatenate([vtp_ref[rows, :], vtc_ref[rows, 0:BLOCK]], axis=1)
                else:
                    v_win = vtc_ref[rows, r0 - BLOCK:r0 + BLOCK]
                v_aug = jnp.concatenate([v_win, ones_rows], axis=0)
                for i0 in range(0, group, 2):
                    pt_pair = jnp.concatenate([p_t[i0][:, lanes], p_t[i0 + 1][:, lanes]], axis=1)
                    m_pair = jnp.concatenate([m_row[i0][:, lanes], m_row[i0 + 1][:, lanes]], axis=1)
                    head = g * group + i0
                    sink = jnp.where(left, sink_ref[head], sink_ref[head + 1])
                    o_aug = _dot(v_aug, pt_pair)
                    denom = o_aug[HEAD_DIM:HEAD_DIM + 1, :] + jnp.exp(sink - m_pair)
                    o_t = (o_aug[0:HEAD_DIM, :] / denom).astype(BF16)
                    att_t_ref[head * HEAD_DIM:(head + 1) * HEAD_DIM, r0:r0 + BLOCK] = o_t[:, 0:BLOCK]
                    att_t_ref[(head + 1) * HEAD_DIM:(head + 2) * HEAD_DIM, r0:r0 + BLOCK] = o_t[:, BLOCK:]

    att = att_t_ref[...].T
    proj = _dot(att, wo_ref[...])
    o_ref[...] = h_ref[...] + _rms(proj, post_ref[...])


def _attention(h, q, k, v_t, sinks, w_o, post, seq_len):
    t, d = h.shape
    qd = q.shape[1]
    kvd = k.shape[1]
    tq = ROW_TILE
    tps = seq_len // tq
    bpt = tq // BLOCK
    row = lambda i: (i, 0)
    prev_blk = lambda i: jnp.maximum(i * bpt - 1, (i // tps) * tps * bpt)
    kern = functools.partial(_attn_kernel, tiles_per_seq=tps)
    return pl.pallas_call(
        kern,
        grid=(t // tq,),
        in_specs=[
            pl.BlockSpec(memory_space=pltpu.SMEM),
            pl.BlockSpec((tq, d), row),
            pl.BlockSpec((tq, qd), row),
            pl.BlockSpec((BLOCK, kvd), lambda i: (prev_blk(i), 0)),
            pl.BlockSpec((tq, kvd), row),
            pl.BlockSpec((kvd, BLOCK), lambda i: (0, prev_blk(i))),
            pl.BlockSpec((kvd, tq), lambda i: (0, i)),
            _const_spec((qd, d)),
            _const_spec((1, d)),
        ],
        out_specs=pl.BlockSpec((tq, d), row),
        out_shape=jax.ShapeDtypeStruct((t, d), F32),
        scratch_shapes=[pltpu.VMEM((qd, tq), BF16)],
        compiler_params=_params(("parallel",)),
        name="swa_attention",
    )(sinks, h, q, k, k, v_t, v_t, w_o, post)


def kernel(x, a_pre_norm, a_w_in, a_conv_w, a_w_out, a_post_norm, ffn_pre_norm, ffn_w_gate_up,
           ffn_w_down, ffn_post_norm, kv_norm, w_kv, b_pre_norm, b_w_q, b_sinks, b_w_o,
           b_post_norm):
    bsz, s, d = x.shape
    n_a = a_w_in.shape[0]
    depth = ffn_w_gate_up.shape[0]
    f = ffn_w_down.shape[1]
    kvd = w_kv.shape[1] // 2
    n_heads = b_w_q.shape[2] // HEAD_DIM
    tables = _rope_tables(s)
    vec = lambda g: g.reshape(1, -1)
    q_cols = np.asarray(_q_head_order(n_heads))[:, None] * HEAD_DIM + np.arange(HEAD_DIM)[None, :]
    q_cols = q_cols.reshape(-1)

    h = x.reshape(bsz * s, d)
    kv = None
    for l in range(depth):
        if l < n_a:
            h = _conv_mixer(h, vec(a_pre_norm[l]), a_w_in[l].astype(BF16), a_conv_w[l],
                            a_w_out[l].astype(BF16), vec(a_post_norm[l]), s)
        else:
            j = l - n_a
            w_k = w_kv[:, :kvd].astype(BF16)
            w_vt = w_kv[:, kvd:].T.astype(BF16)
            w_q = b_w_q[j][:, q_cols].astype(BF16)
            q, k_new, vt_new = _qkv(h, vec(kv_norm), w_k, w_vt, vec(b_pre_norm[j]), w_q, tables, s)
            if kv is None:
                kv = (k_new, vt_new)
            h = _attention(h, q, kv[0], kv[1], b_sinks[j], b_w_o[j].astype(BF16),
                           vec(b_post_norm[j]), s)
        w_gu = ffn_w_gate_up[l]
        h = _ffn(h, vec(ffn_pre_norm[l]), w_gu[:, :f].astype(BF16), w_gu[:, f:].astype(BF16),
                 ffn_w_down[l].astype(BF16), vec(ffn_post_norm[l]))
    return h.reshape(bsz, s, d)
```

```python
import functools
import math

import jax
import jax.numpy as jnp
import numpy as np
from jax import lax
from jax.experimental import pallas as pl
from jax.experimental.pallas import tpu as pltpu

CONV_W = 3
HEAD_DIM = 64
N_KV_HEADS = 4
WINDOW = 128
BLOCK = 128
ROT_DIM = HEAD_DIM // 4
ROPE_THETA = 500000.0
EPS = 1e-6
NEG = -1e30

LANES = 128
SUBLANES = 8
BF16_ROWS = 16
ROW_TILE = 512
VMEM_LIMIT = 56 * 1024 * 1024

F32 = jnp.float32
BF16 = jnp.bfloat16

assert WINDOW == BLOCK and LANES == 2 * HEAD_DIM


def _rms(x, g):
    r = lax.rsqrt(jnp.mean(x * x, axis=-1, keepdims=True) + EPS)
    return (x * r) * g


def _dot(a, b):
    return jnp.dot(a, b, preferred_element_type=F32)


def _dot_nt(a, b):
    return lax.dot_general(a, b, (((1,), (1,)), ((), ())), preferred_element_type=F32)


def _const_spec(shape):
    return pl.BlockSpec(shape, lambda *_: (0,) * len(shape), pipeline_mode=pl.Buffered(1))


def _params(semantics):
    return pltpu.CompilerParams(dimension_semantics=semantics, vmem_limit_bytes=VMEM_LIMIT)


def _conv_mixer_kernel(h_ref, pre_ref, wb_ref, wc_ref, wu_ref, cw_ref, wout_ref, post_ref, o_ref,
                       cu_ref, *, tiles_per_seq, sub_rows):
    tm, d = h_ref.shape
    i = pl.program_id(0)

    @pl.when(i % tiles_per_seq == 0)
    def _():
        cu_ref[0:SUBLANES, :] = jnp.zeros((SUBLANES, d), F32)

    @pl.when(i % tiles_per_seq != 0)
    def _():
        cu_ref[0:SUBLANES, :] = cu_ref[tm:tm + SUBLANES, :]

    cw = cw_ref[...]
    for r0 in range(0, tm, sub_rows):
        h = h_ref[r0:r0 + sub_rows, :]
        hn = _rms(h, pre_ref[...]).astype(BF16)
        cu = _dot(hn, wc_ref[...]) * _dot(hn, wu_ref[...])
        base = SUBLANES + r0
        cu_ref[base:base + sub_rows, :] = cu
        conv = (cu_ref[base - 2:base - 2 + sub_rows, :] * cw[0:1, :]
                + cu_ref[base - 1:base - 1 + sub_rows, :] * cw[1:2, :]
                + cu * cw[2:3, :])
        y = (_dot(hn, wb_ref[...]) * conv).astype(BF16)
        mix = _dot(y, wout_ref[...])
        o_ref[r0:r0 + sub_rows, :] = h + _rms(mix, post_ref[...])


def _conv_mixer(h, pre, w_b, w_c, w_u, conv_w, w_out, post, seq_len):
    t, d = h.shape
    tm = 2 * ROW_TILE
    kern = functools.partial(_conv_mixer_kernel, tiles_per_seq=seq_len // tm, sub_rows=ROW_TILE)
    return pl.pallas_call(
        kern,
        grid=(t // tm,),
        in_specs=[
            pl.BlockSpec((tm, d), lambda i: (i, 0)),
            _const_spec((1, d)),
            _const_spec((d, d)),
            _const_spec((d, d)),
            _const_spec((d, d)),
            _const_spec((CONV_W, d)),
            _const_spec((d, d)),
            _const_spec((1, d)),
        ],
        out_specs=pl.BlockSpec((tm, d), lambda i: (i, 0)),
        out_shape=jax.ShapeDtypeStruct((t, d), F32),
        scratch_shapes=[pltpu.VMEM((tm + SUBLANES, d), F32)],
        compiler_params=_params(("arbitrary",)),
        name="conv_mixer",
    )(h, pre, w_b, w_c, w_u, conv_w, w_out, post)


def _ffn_kernel(h_ref, pre_ref, wg_ref, wu_ref, wd_ref, post_ref, o_ref, *, chunks, sub_rows):
    for r0 in range(0, h_ref.shape[0], sub_rows):
        h = h_ref[r0:r0 + sub_rows, :]
        hn = _rms(h, pre_ref[...]).astype(BF16)
        acc = None
        for lo, hi in chunks:
            g = _dot(hn, wg_ref[:, lo:hi])
            u = _dot(hn, wu_ref[:, lo:hi])
            a = (g * jax.nn.sigmoid(g) * u).astype(BF16)
            part = _dot(a, wd_ref[lo:hi, :])
            acc = part if acc is None else acc + part
        o_ref[r0:r0 + sub_rows, :] = h + _rms(acc, post_ref[...])


def _ffn_chunks(f):
    step = 768
    return tuple((lo, min(lo + step, f)) for lo in range(0, f, step))


def _ffn(h, pre, w_gate, w_up, w_down, post):
    t, d = h.shape
    f = w_down.shape[0]
    tm = 2 * ROW_TILE
    kern = functools.partial(_ffn_kernel, chunks=_ffn_chunks(f), sub_rows=ROW_TILE)
    return pl.pallas_call(
        kern,
        grid=(t // tm,),
        in_specs=[
            pl.BlockSpec((tm, d), lambda i: (i, 0)),
            _const_spec((1, d)),
            _const_spec((d, f)),
            _const_spec((d, f)),
            _const_spec((f, d)),
            _const_spec((1, d)),
        ],
        out_specs=pl.BlockSpec((tm, d), lambda i: (i, 0)),
        out_shape=jax.ShapeDtypeStruct((t, d), F32),
        compiler_params=_params(("parallel",)),
        name="swiglu",
    )(h, pre, w_gate, w_up, w_down, post)


def _rope(t, cos_t, sin_up, sin_dn):
    outs = []
    for j in range(t.shape[1] // LANES):
        c = t[:, j * LANES:(j + 1) * LANES]
        outs.append(c * cos_t
                    + pltpu.roll(c, ROT_DIM // 2, axis=1) * sin_up
                    + pltpu.roll(c, LANES - ROT_DIM // 2, axis=1) * sin_dn)
    return jnp.concatenate(outs, axis=1)


def _qkv_kernel(h_ref, kvn_ref, wk_ref, wvt_ref, qn_ref, wq_ref, cos_ref, sup_ref, sdn_ref,
                q_ref, k_ref, vt_ref):
    h = h_ref[...]
    cos_t, sin_up, sin_dn = cos_ref[...], sup_ref[...], sdn_ref[...]
    hkv = _rms(h, kvn_ref[...]).astype(BF16)
    k = _dot(hkv, wk_ref[...])
    k_ref[...] = _rope(k, cos_t, sin_up, sin_dn).astype(BF16)
    vt_ref[...] = _dot_nt(wvt_ref[...], hkv).astype(BF16)
    hq = _rms(h, qn_ref[...]).astype(BF16)
    q = _dot(hq, wq_ref[...])
    q_ref[...] = (_rope(q, cos_t, sin_up, sin_dn) * (1.0 / math.sqrt(HEAD_DIM))).astype(BF16)


def _rope_tables(seq_len):
    half = ROT_DIM // 2
    inv_freq = ROPE_THETA ** (-jnp.arange(0, ROT_DIM, 2, dtype=F32) / ROT_DIM)
    ang = jnp.arange(seq_len, dtype=F32)[:, None] * inv_freq[None, :]
    cos, sin = jnp.cos(ang), jnp.sin(ang)
    ones = jnp.ones((seq_len, HEAD_DIM - ROT_DIM), F32)
    zeros = jnp.zeros((seq_len, HEAD_DIM - ROT_DIM), F32)
    zh = jnp.zeros((seq_len, half), F32)
    reps = LANES // HEAD_DIM
    cos_t = jnp.tile(jnp.concatenate([cos, cos, ones], axis=1), (1, reps))
    sin_up = jnp.tile(jnp.concatenate([zh, sin, zeros], axis=1), (1, reps))
    sin_dn = jnp.tile(jnp.concatenate([-sin, zh, zeros], axis=1), (1, reps))
    return cos_t, sin_up, sin_dn


def _q_head_order(n_heads):
    group = n_heads // N_KV_HEADS
    order = []
    for p in range(N_KV_HEADS // 2):
        for i in range(group):
            order += [(2 * p) * group + i, (2 * p + 1) * group + i]
    return order


def _qkv(h, kv_norm, w_k, w_vt, q_norm, w_q, tables, seq_len):
    t, d = h.shape
    kvd = w_k.shape[1]
    qd = w_q.shape[1]
    tm = ROW_TILE
    tps = seq_len // tm
    row = lambda i: (i, 0)
    pos = lambda i: (i % tps, 0)
    return pl.pallas_call(
        _qkv_kernel,
        grid=(t // tm,),
        in_specs=[
            pl.BlockSpec((tm, d), row),
            _const_spec((1, d)),
            _const_spec((d, kvd)),
            _const_spec((kvd, d)),
            _const_spec((1, d)),
            _const_spec((d, qd)),
            pl.BlockSpec((tm, LANES), pos),
            pl.BlockSpec((tm, LANES), pos),
            pl.BlockSpec((tm, LANES), pos),
        ],
        out_specs=[
            pl.BlockSpec((tm, qd), row),
            pl.BlockSpec((tm, kvd), row),
            pl.BlockSpec((kvd, tm), lambda i: (0, i)),
        ],
        out_shape=[
            jax.ShapeDtypeStruct((t, qd), BF16),
            jax.ShapeDtypeStruct((t, kvd), BF16),
            jax.ShapeDtypeStruct((kvd, t), BF16),
        ],
        compiler_params=_params(("parallel",)),
        name="qkv_proj",
    )(h, kv_norm, w_k, w_vt, q_norm, w_q, *tables)


def _attn_kernel(sink_ref, h_ref, q_ref, kp_ref, kc_ref, vtp_ref, vtc_ref, wo_ref, post_ref,
                 o_ref, att_t_ref, *, tiles_per_seq):
    tq = q_ref.shape[0]
    n_heads = q_ref.shape[1] // HEAD_DIM
    group = n_heads // N_KV_HEADS
    first_tile = (pl.program_id(0) % tiles_per_seq) == 0

    key_j = lax.broadcasted_iota(jnp.int32, (BLOCK, 2 * BLOCK), 0)
    lane = lax.broadcasted_iota(jnp.int32, (BLOCK, 2 * BLOCK), 1)
    from_prev = key_j > (lane % BLOCK)
    lo_half = lax.broadcasted_iota(jnp.int32, (BLOCK, LANES), 1) < HEAD_DIM
    left = lax.broadcasted_iota(jnp.int32, (1, 2 * BLOCK), 1) < BLOCK
    ones_rows = jnp.ones((BF16_ROWS, 2 * BLOCK), BF16)

    for blk in range(tq // BLOCK):
        r0 = blk * BLOCK
        if blk == 0:
            prev_bias = jnp.where(first_tile, NEG, 0.0).astype(F32)
        for p in range(N_KV_HEADS // 2):
            kv_lanes = slice(p * LANES, (p + 1) * LANES)
            if blk == 0:
                k_win = jnp.concatenate([kp_ref[:, kv_lanes], kc_ref[0:BLOCK, kv_lanes]], axis=0)
            else:
                k_win = kc_ref[r0 - BLOCK:r0 + BLOCK, kv_lanes]
            p_t, m_row = [], []
            for i in range(group):
                c = p * group + i
                qc = q_ref[r0:r0 + BLOCK, c * LANES:(c + 1) * LANES]
                zero = jnp.zeros_like(qc)
                q_split = jnp.concatenate([jnp.where(lo_half, qc, zero),
                                           jnp.where(lo_half, zero, qc)], axis=0)
                s_t = _dot_nt(k_win, q_split)
                s_prev = s_t[0:BLOCK]
                if blk == 0:
                    s_prev = s_prev + prev_bias
                s_win = jnp.where(from_prev, s_prev, s_t[BLOCK:])
                m = jnp.max(s_win, axis=0, keepdims=True)
                e = jnp.exp(s_win - m)
                p_t.append(jnp.concatenate([jnp.where(from_prev, e, 0.0),
                                            jnp.where(from_prev, 0.0, e)], axis=0).astype(BF16))
                m_row.append(m)
            for side in range(2):
                g = 2 * p + side
                lanes = slice(side * BLOCK, (side + 1) * BLOCK)
                rows = slice(g * HEAD_DIM, (g + 1) * HEAD_DIM)
                if blk == 0:
                    v_win = jnp.concatenate([vtp_ref[rows, :], vtc_ref[rows, 0:BLOCK]], axis=1)
                else:
                    v_win = vtc_ref[rows, r0 - BLOCK:r0 + BLOCK]
                v_aug = jnp.concatenate([v_win, ones_rows], axis=0)
                for i0 in range(0, group, 2):
                    pt_pair = jnp.concatenate([p_t[i0][:, lanes], p_t[i0 + 1][:, lanes]], axis=1)
                    m_pair = jnp.concatenate([m_row[i0][:, lanes], m_row[i0 + 1][:, lanes]], axis=1)
                    head = g * group + i0
                    sink = jnp.where(left, sink_ref[head], sink_ref[head + 1])
                    o_aug = _dot(v_aug, pt_pair)
                    denom = o_aug[HEAD_DIM:HEAD_DIM + 1, :] + jnp.exp(sink - m_pair)
                    o_t = (o_aug[0:HEAD_DIM, :] / denom).astype(BF16)
                    att_t_ref[head * HEAD_DIM:(head + 1) * HEAD_DIM, r0:r0 + BLOCK] = o_t[:, 0:BLOCK]
                    att_t_ref[(head + 1) * HEAD_DIM:(head + 2) * HEAD_DIM, r0:r0 + BLOCK] = o_t[:, BLOCK:]

    att = att_t_ref[...].T
    proj = _dot(att, wo_ref[...])
    o_ref[...] = h_ref[...] + _rms(proj, post_ref[...])


def _attention(h, q, k, v_t, sinks, w_o, post, seq_len):
    t, d = h.shape
    qd = q.shape[1]
    kvd = k.shape[1]
    tq = ROW_TILE
    tps = seq_len // tq
    bpt = tq // BLOCK
    row = lambda i: (i, 0)
    prev_blk = lambda i: jnp.maximum(i * bpt - 1, (i // tps) * tps * bpt)
    kern = functools.partial(_attn_kernel, tiles_per_seq=tps)
    return pl.pallas_call(
        kern,
        grid=(t // tq,),
        in_specs=[
            pl.BlockSpec(memory_space=pltpu.SMEM),
            pl.BlockSpec((tq, d), row),
            pl.BlockSpec((tq, qd), row),
            pl.BlockSpec((BLOCK, kvd), lambda i: (prev_blk(i), 0)),
            pl.BlockSpec((tq, kvd), row),
            pl.BlockSpec((kvd, BLOCK), lambda i: (0, prev_blk(i))),
            pl.BlockSpec((kvd, tq), lambda i: (0, i)),
            _const_spec((qd, d)),
            _const_spec((1, d)),
        ],
        out_specs=pl.BlockSpec((tq, d), row),
        out_shape=jax.ShapeDtypeStruct((t, d), F32),
        scratch_shapes=[pltpu.VMEM((qd, tq), BF16)],
        compiler_params=_params(("parallel",)),
        name="swa_attention",
    )(sinks, h, q, k, k, v_t, v_t, w_o, post)


def kernel(x, a_pre_norm, a_w_in, a_conv_w, a_w_out, a_post_norm, ffn_pre_norm, ffn_w_gate_up,
           ffn_w_down, ffn_post_norm, kv_norm, w_kv, b_pre_norm, b_w_q, b_sinks, b_w_o,
           b_post_norm):
    bsz, s, d = x.shape
    n_a = a_w_in.shape[0]
    depth = ffn_w_gate_up.shape[0]
    f = ffn_w_down.shape[1]
    kvd = w_kv.shape[1] // 2
    n_heads = b_w_q.shape[2] // HEAD_DIM
    tables = _rope_tables(s)
    vec = lambda g: g.reshape(1, -1)
    q_cols = np.asarray(_q_head_order(n_heads))[:, None] * HEAD_DIM + np.arange(HEAD_DIM)[None, :]
    q_cols = q_cols.reshape(-1)

    h = x.reshape(bsz * s, d)
    kv = None
    for l in range(depth):
        if l < n_a:
            w_in = a_w_in[l]
            h = _conv_mixer(h, vec(a_pre_norm[l]), w_in[:, :d].astype(BF16),
                            w_in[:, d:2 * d].astype(BF16), w_in[:, 2 * d:].astype(BF16),
                            a_conv_w[l], a_w_out[l].astype(BF16), vec(a_post_norm[l]), s)
        else:
            j = l - n_a
            w_k = w_kv[:, :kvd].astype(BF16)
            w_vt = w_kv[:, kvd:].T.astype(BF16)
            w_q = b_w_q[j][:, q_cols].astype(BF16)
            q, k_new, vt_new = _qkv(h, vec(kv_norm), w_k, w_vt, vec(b_pre_norm[j]), w_q, tables, s)
            if kv is None:
                kv = (k_new, vt_new)
            h = _attention(h, q, kv[0], kv[1], b_sinks[j], b_w_o[j].astype(BF16),
                           vec(b_post_norm[j]), s)
        w_gu = ffn_w_gate_up[l]
        h = _ffn(h, vec(ffn_pre_norm[l]), w_gu[:, :f].astype(BF16), w_gu[:, f:].astype(BF16),
                 ffn_w_down[l].astype(BF16), vec(ffn_post_norm[l]))
    return h.reshape(bsz, s, d)
```

```python
import functools
import math

import jax
import jax.numpy as jnp
import numpy as np
from jax import lax
from jax.experimental import pallas as pl
from jax.experimental.pallas import tpu as pltpu

CONV_W = 3
HEAD_DIM = 64
N_KV_HEADS = 4
WINDOW = 128
BLOCK = 128
ROT_DIM = HEAD_DIM // 4
ROPE_THETA = 500000.0
EPS = 1e-6
NEG = -1e30

LANES = 128
SUBLANES = 8
BF16_ROWS = 16
ROW_TILE = 512
VMEM_LIMIT = 56 * 1024 * 1024

F32 = jnp.float32
BF16 = jnp.bfloat16

assert WINDOW == BLOCK and LANES == 2 * HEAD_DIM


def _rms(x, g):
    r = lax.rsqrt(jnp.mean(x * x, axis=-1, keepdims=True) + EPS)
    return (x * r) * g


def _dot(a, b):
    return jnp.dot(a, b, preferred_element_type=F32)


def _dot_nt(a, b):
    return lax.dot_general(a, b, (((1,), (1,)), ((), ())), preferred_element_type=F32)


def _const_spec(shape):
    return pl.BlockSpec(shape, lambda *_: (0,) * len(shape), pipeline_mode=pl.Buffered(1))


def _params(semantics):
    return pltpu.CompilerParams(dimension_semantics=semantics, vmem_limit_bytes=VMEM_LIMIT)


def _rope(t, cos_t, sin_up, sin_dn):
    outs = []
    for j in range(t.shape[1] // LANES):
        c = t[:, j * LANES:(j + 1) * LANES]
        outs.append(c * cos_t
                    + pltpu.roll(c, ROT_DIM // 2, axis=1) * sin_up
                    + pltpu.roll(c, LANES - ROT_DIM // 2, axis=1) * sin_dn)
    return jnp.concatenate(outs, axis=1)


def _rope_tables(seq_len):
    half = ROT_DIM // 2
    inv_freq = ROPE_THETA ** (-jnp.arange(0, ROT_DIM, 2, dtype=F32) / ROT_DIM)
    ang = jnp.arange(seq_len, dtype=F32)[:, None] * inv_freq[None, :]
    cos, sin = jnp.cos(ang), jnp.sin(ang)
    ones = jnp.ones((seq_len, HEAD_DIM - ROT_DIM), F32)
    zeros = jnp.zeros((seq_len, HEAD_DIM - ROT_DIM), F32)
    zh = jnp.zeros((seq_len, half), F32)
    reps = LANES // HEAD_DIM
    cos_t = jnp.tile(jnp.concatenate([cos, cos, ones], axis=1), (1, reps))
    sin_up = jnp.tile(jnp.concatenate([zh, sin, zeros], axis=1), (1, reps))
    sin_dn = jnp.tile(jnp.concatenate([-sin, zh, zeros], axis=1), (1, reps))
    return cos_t, sin_up, sin_dn


def _conv_mixer_kernel(h_ref, pre_ref, win_ref, cw_ref, wout_ref, post_ref, o_ref, cu_ref,
                       *, tiles_per_seq, sub_rows):
    tm, d = h_ref.shape
    i = pl.program_id(0)

    @pl.when(i % tiles_per_seq == 0)
    def _():
        cu_ref[0:SUBLANES, :] = jnp.zeros((SUBLANES, d), F32)

    @pl.when(i % tiles_per_seq != 0)
    def _():
        cu_ref[0:SUBLANES, :] = cu_ref[tm:tm + SUBLANES, :]

    cw = cw_ref[...]
    for r0 in range(0, tm, sub_rows):
        h = h_ref[r0:r0 + sub_rows, :]
        hn = _rms(h, pre_ref[...])
        cu = _dot(hn, win_ref[:, d:2 * d]) * _dot(hn, win_ref[:, 2 * d:3 * d])
        base = SUBLANES + r0
        cu_ref[base:base + sub_rows, :] = cu
        conv = (cu_ref[base - 2:base - 2 + sub_rows, :] * cw[0:1, :]
                + cu_ref[base - 1:base - 1 + sub_rows, :] * cw[1:2, :]
                + cu * cw[2:3, :])
        y = _dot(hn, win_ref[:, 0:d]) * conv
        mix = _dot(y, wout_ref[...])
        o_ref[r0:r0 + sub_rows, :] = h + _rms(mix, post_ref[...])


def _conv_mixer(h, pre, w_in, conv_w, w_out, post, seq_len):
    t, d = h.shape
    tm = 2 * ROW_TILE
    kern = functools.partial(_conv_mixer_kernel, tiles_per_seq=seq_len // tm, sub_rows=ROW_TILE)
    return pl.pallas_call(
        kern,
        grid=(t // tm,),
        in_specs=[
            pl.BlockSpec((tm, d), lambda i: (i, 0)),
            _const_spec((1, d)),
            _const_spec((d, 3 * d)),
            _const_spec((CONV_W, d)),
            _const_spec((d, d)),
            _const_spec((1, d)),
        ],
        out_specs=pl.BlockSpec((tm, d), lambda i: (i, 0)),
        out_shape=jax.ShapeDtypeStruct((t, d), F32),
        scratch_shapes=[pltpu.VMEM((tm + SUBLANES, d), F32)],
        compiler_params=_params(("arbitrary",)),
        name="conv_mixer",
    )(h, pre, w_in, conv_w, w_out, post)


def _swiglu_rows(h, pre_ref, wg_ref, wu_ref, wd_ref, post_ref, chunks):
    hn = _rms(h, pre_ref[...]).astype(BF16)
    acc = None
    for lo, hi in chunks:
        g = _dot(hn, wg_ref[:, lo:hi])
        u = _dot(hn, wu_ref[:, lo:hi])
        a = (g * jax.nn.sigmoid(g) * u).astype(BF16)
        part = _dot(a, wd_ref[lo:hi, :])
        acc = part if acc is None else acc + part
    return h + _rms(acc, post_ref[...])


def _ffn_kernel(h_ref, pre_ref, wg_ref, wu_ref, wd_ref, post_ref, o_ref, *, chunks, sub_rows):
    for r0 in range(0, h_ref.shape[0], sub_rows):
        rows = slice(r0, r0 + sub_rows)
        o_ref[rows, :] = _swiglu_rows(h_ref[rows, :], pre_ref, wg_ref, wu_ref, wd_ref, post_ref,
                                      chunks)


def _ffn_kv_kernel(h_ref, pre_ref, wg_ref, wu_ref, wd_ref, post_ref, kvn_ref, wk_ref, wvt_ref,
                   cos_ref, sup_ref, sdn_ref, o_ref, k_ref, vt_ref, *, chunks, sub_rows):
    for r0 in range(0, h_ref.shape[0], sub_rows):
        rows = slice(r0, r0 + sub_rows)
        out = _swiglu_rows(h_ref[rows, :], pre_ref, wg_ref, wu_ref, wd_ref, post_ref, chunks)
        o_ref[rows, :] = out
        hkv = _rms(out, kvn_ref[...])
        k = _rope(_dot(hkv, wk_ref[...]), cos_ref[rows, :], sup_ref[rows, :], sdn_ref[rows, :])
        k_ref[rows, :] = k.astype(BF16)
        vt_ref[:, rows] = _dot_nt(wvt_ref[...], hkv).astype(BF16)


def _ffn_chunks(f):
    step = 768
    return tuple((lo, min(lo + step, f)) for lo in range(0, f, step))


def _ffn(h, pre, w_gate, w_up, w_down, post, kv_proj=None):
    t, d = h.shape
    f = w_down.shape[0]
    tm = 2 * ROW_TILE
    row = lambda i: (i, 0)
    in_specs = [
        pl.BlockSpec((tm, d), row),
        _const_spec((1, d)),
        _const_spec((d, f)),
        _const_spec((d, f)),
        _const_spec((f, d)),
        _const_spec((1, d)),
    ]
    args = [h, pre, w_gate, w_up, w_down, post]
    out_specs = pl.BlockSpec((tm, d), row)
    out_shape = jax.ShapeDtypeStruct((t, d), F32)
    static = dict(chunks=_ffn_chunks(f), sub_rows=ROW_TILE)
    if kv_proj is None:
        kern = functools.partial(_ffn_kernel, **static)
    else:
        kv_norm, w_k, w_vt, tables, seq_len = kv_proj
        kvd = w_k.shape[1]
        tps = seq_len // tm
        pos = lambda i: (i % tps, 0)
        kern = functools.partial(_ffn_kv_kernel, **static)
        in_specs += [_const_spec((1, d)), _const_spec((d, kvd)), _const_spec((kvd, d))]
        in_specs += [pl.BlockSpec((tm, LANES), pos)] * 3
        args += [kv_norm, w_k, w_vt, *tables]
        out_specs = [out_specs, pl.BlockSpec((tm, kvd), row), pl.BlockSpec((kvd, tm), lambda i: (0, i))]
        out_shape = [out_shape, jax.ShapeDtypeStruct((t, kvd), BF16),
                     jax.ShapeDtypeStruct((kvd, t), BF16)]
    return pl.pallas_call(
        kern,
        grid=(t // tm,),
        in_specs=in_specs,
        out_specs=out_specs,
        out_shape=out_shape,
        compiler_params=_params(("parallel",)),
        name="swiglu",
    )(*args)


def _q_head_order(n_heads):
    group = n_heads // N_KV_HEADS
    order = []
    for p in range(N_KV_HEADS // 2):
        for i in range(group):
            order += [(2 * p) * group + i, (2 * p + 1) * group + i]
    return order


def _attn_kernel(sink_ref, h_ref, qn_ref, wq_ref, cos_ref, sup_ref, sdn_ref, kp_ref, kc_ref,
                 vtp_ref, vtc_ref, wo_ref, post_ref, o_ref, q_ref, att_t_ref, *, tiles_per_seq):
    tq = q_ref.shape[0]
    n_heads = q_ref.shape[1] // HEAD_DIM
    group = n_heads // N_KV_HEADS
    first_tile = (pl.program_id(0) % tiles_per_seq) == 0

    h = h_ref[...]
    q = _rope(_dot(_rms(h, qn_ref[...]), wq_ref[...]), cos_ref[...], sup_ref[...], sdn_ref[...])
    q_ref[...] = (q * (1.0 / math.sqrt(HEAD_DIM))).astype(BF16)

    key_j = lax.broadcasted_iota(jnp.int32, (BLOCK, 2 * BLOCK), 0)
    lane = lax.broadcasted_iota(jnp.int32, (BLOCK, 2 * BLOCK), 1)
    from_prev = key_j > (lane % BLOCK)
    lo_half = lax.broadcasted_iota(jnp.int32, (BLOCK, LANES), 1) < HEAD_DIM
    left = lax.broadcasted_iota(jnp.int32, (1, 2 * BLOCK), 1) < BLOCK
    ones_rows = jnp.ones((BF16_ROWS, 2 * BLOCK), BF16)

    for blk in range(tq // BLOCK):
        r0 = blk * BLOCK
        if blk == 0:
            prev_bias = jnp.where(first_tile, NEG, 0.0).astype(F32)
        for p in range(N_KV_HEADS // 2):
            kv_lanes = slice(p * LANES, (p + 1) * LANES)
            if blk == 0:
                k_win = jnp.concatenate([kp_ref[:, kv_lanes], kc_ref[0:BLOCK, kv_lanes]], axis=0)
            else:
                k_win = kc_ref[r0 - BLOCK:r0 + BLOCK, kv_lanes]
            p_t, m_row = [], []
            for i in range(group):
                c = p * group + i
                qc = q_ref[r0:r0 + BLOCK, c * LANES:(c + 1) * LANES]
                zero = jnp.zeros_like(qc)
                q_split = jnp.concatenate([jnp.where(lo_half, qc, zero),
                                           jnp.where(lo_half, zero, qc)], axis=0)
                s_t = _dot_nt(k_win, q_split)
                s_prev = s_t[0:BLOCK]
                if blk == 0:
                    s_prev = s_prev + prev_bias
                s_win = jnp.where(from_prev, s_prev, s_t[BLOCK:])
                m = jnp.max(s_win, axis=0, keepdims=True)
                e = jnp.exp(s_win - m)
                p_t.append(jnp.concatenate([jnp.where(from_prev, e, 0.0),
                                            jnp.where(from_prev, 0.0, e)], axis=0).astype(BF16))
                m_row.append(m)
            for side in range(2):
                g = 2 * p + side
                lanes = slice(side * BLOCK, (side + 1) * BLOCK)
                rows = slice(g * HEAD_DIM, (g + 1) * HEAD_DIM)
                if blk == 0:
                    v_win = jnp.concatenate([vtp_ref[rows, :], vtc_ref[rows, 0:BLOCK]], axis=1)
                else:
                    v_win = vtc_ref[rows, r0 - BLOCK:r0 + BLOCK]
                v_aug = jnp.concatenate([v_win, ones_rows], axis=0)
                for i0 in range(0, group, 2):
                    pt_pair = jnp.concatenate([p_t[i0][:, lanes], p_t[i0 + 1][:, lanes]], axis=1)
                    m_pair = jnp.concatenate([m_row[i0][:, lanes], m_row[i0 + 1][:, lanes]], axis=1)
                    head = g * group + i0
                    sink = jnp.where(left, sink_ref[head], sink_ref[head + 1])
                    o_aug = _dot(v_aug, pt_pair)
                    denom = o_aug[HEAD_DIM:HEAD_DIM + 1, :] + jnp.exp(sink - m_pair)
                    o_t = (o_aug[0:HEAD_DIM, :] / denom).astype(BF16)
                    att_t_ref[head * HEAD_DIM:(head + 1) * HEAD_DIM, r0:r0 + BLOCK] = o_t[:, 0:BLOCK]
                    att_t_ref[(head + 1) * HEAD_DIM:(head + 2) * HEAD_DIM, r0:r0 + BLOCK] = o_t[:, BLOCK:]

    att = att_t_ref[...].T
    proj = _dot(att, wo_ref[...])
    o_ref[...] = h + _rms(proj, post_ref[...])


def _attention(h, q_norm, w_q, tables, k, v_t, sinks, w_o, post, seq_len):
    t, d = h.shape
    qd = w_q.shape[1]
    kvd = k.shape[1]
    tq = ROW_TILE
    tps = seq_len // tq
    bpt = tq // BLOCK
    row = lambda i: (i, 0)
    pos = lambda i: (i % tps, 0)
    prev_blk = lambda i: jnp.maximum(i * bpt - 1, (i // tps) * tps * bpt)
    kern = functools.partial(_attn_kernel, tiles_per_seq=tps)
    return pl.pallas_call(
        kern,
        grid=(t // tq,),
        in_specs=[
            pl.BlockSpec(memory_space=pltpu.SMEM),
            pl.BlockSpec((tq, d), row),
            _const_spec((1, d)),
            _const_spec((d, qd)),
            pl.BlockSpec((tq, LANES), pos),
            pl.BlockSpec((tq, LANES), pos),
            pl.BlockSpec((tq, LANES), pos),
            pl.BlockSpec((BLOCK, kvd), lambda i: (prev_blk(i), 0)),
            pl.BlockSpec((tq, kvd), row),
            pl.BlockSpec((kvd, BLOCK), lambda i: (0, prev_blk(i))),
            pl.BlockSpec((kvd, tq), lambda i: (0, i)),
            _const_spec((qd, d)),
            _const_spec((1, d)),
        ],
        out_specs=pl.BlockSpec((tq, d), row),
        out_shape=jax.ShapeDtypeStruct((t, d), F32),
        scratch_shapes=[pltpu.VMEM((tq, qd), BF16), pltpu.VMEM((qd, tq), BF16)],
        compiler_params=_params(("parallel",)),
        name="swa_attention",
    )(sinks, h, q_norm, w_q, *tables, k, k, v_t, v_t, w_o, post)


def kernel(x, a_pre_norm, a_w_in, a_conv_w, a_w_out, a_post_norm, ffn_pre_norm, ffn_w_gate_up,
           ffn_w_down, ffn_post_norm, kv_norm, w_kv, b_pre_norm, b_w_q, b_sinks, b_w_o,
           b_post_norm):
    bsz, s, d = x.shape
    n_a = a_w_in.shape[0]
    depth = ffn_w_gate_up.shape[0]
    f = ffn_w_down.shape[1]
    kvd = w_kv.shape[1] // 2
    n_heads = b_w_q.shape[2] // HEAD_DIM
    assert 1 <= n_a < depth, "the shared K/V is produced by the last short-conv layer's SwiGLU"
    tables = _rope_tables(s)
    vec = lambda g: g.reshape(1, -1)
    q_cols = np.asarray(_q_head_order(n_heads))[:, None] * HEAD_DIM + np.arange(HEAD_DIM)[None, :]
    q_cols = q_cols.reshape(-1)

    h = x.reshape(bsz * s, d)
    k = v_t = None
    for l in range(depth):
        w_gu = ffn_w_gate_up[l]
        ffn_args = (vec(ffn_pre_norm[l]), w_gu[:, :f].astype(BF16), w_gu[:, f:].astype(BF16),
                    ffn_w_down[l].astype(BF16), vec(ffn_post_norm[l]))
        if l < n_a:
            h = _conv_mixer(h, vec(a_pre_norm[l]), a_w_in[l], a_conv_w[l], a_w_out[l],
                            vec(a_post_norm[l]), s)
            if l == n_a - 1:
                kv_proj = (vec(kv_norm), w_kv[:, :kvd], w_kv[:, kvd:].T, tables, s)
                h, k, v_t = _ffn(h, *ffn_args, kv_proj=kv_proj)
            else:
                h = _ffn(h, *ffn_args)
        else:
            j = l - n_a
            h = _attention(h, vec(b_pre_norm[j]), b_w_q[j][:, q_cols], tables, k, v_t, b_sinks[j],
                           b_w_o[j].astype(BF16), vec(b_post_norm[j]), s)
            h = _ffn(h, *ffn_args)
    return h.reshape(bsz, s, d)
```

```python
import functools
import math

import jax
import jax.numpy as jnp
import numpy as np
from jax import lax
from jax.experimental import pallas as pl
from jax.experimental.pallas import tpu as pltpu

CONV_W = 3
HEAD_DIM = 64
N_KV_HEADS = 4
WINDOW = 128
BLOCK = 128
ROT_DIM = HEAD_DIM // 4
ROPE_THETA = 500000.0
EPS = 1e-6
NEG = -1e30

LANES = 128
SUBLANES = 8
BF16_ROWS = 16
ROW_TILE = 512
FFN_TILE = 512
FFN_SUB_ROWS = 512
VMEM_LIMIT = 60 * 1024 * 1024

F32 = jnp.float32
BF16 = jnp.bfloat16

assert WINDOW == BLOCK and LANES == 2 * HEAD_DIM


def _rms(x, g):
    r = lax.rsqrt(jnp.mean(x * x, axis=-1, keepdims=True) + EPS)
    return (x * r) * g


def _dot(a, b):
    return jnp.dot(a, b, preferred_element_type=F32)


def _dot_nt(a, b):
    return lax.dot_general(a, b, (((1,), (1,)), ((), ())), preferred_element_type=F32)


def _const_spec(shape):
    return pl.BlockSpec(shape, lambda *_: (0,) * len(shape), pipeline_mode=pl.Buffered(1))


def _layer_spec(shape, layer):
    return pl.BlockSpec((None,) + tuple(shape), lambda *_: (layer,) + (0,) * len(shape),
                        pipeline_mode=pl.Buffered(1))


def _params(semantics):
    return pltpu.CompilerParams(dimension_semantics=semantics, vmem_limit_bytes=VMEM_LIMIT)


def _rope(t, cos_t, sin_up, sin_dn):
    outs = []
    for j in range(t.shape[1] // LANES):
        c = t[:, j * LANES:(j + 1) * LANES]
        outs.append(c * cos_t
                    + pltpu.roll(c, ROT_DIM // 2, axis=1) * sin_up
                    + pltpu.roll(c, LANES - ROT_DIM // 2, axis=1) * sin_dn)
    return jnp.concatenate(outs, axis=1)


def _rope_tables(seq_len):
    half = ROT_DIM // 2
    inv_freq = ROPE_THETA ** (-jnp.arange(0, ROT_DIM, 2, dtype=F32) / ROT_DIM)
    ang = jnp.arange(seq_len, dtype=F32)[:, None] * inv_freq[None, :]
    cos, sin = jnp.cos(ang), jnp.sin(ang)
    ones = jnp.ones((seq_len, HEAD_DIM - ROT_DIM), F32)
    zeros = jnp.zeros((seq_len, HEAD_DIM - ROT_DIM), F32)
    zh = jnp.zeros((seq_len, half), F32)
    reps = LANES // HEAD_DIM
    cos_t = jnp.tile(jnp.concatenate([cos, cos, ones], axis=1), (1, reps))
    sin_up = jnp.tile(jnp.concatenate([zh, sin, zeros], axis=1), (1, reps))
    sin_dn = jnp.tile(jnp.concatenate([-sin, zh, zeros], axis=1), (1, reps))
    return cos_t, sin_up, sin_dn


def _conv_mixer_kernel(h_ref, pre_ref, win_ref, cw_ref, wout_ref, post_ref, o_ref, cu_ref,
                       *, tiles_per_seq, sub_rows):
    tm, d = h_ref.shape
    i = pl.program_id(0)

    @pl.when(i % tiles_per_seq == 0)
    def _():
        cu_ref[0:SUBLANES, :] = jnp.zeros((SUBLANES, d), F32)

    @pl.when(i % tiles_per_seq != 0)
    def _():
        cu_ref[0:SUBLANES, :] = cu_ref[tm:tm + SUBLANES, :]

    cw = cw_ref[...]
    for r0 in range(0, tm, sub_rows):
        h = h_ref[r0:r0 + sub_rows, :]
        hn = _rms(h, pre_ref[...])
        cu = _dot(hn, win_ref[:, d:2 * d]) * _dot(hn, win_ref[:, 2 * d:3 * d])
        base = SUBLANES + r0
        cu_ref[base:base + sub_rows, :] = cu
        conv = (cu_ref[base - 2:base - 2 + sub_rows, :] * cw[0:1, :]
                + cu_ref[base - 1:base - 1 + sub_rows, :] * cw[1:2, :]
                + cu * cw[2:3, :])
        y = _dot(hn, win_ref[:, 0:d]) * conv
        mix = _dot(y, wout_ref[...])
        o_ref[r0:r0 + sub_rows, :] = h + _rms(mix, post_ref[...])


def _conv_mixer(h, pre, w_in, conv_w, w_out, post, layer, seq_len):
    t, d = h.shape
    tm = 2 * ROW_TILE
    kern = functools.partial(_conv_mixer_kernel, tiles_per_seq=seq_len // tm, sub_rows=ROW_TILE)
    return pl.pallas_call(
        kern,
        grid=(t // tm,),
        in_specs=[
            pl.BlockSpec((tm, d), lambda i: (i, 0)),
            _const_spec((1, d)),
            _layer_spec((d, 3 * d), layer),
            _const_spec((CONV_W, d)),
            _layer_spec((d, d), layer),
            _const_spec((1, d)),
        ],
        out_specs=pl.BlockSpec((tm, d), lambda i: (i, 0)),
        out_shape=jax.ShapeDtypeStruct((t, d), F32),
        scratch_shapes=[pltpu.VMEM((tm + SUBLANES, d), F32)],
        compiler_params=_params(("arbitrary",)),
        name="conv_mixer",
    )(h, pre, w_in, conv_w, w_out, post)


def _swiglu_rows(h, pre_ref, wgu_ref, wd_ref, post_ref, chunks):
    f = wd_ref.shape[0]
    hn = _rms(h, pre_ref[...])
    acc = None
    for lo, hi in chunks:
        g = _dot(hn, wgu_ref[:, lo:hi])
        u = _dot(hn, wgu_ref[:, f + lo:f + hi])
        a = g * jax.nn.sigmoid(g) * u
        part = _dot(a, wd_ref[lo:hi, :])
        acc = part if acc is None else acc + part
    return h + _rms(acc, post_ref[...])


def _ffn_kernel(h_ref, pre_ref, wgu_ref, wd_ref, post_ref, o_ref, *, chunks, sub_rows):
    for r0 in range(0, h_ref.shape[0], sub_rows):
        rows = slice(r0, r0 + sub_rows)
        o_ref[rows, :] = _swiglu_rows(h_ref[rows, :], pre_ref, wgu_ref, wd_ref, post_ref, chunks)


def _ffn_kv_kernel(h_ref, pre_ref, wgu_ref, wd_ref, post_ref, kvn_ref, wk_ref, wvt_ref,
                   cos_ref, sup_ref, sdn_ref, o_ref, k_ref, vt_ref, *, chunks, sub_rows):
    for r0 in range(0, h_ref.shape[0], sub_rows):
        rows = slice(r0, r0 + sub_rows)
        out = _swiglu_rows(h_ref[rows, :], pre_ref, wgu_ref, wd_ref, post_ref, chunks)
        o_ref[rows, :] = out
        hkv = _rms(out, kvn_ref[...])
        k = _rope(_dot(hkv, wk_ref[...]), cos_ref[rows, :], sup_ref[rows, :], sdn_ref[rows, :])
        k_ref[rows, :] = k.astype(BF16)
        vt_ref[:, rows] = _dot_nt(wvt_ref[...], hkv).astype(BF16)


def _ffn_chunks(f):
    step = 768
    return tuple((lo, min(lo + step, f)) for lo in range(0, f, step))


def _ffn(h, pre, w_gate_up, w_down, post, layer, kv_proj=None):
    t, d = h.shape
    f = w_down.shape[1]
    tm = FFN_TILE
    row = lambda i: (i, 0)
    in_specs = [
        pl.BlockSpec((tm, d), row),
        _const_spec((1, d)),
        _layer_spec((d, 2 * f), layer),
        _layer_spec((f, d), layer),
        _const_spec((1, d)),
    ]
    args = [h, pre, w_gate_up, w_down, post]
    out_specs = pl.BlockSpec((tm, d), row)
    out_shape = jax.ShapeDtypeStruct((t, d), F32)
    static = dict(chunks=_ffn_chunks(f), sub_rows=FFN_SUB_ROWS)
    if kv_proj is None:
        kern = functools.partial(_ffn_kernel, **static)
    else:
        kv_norm, w_k, w_vt, tables, seq_len = kv_proj
        kvd = w_k.shape[1]
        tps = seq_len // tm
        pos = lambda i: (i % tps, 0)
        kern = functools.partial(_ffn_kv_kernel, **static)
        in_specs += [_const_spec((1, d)), _const_spec((d, kvd)), _const_spec((kvd, d))]
        in_specs += [pl.BlockSpec((tm, LANES), pos)] * 3
        args += [kv_norm, w_k, w_vt, *tables]
        out_specs = [out_specs, pl.BlockSpec((tm, kvd), row), pl.BlockSpec((kvd, tm), lambda i: (0, i))]
        out_shape = [out_shape, jax.ShapeDtypeStruct((t, kvd), BF16),
                     jax.ShapeDtypeStruct((kvd, t), BF16)]
    return pl.pallas_call(
        kern,
        grid=(t // tm,),
        in_specs=in_specs,
        out_specs=out_specs,
        out_shape=out_shape,
        compiler_params=_params(("parallel",)),
        name="swiglu",
    )(*args)


def _q_head_order(n_heads):
    group = n_heads // N_KV_HEADS
    order = []
    for p in range(N_KV_HEADS // 2):
        for i in range(group):
            order += [(2 * p) * group + i, (2 * p + 1) * group + i]
    return order


def _attn_kernel(sink_ref, h_ref, qn_ref, wq_ref, cos_ref, sup_ref, sdn_ref, kp_ref, kc_ref,
                 vtp_ref, vtc_ref, wo_ref, post_ref, o_ref, q_ref, att_t_ref, *, tiles_per_seq):
    tq = q_ref.shape[0]
    n_heads = q_ref.shape[1] // HEAD_DIM
    group = n_heads // N_KV_HEADS
    first_tile = (pl.program_id(0) % tiles_per_seq) == 0

    h = h_ref[...]
    q = _rope(_dot(_rms(h, qn_ref[...]), wq_ref[...]), cos_ref[...], sup_ref[...], sdn_ref[...])
    q_ref[...] = (q * (1.0 / math.sqrt(HEAD_DIM))).astype(BF16)

    key_j = lax.broadcasted_iota(jnp.int32, (BLOCK, 2 * BLOCK), 0)
    lane = lax.broadcasted_iota(jnp.int32, (BLOCK, 2 * BLOCK), 1)
    from_prev = key_j > (lane % BLOCK)
    lo_half = lax.broadcasted_iota(jnp.int32, (BLOCK, LANES), 1) < HEAD_DIM
    left = lax.broadcasted_iota(jnp.int32, (1, 2 * BLOCK), 1) < BLOCK
    ones_rows = jnp.ones((BF16_ROWS, 2 * BLOCK), BF16)

    for blk in range(tq // BLOCK):
        r0 = blk * BLOCK
        if blk == 0:
            prev_bias = jnp.where(first_tile, NEG, 0.0).astype(F32)
        for p in range(N_KV_HEADS // 2):
            kv_lanes = slice(p * LANES, (p + 1) * LANES)
            if blk == 0:
                k_win = jnp.concatenate([kp_ref[:, kv_lanes], kc_ref[0:BLOCK, kv_lanes]], axis=0)
            else:
                k_win = kc_ref[r0 - BLOCK:r0 + BLOCK, kv_lanes]
            p_t, m_row = [], []
            for i in range(group):
                c = p * group + i
                qc = q_ref[r0:r0 + BLOCK, c * LANES:(c + 1) * LANES]
                zero = jnp.zeros_like(qc)
                q_split = jnp.concatenate([jnp.where(lo_half, qc, zero),
                                           jnp.where(lo_half, zero, qc)], axis=0)
                s_t = _dot_nt(k_win, q_split)
                s_prev = s_t[0:BLOCK]
                if blk == 0:
                    s_prev = s_prev + prev_bias
                s_win = jnp.where(from_prev, s_prev, s_t[BLOCK:])
                m = jnp.max(s_win, axis=0, keepdims=True)
                e = jnp.exp(s_win - m)
                p_t.append(jnp.concatenate([jnp.where(from_prev, e, 0.0),
                                            jnp.where(from_prev, 0.0, e)], axis=0).astype(BF16))
                m_row.append(m)
            for side in range(2):
                g = 2 * p + side
                lanes = slice(side * BLOCK, (side + 1) * BLOCK)
                rows = slice(g * HEAD_DIM, (g + 1) * HEAD_DIM)
                if blk == 0:
                    v_win = jnp.concatenate([vtp_ref[rows, :], vtc_ref[rows, 0:BLOCK]], axis=1)
                else:
                    v_win = vtc_ref[rows, r0 - BLOCK:r0 + BLOCK]
                v_aug = jnp.concatenate([v_win, ones_rows], axis=0)
                for i0 in range(0, group, 2):
                    pt_pair = jnp.concatenate([p_t[i0][:, lanes], p_t[i0 + 1][:, lanes]], axis=1)
                    m_pair = jnp.concatenate([m_row[i0][:, lanes], m_row[i0 + 1][:, lanes]], axis=1)
                    head = g * group + i0
                    sink = jnp.where(left, sink_ref[head], sink_ref[head + 1])
                    o_aug = _dot(v_aug, pt_pair)
                    denom = o_aug[HEAD_DIM:HEAD_DIM + 1, :] + jnp.exp(sink - m_pair)
                    o_t = (o_aug[0:HEAD_DIM, :] / denom).astype(BF16)
                    att_t_ref[head * HEAD_DIM:(head + 1) * HEAD_DIM, r0:r0 + BLOCK] = o_t[:, 0:BLOCK]
                    att_t_ref[(head + 1) * HEAD_DIM:(head + 2) * HEAD_DIM, r0:r0 + BLOCK] = o_t[:, BLOCK:]

    att = att_t_ref[...].T
    proj = _dot(att, wo_ref[...])
    o_ref[...] = h + _rms(proj, post_ref[...])


def _attention(h, q_norm, w_q, tables, k, v_t, sinks, w_o, post, seq_len):
    t, d = h.shape
    qd = w_q.shape[1]
    kvd = k.shape[1]
    tq = ROW_TILE
    tps = seq_len // tq
    bpt = tq // BLOCK
    row = lambda i: (i, 0)
    pos = lambda i: (i % tps, 0)
    prev_blk = lambda i: jnp.maximum(i * bpt - 1, (i // tps) * tps * bpt)
    kern = functools.partial(_attn_kernel, tiles_per_seq=tps)
    return pl.pallas_call(
        kern,
        grid=(t // tq,),
        in_specs=[
            pl.BlockSpec(memory_space=pltpu.SMEM),
            pl.BlockSpec((tq, d), row),
            _const_spec((1, d)),
            _const_spec((d, qd)),
            pl.BlockSpec((tq, LANES), pos),
            pl.BlockSpec((tq, LANES), pos),
            pl.BlockSpec((tq, LANES), pos),
            pl.BlockSpec((BLOCK, kvd), lambda i: (prev_blk(i), 0)),
            pl.BlockSpec((tq, kvd), row),
            pl.BlockSpec((kvd, BLOCK), lambda i: (0, prev_blk(i))),
            pl.BlockSpec((kvd, tq), lambda i: (0, i)),
            _const_spec((qd, d)),
            _const_spec((1, d)),
        ],
        out_specs=pl.BlockSpec((tq, d), row),
        out_shape=jax.ShapeDtypeStruct((t, d), F32),
        scratch_shapes=[pltpu.VMEM((tq, qd), BF16), pltpu.VMEM((qd, tq), BF16)],
        compiler_params=_params(("parallel",)),
        name="swa_attention",
    )(sinks, h, q_norm, w_q, *tables, k, k, v_t, v_t, w_o, post)


def kernel(x, a_pre_norm, a_w_in, a_conv_w, a_w_out, a_post_norm, ffn_pre_norm, ffn_w_gate_up,
           ffn_w_down, ffn_post_norm, kv_norm, w_kv, b_pre_norm, b_w_q, b_sinks, b_w_o,
           b_post_norm):
    bsz, s, d = x.shape
    n_a = a_w_in.shape[0]
    depth = ffn_w_gate_up.shape[0]
    f = ffn_w_down.shape[1]
    kvd = w_kv.shape[1] // 2
    n_heads = b_w_q.shape[2] // HEAD_DIM
    assert 1 <= n_a < depth, "the shared K/V is produced by the last short-conv layer's SwiGLU"
    tables = _rope_tables(s)
    vec = lambda g: g.reshape(1, -1)
    q_cols = np.asarray(_q_head_order(n_heads))[:, None] * HEAD_DIM + np.arange(HEAD_DIM)[None, :]
    q_cols = q_cols.reshape(-1)

    h = x.reshape(bsz * s, d)
    k = v_t = None
    for l in range(depth):
        ffn_args = (vec(ffn_pre_norm[l]), ffn_w_gate_up, ffn_w_down, vec(ffn_post_norm[l]), l)
        if l < n_a:
            h = _conv_mixer(h, vec(a_pre_norm[l]), a_w_in, a_conv_w[l], a_w_out,
                            vec(a_post_norm[l]), l, s)
            if l == n_a - 1:
                kv_proj = (vec(kv_norm), w_kv[:, :kvd], w_kv[:, kvd:].T, tables, s)
                h, k, v_t = _ffn(h, *ffn_args, kv_proj=kv_proj)
            else:
                h = _ffn(h, *ffn_args)
        else:
            j = l - n_a
            h = _attention(h, vec(b_pre_norm[j]), b_w_q[j][:, q_cols], tables, k, v_t, b_sinks[j],
                           b_w_o[j].astype(BF16), vec(b_post_norm[j]), s)
            h = _ffn(h, *ffn_args)
    return h.reshape(bsz, s, d)
```

```python
import functools
import math

import jax
import jax.numpy as jnp
import numpy as np
from jax import lax
from jax.experimental import pallas as pl
from jax.experimental.pallas import tpu as pltpu

CONV_W = 3
HEAD_DIM = 64
N_KV_HEADS = 4
WINDOW = 128
BLOCK = 128
ROT_DIM = HEAD_DIM // 4
ROPE_THETA = 500000.0
EPS = 1e-6
NEG = -1e30

LANES = 128
SUBLANES = 8
BF16_ROWS = 16
ROW_TILE = 512
FFN_TILE = 512
FFN_SUB_ROWS = 512
SCORE_LOOKAHEAD = 3
VMEM_LIMIT = 60 * 1024 * 1024

F32 = jnp.float32
BF16 = jnp.bfloat16

assert WINDOW == BLOCK and LANES == 2 * HEAD_DIM


def _rms(x, g):
    r = lax.rsqrt(jnp.mean(x * x, axis=-1, keepdims=True) + EPS)
    return (x * r) * g


def _dot(a, b):
    return jnp.dot(a, b, preferred_element_type=F32)


def _dot_nt(a, b):
    return lax.dot_general(a, b, (((1,), (1,)), ((), ())), preferred_element_type=F32)


def _const_spec(shape):
    return pl.BlockSpec(shape, lambda *_: (0,) * len(shape), pipeline_mode=pl.Buffered(1))


def _layer_spec(shape, layer):
    return pl.BlockSpec((None,) + tuple(shape), lambda *_: (layer,) + (0,) * len(shape),
                        pipeline_mode=pl.Buffered(1))


def _params(semantics):
    return pltpu.CompilerParams(dimension_semantics=semantics, vmem_limit_bytes=VMEM_LIMIT)


def _rope(t, table):
    cos_t, sin_up, sin_dn = (table[:, j * LANES:(j + 1) * LANES] for j in range(3))
    outs = []
    for j in range(t.shape[1] // LANES):
        c = t[:, j * LANES:(j + 1) * LANES]
        outs.append(c * cos_t
                    + pltpu.roll(c, ROT_DIM // 2, axis=1) * sin_up
                    + pltpu.roll(c, LANES - ROT_DIM // 2, axis=1) * sin_dn)
    return jnp.concatenate(outs, axis=1)


def _rope_tables(seq_len):
    half = ROT_DIM // 2
    inv_freq = ROPE_THETA ** (-jnp.arange(0, ROT_DIM, 2, dtype=F32) / ROT_DIM)
    ang = jnp.arange(seq_len, dtype=F32)[:, None] * inv_freq[None, :]
    cos, sin = jnp.cos(ang), jnp.sin(ang)
    ones = jnp.ones((seq_len, HEAD_DIM - ROT_DIM), F32)
    zeros = jnp.zeros((seq_len, HEAD_DIM - ROT_DIM), F32)
    zh = jnp.zeros((seq_len, half), F32)
    reps = LANES // HEAD_DIM
    cos_t = jnp.tile(jnp.concatenate([cos, cos, ones], axis=1), (1, reps))
    sin_up = jnp.tile(jnp.concatenate([zh, sin, zeros], axis=1), (1, reps))
    sin_dn = jnp.tile(jnp.concatenate([-sin, zh, zeros], axis=1), (1, reps))
    return jnp.concatenate([cos_t, sin_up, sin_dn], axis=1)


def _conv_mixer_kernel(h_ref, pre_ref, win_ref, cw_ref, wout_ref, post_ref, o_ref, cu_ref,
                       *, tiles_per_seq, sub_rows):
    tm, d = h_ref.shape
    i = pl.program_id(0)

    @pl.when(i % tiles_per_seq == 0)
    def _():
        cu_ref[0:SUBLANES, :] = jnp.zeros((SUBLANES, d), F32)

    @pl.when(i % tiles_per_seq != 0)
    def _():
        cu_ref[0:SUBLANES, :] = cu_ref[tm:tm + SUBLANES, :]

    cw = cw_ref[...]
    for r0 in range(0, tm, sub_rows):
        h = h_ref[r0:r0 + sub_rows, :]
        hn = _rms(h, pre_ref[...])
        cu = _dot(hn, win_ref[:, d:2 * d]) * _dot(hn, win_ref[:, 2 * d:3 * d])
        base = SUBLANES + r0
        cu_ref[base:base + sub_rows, :] = cu
        conv = (cu_ref[base - 2:base - 2 + sub_rows, :] * cw[0:1, :]
                + cu_ref[base - 1:base - 1 + sub_rows, :] * cw[1:2, :]
                + cu * cw[2:3, :])
        y = _dot(hn, win_ref[:, 0:d]) * conv
        mix = _dot(y, wout_ref[...])
        o_ref[r0:r0 + sub_rows, :] = h + _rms(mix, post_ref[...])


def _conv_mixer(h, pre, w_in, conv_w, w_out, post, layer, seq_len):
    t, d = h.shape
    tm = 2 * ROW_TILE
    kern = functools.partial(_conv_mixer_kernel, tiles_per_seq=seq_len // tm, sub_rows=ROW_TILE)
    return pl.pallas_call(
        kern,
        grid=(t // tm,),
        in_specs=[
            pl.BlockSpec((tm, d), lambda i: (i, 0)),
            _const_spec((1, d)),
            _layer_spec((d, 3 * d), layer),
            _const_spec((CONV_W, d)),
            _layer_spec((d, d), layer),
            _const_spec((1, d)),
        ],
        out_specs=pl.BlockSpec((tm, d), lambda i: (i, 0)),
        out_shape=jax.ShapeDtypeStruct((t, d), F32),
        scratch_shapes=[pltpu.VMEM((tm + SUBLANES, d), F32)],
        compiler_params=_params(("arbitrary",)),
        name="conv_mixer",
    )(h, pre, w_in, conv_w, w_out, post)


def _swiglu_rows(h, pre_ref, wgu_ref, wd_ref, post_ref, chunks):
    f = wd_ref.shape[0]
    hn = _rms(h, pre_ref[...])
    acc = None
    for lo, hi in chunks:
        g = _dot(hn, wgu_ref[:, lo:hi])
        u = _dot(hn, wgu_ref[:, f + lo:f + hi])
        a = g * jax.nn.sigmoid(g) * u
        part = _dot(a, wd_ref[lo:hi, :])
        acc = part if acc is None else acc + part
    return h + _rms(acc, post_ref[...])


def _ffn_kernel(h_ref, pre_ref, wgu_ref, wd_ref, post_ref, o_ref, *, chunks, sub_rows):
    for r0 in range(0, h_ref.shape[0], sub_rows):
        rows = slice(r0, r0 + sub_rows)
        o_ref[rows, :] = _swiglu_rows(h_ref[rows, :], pre_ref, wgu_ref, wd_ref, post_ref, chunks)


def _ffn_kv_kernel(h_ref, pre_ref, wgu_ref, wd_ref, post_ref, kvn_ref, wkv_ref, rope_ref,
                   o_ref, k_ref, vt_ref, *, chunks, sub_rows):
    kvd = k_ref.shape[1]
    for r0 in range(0, h_ref.shape[0], sub_rows):
        rows = slice(r0, r0 + sub_rows)
        out = _swiglu_rows(h_ref[rows, :], pre_ref, wgu_ref, wd_ref, post_ref, chunks)
        o_ref[rows, :] = out
        kv = _dot(_rms(out, kvn_ref[...]), wkv_ref[...])
        k_ref[rows, :] = _rope(kv[:, :kvd], rope_ref[rows, :]).astype(BF16)
        vt_ref[:, rows] = kv[:, kvd:].T.astype(BF16)


def _ffn_chunks(f):
    step = 768
    return tuple((lo, min(lo + step, f)) for lo in range(0, f, step))


def _ffn(h, pre, w_gate_up, w_down, post, layer, kv_proj=None):
    t, d = h.shape
    f = w_down.shape[1]
    tm = FFN_TILE
    row = lambda i: (i, 0)
    in_specs = [
        pl.BlockSpec((tm, d), row),
        _const_spec((1, d)),
        _layer_spec((d, 2 * f), layer),
        _layer_spec((f, d), layer),
        _const_spec((1, d)),
    ]
    args = [h, pre, w_gate_up, w_down, post]
    out_specs = pl.BlockSpec((tm, d), row)
    out_shape = jax.ShapeDtypeStruct((t, d), F32)
    static = dict(chunks=_ffn_chunks(f), sub_rows=FFN_SUB_ROWS)
    if kv_proj is None:
        kern = functools.partial(_ffn_kernel, **static)
    else:
        kv_norm, w_kv, rope_table, seq_len = kv_proj
        kvd = w_kv.shape[1] // 2
        tps = seq_len // tm
        kern = functools.partial(_ffn_kv_kernel, **static)
        in_specs += [_const_spec((1, d)), _const_spec((d, 2 * kvd)),
                     pl.BlockSpec((tm, rope_table.shape[1]), lambda i: (i % tps, 0))]
        args += [kv_norm, w_kv, rope_table]
        out_specs = [out_specs, pl.BlockSpec((tm, kvd), row), pl.BlockSpec((kvd, tm), lambda i: (0, i))]
        out_shape = [out_shape, jax.ShapeDtypeStruct((t, kvd), BF16),
                     jax.ShapeDtypeStruct((kvd, t), BF16)]
    return pl.pallas_call(
        kern,
        grid=(t // tm,),
        in_specs=in_specs,
        out_specs=out_specs,
        out_shape=out_shape,
        compiler_params=_params(("parallel",)),
        name="swiglu",
    )(*args)


def _permute_q_heads(w_q, n_heads):
    d = w_q.shape[0]
    group = n_heads // N_KV_HEADS
    w = w_q.reshape(d, N_KV_HEADS // 2, 2, group, HEAD_DIM)
    return w.transpose(0, 1, 3, 2, 4).reshape(d, n_heads * HEAD_DIM)


def _attn_kernel(sink_ref, h_ref, qn_ref, wq_ref, rope_ref, kp_ref, kc_ref, vtp_ref, vtc_ref,
                 wo_ref, post_ref, o_ref, q_ref, att_t_ref, *, tiles_per_seq):
    tq = q_ref.shape[0]
    n_heads = q_ref.shape[1] // HEAD_DIM
    group = n_heads // N_KV_HEADS
    first_tile = (pl.program_id(0) % tiles_per_seq) == 0

    h = h_ref[...]
    q = _rope(_dot(_rms(h, qn_ref[...]), wq_ref[...]), rope_ref[...])
    q_ref[...] = (q * (1.0 / math.sqrt(HEAD_DIM))).astype(BF16)

    width = 4 * BLOCK
    key_j = lax.broadcasted_iota(jnp.int32, (BLOCK, width), 0)
    lane = lax.broadcasted_iota(jnp.int32, (BLOCK, width), 1)
    from_prev = key_j > (lane % BLOCK)
    lo_half = lax.broadcasted_iota(jnp.int32, (BLOCK, LANES), 1) < HEAD_DIM
    left = lax.broadcasted_iota(jnp.int32, (1, 2 * BLOCK), 1) < BLOCK
    ones_rows = jnp.ones((BF16_ROWS, 2 * BLOCK), BF16)

    prev_bias = jnp.where(first_tile, NEG, 0.0).astype(F32)

    def scores(blk, p, i0):
        r0 = blk * BLOCK
        kv_lanes = slice(p * LANES, (p + 1) * LANES)
        if blk == 0:
            k_win = jnp.concatenate([kp_ref[:, kv_lanes], kc_ref[0:BLOCK, kv_lanes]], axis=0)
        else:
            k_win = kc_ref[r0 - BLOCK:r0 + BLOCK, kv_lanes]
        c0 = p * group + i0
        qc0 = q_ref[r0:r0 + BLOCK, c0 * LANES:(c0 + 1) * LANES]
        qc1 = q_ref[r0:r0 + BLOCK, (c0 + 1) * LANES:(c0 + 2) * LANES]
        zero = jnp.zeros_like(qc0)
        q_split = jnp.concatenate(
            [jnp.where(lo_half, qc0, zero), jnp.where(lo_half, qc1, zero),
             jnp.where(lo_half, zero, qc0), jnp.where(lo_half, zero, qc1)], axis=0)
        return _dot_nt(k_win, q_split)

    def attend(blk, p, i0, s_t):
        r0 = blk * BLOCK
        kv_lanes = slice(p * LANES, (p + 1) * LANES)
        if blk == 0:
            v_win = jnp.concatenate([vtp_ref[kv_lanes, :], vtc_ref[kv_lanes, 0:BLOCK]], axis=1)
        else:
            v_win = vtc_ref[kv_lanes, r0 - BLOCK:r0 + BLOCK]
        s_prev = s_t[0:BLOCK]
        if blk == 0:
            s_prev = s_prev + prev_bias
        s_win = jnp.where(from_prev, s_prev, s_t[BLOCK:])
        m = jnp.max(s_win, axis=0, keepdims=True)
        e = jnp.exp(s_win - m)
        p_t = jnp.concatenate([jnp.where(from_prev, e, 0.0),
                               jnp.where(from_prev, 0.0, e)], axis=0).astype(BF16)
        for side in range(2):
            lanes = slice(side * 2 * BLOCK, (side + 1) * 2 * BLOCK)
            head = (2 * p + side) * group + i0
            v_rows = v_win[side * HEAD_DIM:(side + 1) * HEAD_DIM]
            v_aug = jnp.concatenate([v_rows, ones_rows], axis=0)
            o_aug = _dot(v_aug, p_t[:, lanes])
            sink = jnp.where(left, sink_ref[head], sink_ref[head + 1])
            denom = o_aug[HEAD_DIM:HEAD_DIM + 1, :] + jnp.exp(sink - m[:, lanes])
            o_t = (o_aug[0:HEAD_DIM, :] / denom).astype(BF16)
            att_t_ref[head * HEAD_DIM:(head + 1) * HEAD_DIM, r0:r0 + BLOCK] = o_t[:, 0:BLOCK]
            att_t_ref[(head + 1) * HEAD_DIM:(head + 2) * HEAD_DIM, r0:r0 + BLOCK] = o_t[:, BLOCK:]

    units = [(blk, p, i0) for blk in range(tq // BLOCK) for p in range(N_KV_HEADS // 2)
             for i0 in range(0, group, 2)]
    pending = [scores(*u) for u in units[:SCORE_LOOKAHEAD]]
    for n, unit in enumerate(units):
        if n + SCORE_LOOKAHEAD < len(units):
            pending.append(scores(*units[n + SCORE_LOOKAHEAD]))
        attend(*unit, pending.pop(0))

    att = att_t_ref[...].T
    proj = _dot(att, wo_ref[...])
    o_ref[...] = h + _rms(proj, post_ref[...])


def _attention(h, q_norm, w_q, rope_table, k, v_t, sinks, w_o, post, seq_len):
    t, d = h.shape
    qd = w_q.shape[1]
    kvd = k.shape[1]
    tq = ROW_TILE
    tps = seq_len // tq
    bpt = tq // BLOCK
    row = lambda i: (i, 0)
    prev_blk = lambda i: jnp.maximum(i * bpt - 1, (i // tps) * tps * bpt)
    kern = functools.partial(_attn_kernel, tiles_per_seq=tps)
    return pl.pallas_call(
        kern,
        grid=(t // tq,),
        in_specs=[
            pl.BlockSpec(memory_space=pltpu.SMEM),
            pl.BlockSpec((tq, d), row),
            _const_spec((1, d)),
            _const_spec((d, qd)),
            pl.BlockSpec((tq, rope_table.shape[1]), lambda i: (i % tps, 0)),
            pl.BlockSpec((BLOCK, kvd), lambda i: (prev_blk(i), 0)),
            pl.BlockSpec((tq, kvd), row),
            pl.BlockSpec((kvd, BLOCK), lambda i: (0, prev_blk(i))),
            pl.BlockSpec((kvd, tq), lambda i: (0, i)),
            _const_spec((qd, d)),
            _const_spec((1, d)),
        ],
        out_specs=pl.BlockSpec((tq, d), row),
        out_shape=jax.ShapeDtypeStruct((t, d), F32),
        scratch_shapes=[pltpu.VMEM((tq, qd), BF16), pltpu.VMEM((qd, tq), BF16)],
        compiler_params=_params(("parallel",)),
        name="swa_attention",
    )(sinks, h, q_norm, w_q, rope_table, k, k, v_t, v_t, w_o, post)


def kernel(x, a_pre_norm, a_w_in, a_conv_w, a_w_out, a_post_norm, ffn_pre_norm, ffn_w_gate_up,
           ffn_w_down, ffn_post_norm, kv_norm, w_kv, b_pre_norm, b_w_q, b_sinks, b_w_o,
           b_post_norm):
    bsz, s, d = x.shape
    n_a = a_w_in.shape[0]
    depth = ffn_w_gate_up.shape[0]
    f = ffn_w_down.shape[1]
    n_heads = b_w_q.shape[2] // HEAD_DIM
    assert 1 <= n_a < depth, "the shared K/V is produced by the last short-conv layer's SwiGLU"
    rope_table = _rope_tables(s)
    vec = lambda g: g.reshape(1, -1)

    h = x.reshape(bsz * s, d)
    k = v_t = None
    for l in range(depth):
        ffn_args = (vec(ffn_pre_norm[l]), ffn_w_gate_up, ffn_w_down, vec(ffn_post_norm[l]), l)
        if l < n_a:
            h = _conv_mixer(h, vec(a_pre_norm[l]), a_w_in, a_conv_w[l], a_w_out,
                            vec(a_post_norm[l]), l, s)
            if l == n_a - 1:
                kv_proj = (vec(kv_norm), w_kv, rope_table, s)
                h, k, v_t = _ffn(h, *ffn_args, kv_proj=kv_proj)
            else:
                h = _ffn(h, *ffn_args)
        else:
            j = l - n_a
            h = _attention(h, vec(b_pre_norm[j]), _permute_q_heads(b_w_q[j], n_heads), rope_table,
                           k, v_t, b_sinks[j], b_w_o[j].astype(BF16), vec(b_post_norm[j]), s)
            h = _ffn(h, *ffn_args)
    return h.reshape(bsz, s, d)
```

```python
import functools
import math

import jax
import jax.numpy as jnp
import numpy as np
from jax import lax
from jax.experimental import pallas as pl
from jax.experimental.pallas import tpu as pltpu

CONV_W = 3
HEAD_DIM = 64
N_KV_HEADS = 4
WINDOW = 128
BLOCK = 128
ROT_DIM = HEAD_DIM // 4
ROPE_THETA = 500000.0
EPS = 1e-6
NEG = -1e30

LANES = 128
SUBLANES = 8
BF16_ROWS = 16
ROW_TILE = 512
FFN_TILE = 512
ATTN_TILE = 1024
SCORE_LOOKAHEAD = 3
VMEM_LIMIT = 60 * 1024 * 1024

F32 = jnp.float32
BF16 = jnp.bfloat16

assert WINDOW == BLOCK and LANES == 2 * HEAD_DIM


def _rms(x, g):
    r = lax.rsqrt(jnp.mean(x * x, axis=-1, keepdims=True) + EPS)
    return (x * r) * g


def _dot(a, b):
    return jnp.dot(a, b, preferred_element_type=F32)


def _dot_nt(a, b):
    return lax.dot_general(a, b, (((1,), (1,)), ((), ())), preferred_element_type=F32)


def _const_spec(shape):
    return pl.BlockSpec(shape, lambda *_: (0,) * len(shape), pipeline_mode=pl.Buffered(1))


def _layer_spec(shape, layer):
    return pl.BlockSpec((None,) + tuple(shape), lambda *_: (layer,) + (0,) * len(shape),
                        pipeline_mode=pl.Buffered(1))


def _params(semantics):
    return pltpu.CompilerParams(dimension_semantics=semantics, vmem_limit_bytes=VMEM_LIMIT)


def _rope(t, table):
    cos_t, sin_up, sin_dn = (table[:, j * LANES:(j + 1) * LANES] for j in range(3))
    outs = []
    for j in range(t.shape[1] // LANES):
        c = t[:, j * LANES:(j + 1) * LANES]
        outs.append(c * cos_t
                    + pltpu.roll(c, ROT_DIM // 2, axis=1) * sin_up
                    + pltpu.roll(c, LANES - ROT_DIM // 2, axis=1) * sin_dn)
    return jnp.concatenate(outs, axis=1)


def _rope_tables(seq_len):
    half = ROT_DIM // 2
    f32 = np.float32
    inv_freq = f32(ROPE_THETA) ** (-np.arange(0, ROT_DIM, 2, dtype=f32) / f32(ROT_DIM))
    ang = np.arange(seq_len, dtype=f32)[:, None] * inv_freq[None, :]
    cos, sin = np.cos(ang), np.sin(ang)
    ones = np.ones((seq_len, HEAD_DIM - ROT_DIM), f32)
    zeros = np.zeros((seq_len, HEAD_DIM - ROT_DIM), f32)
    zh = np.zeros((seq_len, half), f32)
    reps = LANES // HEAD_DIM
    cos_t = np.tile(np.concatenate([cos, cos, ones], axis=1), (1, reps))
    sin_up = np.tile(np.concatenate([zh, sin, zeros], axis=1), (1, reps))
    sin_dn = np.tile(np.concatenate([-sin, zh, zeros], axis=1), (1, reps))
    return jnp.asarray(np.concatenate([cos_t, sin_up, sin_dn], axis=1), F32)


def _conv_mixer_kernel(h_ref, pre_ref, win_ref, cw_ref, wout_ref, post_ref, o_ref, cu_ref,
                       *, tiles_per_seq, sub_rows):
    tm, d = h_ref.shape
    i = pl.program_id(0)

    @pl.when(i % tiles_per_seq == 0)
    def _():
        cu_ref[0:SUBLANES, :] = jnp.zeros((SUBLANES, d), F32)

    @pl.when(i % tiles_per_seq != 0)
    def _():
        cu_ref[0:SUBLANES, :] = cu_ref[tm:tm + SUBLANES, :]

    cw = cw_ref[...]
    for r0 in range(0, tm, sub_rows):
        h = h_ref[r0:r0 + sub_rows, :]
        hn = _rms(h, pre_ref[...])
        cu = _dot(hn, win_ref[:, d:2 * d]) * _dot(hn, win_ref[:, 2 * d:3 * d])
        base = SUBLANES + r0
        cu_ref[base:base + sub_rows, :] = cu
        conv = (cu_ref[base - 2:base - 2 + sub_rows, :] * cw[0:1, :]
                + cu_ref[base - 1:base - 1 + sub_rows, :] * cw[1:2, :]
                + cu * cw[2:3, :])
        y = _dot(hn, win_ref[:, 0:d]) * conv
        mix = _dot(y, wout_ref[...])
        o_ref[r0:r0 + sub_rows, :] = h + _rms(mix, post_ref[...])


def _conv_mixer(h, pre, w_in, conv_w, w_out, post, layer, seq_len):
    t, d = h.shape
    tm = 2 * ROW_TILE
    kern = functools.partial(_conv_mixer_kernel, tiles_per_seq=seq_len // tm, sub_rows=ROW_TILE)
    return pl.pallas_call(
        kern,
        grid=(t // tm,),
        in_specs=[
            pl.BlockSpec((tm, d), lambda i: (i, 0)),
            _const_spec((1, d)),
            _layer_spec((d, 3 * d), layer),
            _const_spec((CONV_W, d)),
            _layer_spec((d, d), layer),
            _const_spec((1, d)),
        ],
        out_specs=pl.BlockSpec((tm, d), lambda i: (i, 0)),
        out_shape=jax.ShapeDtypeStruct((t, d), F32),
        scratch_shapes=[pltpu.VMEM((tm + SUBLANES, d), F32)],
        compiler_params=_params(("arbitrary",)),
        name="conv_mixer",
    )(h, pre, w_in, conv_w, w_out, post)


def _ffn_kernel(h_ref, pre_ref, wgu_ref, wd_ref, post_ref, o_ref, *, chunks):
    f = wd_ref.shape[0]
    h = h_ref[...]
    hn = _rms(h, pre_ref[...])
    acc = None
    for lo, hi in chunks:
        g = _dot(hn, wgu_ref[:, lo:hi])
        u = _dot(hn, wgu_ref[:, f + lo:f + hi])
        a = g * jax.nn.sigmoid(g) * u
        part = _dot(a, wd_ref[lo:hi, :])
        acc = part if acc is None else acc + part
    o_ref[...] = h + _rms(acc, post_ref[...])


def _ffn_chunks(f):
    step = 768
    return tuple((lo, min(lo + step, f)) for lo in range(0, f, step))


def _ffn(h, pre, w_gate_up, w_down, post, layer):
    t, d = h.shape
    f = w_down.shape[1]
    tm = FFN_TILE
    row = lambda i: (i, 0)
    return pl.pallas_call(
        functools.partial(_ffn_kernel, chunks=_ffn_chunks(f)),
        grid=(t // tm,),
        in_specs=[
            pl.BlockSpec((tm, d), row),
            _const_spec((1, d)),
            _layer_spec((d, 2 * f), layer),
            _layer_spec((f, d), layer),
            _const_spec((1, d)),
        ],
        out_specs=pl.BlockSpec((tm, d), row),
        out_shape=jax.ShapeDtypeStruct((t, d), F32),
        compiler_params=_params(("parallel",)),
        name="swiglu",
    )(h, pre, w_gate_up, w_down, post)


def _attn_kernel(sink_ref, hp_ref, h_ref, ropep_ref, rope_ref, qn_ref, wq_ref, kvn_ref, wkv_ref,
                 wo_ref, post_ref, o_ref, q_ref, k_ref, vt_ref, att_t_ref, wo_bf_ref,
                 *, tiles_per_seq):
    tq = q_ref.shape[0]
    kvd = vt_ref.shape[0]
    n_heads = q_ref.shape[1] // HEAD_DIM
    group = n_heads // N_KV_HEADS
    assert group * HEAD_DIM == 2 * LANES, "one unit = the two q lane chunks of a kv head"
    first_tile = (pl.program_id(0) % tiles_per_seq) == 0

    @pl.when(pl.program_id(0) == 0)
    def _():
        wo_bf_ref[...] = wo_ref[...].astype(BF16)

    lo_half = lax.broadcasted_iota(jnp.int32, (BLOCK, LANES), 1) < HEAD_DIM

    def project_kv(rows_h, rows_rope, r0):
        n = rows_h.shape[0]
        kv = _dot(_rms(rows_h, kvn_ref[...]), wkv_ref[...])
        k = _rope(kv[:, :kvd], rows_rope)
        lo = lax.broadcasted_iota(jnp.int32, (n, LANES), 1) < HEAD_DIM
        for c in range(kvd // LANES):
            pair = k[:, c * LANES:(c + 1) * LANES]
            swapped = pltpu.roll(pair, HEAD_DIM, axis=1)
            k_ref[r0:r0 + n, (2 * c) * LANES:(2 * c + 1) * LANES] = (
                jnp.where(lo, pair, swapped).astype(BF16))
            k_ref[r0:r0 + n, (2 * c + 1) * LANES:(2 * c + 2) * LANES] = (
                jnp.where(lo, swapped, pair).astype(BF16))
        vt_ref[:, r0:r0 + n] = kv[:, kvd:].T.astype(BF16)

    h = h_ref[...]
    project_kv(hp_ref[...], ropep_ref[...], 0)
    project_kv(h, rope_ref[...], BLOCK)
    q = _rope(_dot(_rms(h, qn_ref[...]), wq_ref[...]), rope_ref[...])
    q_ref[...] = (q * (1.0 / math.sqrt(HEAD_DIM))).astype(BF16)

    unit_heads = (0, 2, 1, 3)
    width = group * BLOCK
    key_j = lax.broadcasted_iota(jnp.int32, (BLOCK, width), 0)
    lane = lax.broadcasted_iota(jnp.int32, (BLOCK, width), 1)
    from_prev = key_j > (lane % BLOCK)
    seg = lax.broadcasted_iota(jnp.int32, (1, width), 1) // BLOCK
    ones_rows = jnp.ones((BF16_ROWS, 2 * BLOCK), BF16)

    prev_bias = jnp.where(first_tile, NEG, 0.0).astype(F32)

    def scores(blk, g):
        r0 = blk * BLOCK
        k_win = k_ref[r0:r0 + 2 * BLOCK, g * LANES:(g + 1) * LANES]
        qc0 = q_ref[r0:r0 + BLOCK, (2 * g) * LANES:(2 * g + 1) * LANES]
        qc1 = q_ref[r0:r0 + BLOCK, (2 * g + 1) * LANES:(2 * g + 2) * LANES]
        zero = jnp.zeros_like(qc0)
        q_split = jnp.concatenate(
            [jnp.where(lo_half, qc0, zero), jnp.where(lo_half, qc1, zero),
             jnp.where(lo_half, zero, qc0), jnp.where(lo_half, zero, qc1)], axis=0)
        return _dot_nt(k_win, q_split)

    def attend(blk, g, s_t):
        r0 = blk * BLOCK
        v_win = vt_ref[g * HEAD_DIM:(g + 1) * HEAD_DIM, r0:r0 + 2 * BLOCK]
        s_prev = s_t[0:BLOCK]
        if blk == 0:
            s_prev = s_prev + prev_bias
        s_win = jnp.where(from_prev, s_prev, s_t[BLOCK:])
        m = jnp.max(s_win, axis=0, keepdims=True)
        e = jnp.exp(s_win - m)
        p_t = jnp.concatenate([jnp.where(from_prev, e, 0.0),
                               jnp.where(from_prev, 0.0, e)], axis=0).astype(BF16)
        v_aug = jnp.concatenate([v_win, ones_rows], axis=0)
        o_aug = _dot(v_aug, p_t)
        heads = [g * group + u for u in unit_heads]
        sink = sink_ref[heads[-1]]
        for n in range(group - 2, -1, -1):
            sink = jnp.where(seg == n, sink_ref[heads[n]], sink)
        denom = o_aug[HEAD_DIM:HEAD_DIM + 1, :] + jnp.exp(sink - m)
        o_t = (o_aug[0:HEAD_DIM, :] / denom).astype(BF16)
        for n, head in enumerate(heads):
            att_t_ref[head * HEAD_DIM:(head + 1) * HEAD_DIM, r0:r0 + BLOCK] = (
                o_t[:, n * BLOCK:(n + 1) * BLOCK])

    units = [(blk, g) for blk in range(tq // BLOCK) for g in range(N_KV_HEADS)]
    pending = [scores(*u) for u in units[:SCORE_LOOKAHEAD]]
    for n, unit in enumerate(units):
        if n + SCORE_LOOKAHEAD < len(units):
            pending.append(scores(*units[n + SCORE_LOOKAHEAD]))
        attend(*unit, pending.pop(0))

    att = att_t_ref[...].T
    proj = _dot(att, wo_bf_ref[...])
    o_ref[...] = h + _rms(proj, post_ref[...])


def _attention(h, q_norm, w_q, kv_norm, w_kv, rope_table, sinks, w_o, post, layer, seq_len):
    t, d = h.shape
    qd = w_q.shape[2]
    kvd = w_kv.shape[1] // 2
    tq = ATTN_TILE
    tps = seq_len // tq
    bpt = tq // BLOCK
    row = lambda i: (i, 0)
    prev_blk = lambda i: (jnp.maximum(i * bpt - 1, (i // tps) * tps * bpt), 0)
    prev_pos = lambda i: (jnp.maximum((i % tps) * bpt - 1, 0), 0)
    rope_w = rope_table.shape[1]
    kern = functools.partial(_attn_kernel, tiles_per_seq=tps)
    return pl.pallas_call(
        kern,
        grid=(t // tq,),
        in_specs=[
            pl.BlockSpec(memory_space=pltpu.SMEM),
            pl.BlockSpec((BLOCK, d), prev_blk),
            pl.BlockSpec((tq, d), row),
            pl.BlockSpec((BLOCK, rope_w), prev_pos),
            pl.BlockSpec((tq, rope_w), lambda i: (i % tps, 0)),
            _const_spec((1, d)),
            _layer_spec((d, qd), layer),
            _const_spec((1, d)),
            _const_spec((d, 2 * kvd)),
            _layer_spec((qd, d), layer),
            _const_spec((1, d)),
        ],
        out_specs=pl.BlockSpec((tq, d), row),
        out_shape=jax.ShapeDtypeStruct((t, d), F32),
        scratch_shapes=[
            pltpu.VMEM((tq, qd), BF16),
            pltpu.VMEM((tq + BLOCK, 2 * kvd), BF16),
            pltpu.VMEM((kvd, tq + BLOCK), BF16),
            pltpu.VMEM((qd, tq), BF16),
            pltpu.VMEM((qd, d), BF16),
        ],
        compiler_params=_params(("arbitrary",)),
        name="swa_attention",
    )(sinks, h, h, rope_table, rope_table, q_norm, w_q, kv_norm, w_kv, w_o, post)


def kernel(x, a_pre_norm, a_w_in, a_conv_w, a_w_out, a_post_norm, ffn_pre_norm, ffn_w_gate_up,
           ffn_w_down, ffn_post_norm, kv_norm, w_kv, b_pre_norm, b_w_q, b_sinks, b_w_o,
           b_post_norm):
    bsz, s, d = x.shape
    n_a = a_w_in.shape[0]
    depth = ffn_w_gate_up.shape[0]
    rope_table = _rope_tables(s)
    vec = lambda g: g.reshape(1, -1)

    h = x.reshape(bsz * s, d)
    for l in range(depth):
        if l < n_a:
            h = _conv_mixer(h, vec(a_pre_norm[l]), a_w_in, a_conv_w[l], a_w_out,
                            vec(a_post_norm[l]), l, s)
        else:
            if l == n_a:
                h_kv = h
            assert l == n_a, "K/V are projected inside the first attention layer's call only"
            j = l - n_a
            h = _attention(h_kv, vec(b_pre_norm[j]), b_w_q, vec(kv_norm), w_kv, rope_table,
                           b_sinks[j], b_w_o, vec(b_post_norm[j]), j, s)
        h = _ffn(h, vec(ffn_pre_norm[l]), ffn_w_gate_up, ffn_w_down, vec(ffn_post_norm[l]), l)
    return h.reshape(bsz, s, d)
```

```python
import functools
import math

import jax
import jax.numpy as jnp
import numpy as np
from jax import lax
from jax.experimental import pallas as pl
from jax.experimental.pallas import tpu as pltpu

CONV_W = 3
HEAD_DIM = 64
N_KV_HEADS = 4
WINDOW = 128
BLOCK = 128
ROT_DIM = HEAD_DIM // 4
ROPE_THETA = 500000.0
EPS = 1e-6
NEG = -1e30

LANES = 128
SUBLANES = 8
BF16_ROWS = 16
ROW_TILE = 512
FFN_TILE = 512
ATTN_TILE = 1024
ATTN_PIECE = 256
SCORE_LOOKAHEAD = 3
VMEM_LIMIT = 60 * 1024 * 1024

F32 = jnp.float32
BF16 = jnp.bfloat16

assert WINDOW == BLOCK and LANES == 2 * HEAD_DIM


def _rms(x, g):
    r = lax.rsqrt(jnp.mean(x * x, axis=-1, keepdims=True) + EPS)
    return (x * r) * g


def _dot(a, b):
    return jnp.dot(a, b, preferred_element_type=F32)


def _dot_nt(a, b):
    return lax.dot_general(a, b, (((1,), (1,)), ((), ())), preferred_element_type=F32)


def _const_spec(shape):
    return pl.BlockSpec(shape, lambda *_: (0,) * len(shape), pipeline_mode=pl.Buffered(1))


def _layer_spec(shape, layer):
    return pl.BlockSpec((None,) + tuple(shape), lambda *_: (layer,) + (0,) * len(shape),
                        pipeline_mode=pl.Buffered(1))


def _params(semantics):
    return pltpu.CompilerParams(dimension_semantics=semantics, vmem_limit_bytes=VMEM_LIMIT)


def _rope(t, table):
    cos_t, sin_up, sin_dn = (table[:, j * LANES:(j + 1) * LANES] for j in range(3))
    outs = []
    for j in range(t.shape[1] // LANES):
        c = t[:, j * LANES:(j + 1) * LANES]
        cb = c.astype(BF16)
        outs.append(c * cos_t
                    + pltpu.roll(cb, ROT_DIM // 2, axis=1).astype(F32) * sin_up
                    + pltpu.roll(cb, LANES - ROT_DIM // 2, axis=1).astype(F32) * sin_dn)
    return jnp.concatenate(outs, axis=1)


def _rope_tables(seq_len):
    half = ROT_DIM // 2
    f32 = np.float32
    inv_freq = f32(ROPE_THETA) ** (-np.arange(0, ROT_DIM, 2, dtype=f32) / f32(ROT_DIM))
    ang = np.arange(seq_len, dtype=f32)[:, None] * inv_freq[None, :]
    cos, sin = np.cos(ang), np.sin(ang)
    ones = np.ones((seq_len, HEAD_DIM - ROT_DIM), f32)
    zeros = np.zeros((seq_len, HEAD_DIM - ROT_DIM), f32)
    zh = np.zeros((seq_len, half), f32)
    reps = LANES // HEAD_DIM
    cos_t = np.tile(np.concatenate([cos, cos, ones], axis=1), (1, reps))
    sin_up = np.tile(np.concatenate([zh, sin, zeros], axis=1), (1, reps))
    sin_dn = np.tile(np.concatenate([-sin, zh, zeros], axis=1), (1, reps))
    return jnp.asarray(np.concatenate([cos_t, sin_up, sin_dn], axis=1), F32)


def _conv_mixer_kernel(h_ref, pre_ref, win_ref, cw_ref, wout_ref, post_ref, o_ref, cu_ref,
                       *, tiles_per_seq, sub_rows):
    tm, d = h_ref.shape
    i = pl.program_id(0)

    @pl.when(i % tiles_per_seq == 0)
    def _():
        cu_ref[0:SUBLANES, :] = jnp.zeros((SUBLANES, d), F32)

    @pl.when(i % tiles_per_seq != 0)
    def _():
        cu_ref[0:SUBLANES, :] = cu_ref[tm:tm + SUBLANES, :]

    cw = cw_ref[...]
    for r0 in range(0, tm, sub_rows):
        h = h_ref[r0:r0 + sub_rows, :]
        hn = _rms(h, pre_ref[...])
        cu = _dot(hn, win_ref[:, d:2 * d]) * _dot(hn, win_ref[:, 2 * d:3 * d])
        base = SUBLANES + r0
        cu_ref[base:base + sub_rows, :] = cu
        conv = (cu_ref[base - 2:base - 2 + sub_rows, :] * cw[0:1, :]
                + cu_ref[base - 1:base - 1 + sub_rows, :] * cw[1:2, :]
                + cu * cw[2:3, :])
        y = _dot(hn, win_ref[:, 0:d]) * conv
        mix = _dot(y, wout_ref[...])
        o_ref[r0:r0 + sub_rows, :] = h + _rms(mix, post_ref[...])


def _conv_mixer(h, pre, w_in, conv_w, w_out, post, layer, seq_len):
    t, d = h.shape
    tm = 2 * ROW_TILE
    kern = functools.partial(_conv_mixer_kernel, tiles_per_seq=seq_len // tm, sub_rows=ROW_TILE)
    return pl.pallas_call(
        kern,
        grid=(t // tm,),
        in_specs=[
            pl.BlockSpec((tm, d), lambda i: (i, 0)),
            _const_spec((1, d)),
            _layer_spec((d, 3 * d), layer),
            _const_spec((CONV_W, d)),
            _layer_spec((d, d), layer),
            _const_spec((1, d)),
        ],
        out_specs=pl.BlockSpec((tm, d), lambda i: (i, 0)),
        out_shape=jax.ShapeDtypeStruct((t, d), F32),
        scratch_shapes=[pltpu.VMEM((tm + SUBLANES, d), F32)],
        compiler_params=_params(("arbitrary",)),
        name="conv_mixer",
    )(h, pre, w_in, conv_w, w_out, post)


def _ffn_kernel(h_ref, pre_ref, wgu_ref, wd_ref, post_ref, o_ref, *, chunks):
    f = wd_ref.shape[0]
    h = h_ref[...]
    hn = _rms(h, pre_ref[...])
    acc = None
    for lo, hi in chunks:
        g = _dot(hn, wgu_ref[:, lo:hi])
        u = _dot(hn, wgu_ref[:, f + lo:f + hi])
        a = g * jax.nn.sigmoid(g) * u
        part = _dot(a, wd_ref[lo:hi, :])
        acc = part if acc is None else acc + part
    o_ref[...] = h + _rms(acc, post_ref[...])


def _ffn_chunks(f):
    step = 768
    return tuple((lo, min(lo + step, f)) for lo in range(0, f, step))


def _ffn(h, pre, w_gate_up, w_down, post, layer):
    t, d = h.shape
    f = w_down.shape[1]
    tm = FFN_TILE
    row = lambda i: (i, 0)
    return pl.pallas_call(
        functools.partial(_ffn_kernel, chunks=_ffn_chunks(f)),
        grid=(t // tm,),
        in_specs=[
            pl.BlockSpec((tm, d), row),
            _const_spec((1, d)),
            _layer_spec((d, 2 * f), layer),
            _layer_spec((f, d), layer),
            _const_spec((1, d)),
        ],
        out_specs=pl.BlockSpec((tm, d), row),
        out_shape=jax.ShapeDtypeStruct((t, d), F32),
        compiler_params=_params(("parallel",)),
        name="swiglu",
    )(h, pre, w_gate_up, w_down, post)


def _attn_kernel(sink_ref, hp_ref, h_ref, ropep_ref, rope_ref, qn_ref, wq_ref, kvn_ref, wkv_ref,
                 wo_ref, post_ref, o_ref, q_ref, k_ref, vt_ref, att_t_ref, wo_bf_ref,
                 *, tiles_per_seq):
    tq = q_ref.shape[0]
    kvd = vt_ref.shape[0]
    n_heads = q_ref.shape[1] // HEAD_DIM
    group = n_heads // N_KV_HEADS
    assert group * HEAD_DIM == 2 * LANES, "one unit = the two q lane chunks of a kv head"
    first_tile = (pl.program_id(0) % tiles_per_seq) == 0

    @pl.when(pl.program_id(0) == 0)
    def _():
        wo_bf_ref[...] = wo_ref[...].astype(BF16)

    lo_half = lax.broadcasted_iota(jnp.int32, (BLOCK, LANES), 1) < HEAD_DIM

    def project_kv(rows_h, rows_rope, r0):
        n = rows_h.shape[0]
        kv = _dot(_rms(rows_h, kvn_ref[...]), wkv_ref[...])
        k = _rope(kv[:, :kvd], rows_rope)
        lo = lax.broadcasted_iota(jnp.int32, (n, LANES), 1) < HEAD_DIM
        for c in range(kvd // LANES):
            pair = k[:, c * LANES:(c + 1) * LANES]
            swapped = pltpu.roll(pair, HEAD_DIM, axis=1)
            k_ref[r0:r0 + n, (2 * c) * LANES:(2 * c + 1) * LANES] = (
                jnp.where(lo, pair, swapped).astype(BF16))
            k_ref[r0:r0 + n, (2 * c + 1) * LANES:(2 * c + 2) * LANES] = (
                jnp.where(lo, swapped, pair).astype(BF16))
        vt_ref[:, r0:r0 + n] = kv[:, kvd:].T.astype(BF16)

    def project(piece):
        rows = slice(piece * ATTN_PIECE, (piece + 1) * ATTN_PIECE)
        h = h_ref[rows, :]
        if piece == 0:
            project_kv(hp_ref[...], ropep_ref[...], 0)
        project_kv(h, rope_ref[rows, :], BLOCK + piece * ATTN_PIECE)
        q = _rope(_dot(_rms(h, qn_ref[...]), wq_ref[...]), rope_ref[rows, :])
        q_ref[rows, :] = (q * (1.0 / math.sqrt(HEAD_DIM))).astype(BF16)

    def output(piece):
        rows = slice(piece * ATTN_PIECE, (piece + 1) * ATTN_PIECE)
        att = att_t_ref[:, rows].T
        proj = _dot(att, wo_bf_ref[...])
        o_ref[rows, :] = h_ref[rows, :] + _rms(proj, post_ref[...])

    unit_heads = (0, 2, 1, 3)
    width = group * BLOCK
    key_j = lax.broadcasted_iota(jnp.int32, (BLOCK, width), 0)
    lane = lax.broadcasted_iota(jnp.int32, (BLOCK, width), 1)
    from_prev = key_j > (lane % BLOCK)
    seg = lax.broadcasted_iota(jnp.int32, (1, width), 1) // BLOCK
    ones_rows = jnp.ones((BF16_ROWS, 2 * BLOCK), BF16)

    prev_bias = jnp.where(first_tile, NEG, 0.0).astype(F32)

    def scores(blk, g):
        r0 = blk * BLOCK
        k_win = k_ref[r0:r0 + 2 * BLOCK, g * LANES:(g + 1) * LANES]
        qc0 = q_ref[r0:r0 + BLOCK, (2 * g) * LANES:(2 * g + 1) * LANES]
        qc1 = q_ref[r0:r0 + BLOCK, (2 * g + 1) * LANES:(2 * g + 2) * LANES]
        zero = jnp.zeros_like(qc0)
        q_split = jnp.concatenate(
            [jnp.where(lo_half, qc0, zero), jnp.where(lo_half, qc1, zero),
             jnp.where(lo_half, zero, qc0), jnp.where(lo_half, zero, qc1)], axis=0)
        return _dot_nt(k_win, q_split)

    def attend(blk, g, s_t):
        r0 = blk * BLOCK
        v_win = vt_ref[g * HEAD_DIM:(g + 1) * HEAD_DIM, r0:r0 + 2 * BLOCK]
        s_prev = s_t[0:BLOCK]
        if blk == 0:
            s_prev = s_prev + prev_bias
        s_win = jnp.where(from_prev, s_prev, s_t[BLOCK:])
        m = jnp.max(s_win, axis=0, keepdims=True)
        e = jnp.exp(s_win - m)
        p_t = jnp.concatenate([jnp.where(from_prev, e, 0.0),
                               jnp.where(from_prev, 0.0, e)], axis=0).astype(BF16)
        v_aug = jnp.concatenate([v_win, ones_rows], axis=0)
        o_aug = _dot(v_aug, p_t)
        heads = [g * group + u for u in unit_heads]
        sink = sink_ref[heads[-1]]
        for n in range(group - 2, -1, -1):
            sink = jnp.where(seg == n, sink_ref[heads[n]], sink)
        denom = o_aug[HEAD_DIM:HEAD_DIM + 1, :] + jnp.exp(sink - m)
        o_t = (o_aug[0:HEAD_DIM, :] / denom).astype(BF16)
        for n, head in enumerate(heads):
            att_t_ref[head * HEAD_DIM:(head + 1) * HEAD_DIM, r0:r0 + BLOCK] = (
                o_t[:, n * BLOCK:(n + 1) * BLOCK])

    n_pieces = tq // ATTN_PIECE
    units = [(blk, g) for blk in range(tq // BLOCK) for g in range(N_KV_HEADS)]
    per_piece = len(units) // n_pieces
    assert SCORE_LOOKAHEAD <= per_piece
    project(0)
    if n_pieces > 1:
        project(1)
    pending = [scores(*u) for u in units[:SCORE_LOOKAHEAD]]
    for piece in range(n_pieces):
        for n in range(piece * per_piece, (piece + 1) * per_piece):
            if n + SCORE_LOOKAHEAD < len(units):
                pending.append(scores(*units[n + SCORE_LOOKAHEAD]))
            attend(*units[n], pending.pop(0))
        if piece + 2 < n_pieces:
            project(piece + 2)
        if piece >= 1:
            output(piece - 1)
    output(n_pieces - 1)


def _attention(h, q_norm, w_q, kv_norm, w_kv, rope_table, sinks, w_o, post, layer, seq_len):
    t, d = h.shape
    qd = w_q.shape[2]
    kvd = w_kv.shape[1] // 2
    tq = ATTN_TILE
    tps = seq_len // tq
    bpt = tq // BLOCK
    row = lambda i: (i, 0)
    prev_blk = lambda i: (jnp.maximum(i * bpt - 1, (i // tps) * tps * bpt), 0)
    prev_pos = lambda i: (jnp.maximum((i % tps) * bpt - 1, 0), 0)
    rope_w = rope_table.shape[1]
    kern = functools.partial(_attn_kernel, tiles_per_seq=tps)
    return pl.pallas_call(
        kern,
        grid=(t // tq,),
        in_specs=[
            pl.BlockSpec(memory_space=pltpu.SMEM),
            pl.BlockSpec((BLOCK, d), prev_blk),
            pl.BlockSpec((tq, d), row),
            pl.BlockSpec((BLOCK, rope_w), prev_pos),
            pl.BlockSpec((tq, rope_w), lambda i: (i % tps, 0)),
            _const_spec((1, d)),
            _layer_spec((d, qd), layer),
            _const_spec((1, d)),
            _const_spec((d, 2 * kvd)),
            _layer_spec((qd, d), layer),
            _const_spec((1, d)),
        ],
        out_specs=pl.BlockSpec((tq, d), row),
        out_shape=jax.ShapeDtypeStruct((t, d), F32),
        scratch_shapes=[
            pltpu.VMEM((tq, qd), BF16),
            pltpu.VMEM((tq + BLOCK, 2 * kvd), BF16),
            pltpu.VMEM((kvd, tq + BLOCK), BF16),
            pltpu.VMEM((qd, tq), BF16),
            pltpu.VMEM((qd, d), BF16),
        ],
        compiler_params=_params(("arbitrary",)),
        name="swa_attention",
    )(sinks, h, h, rope_table, rope_table, q_norm, w_q, kv_norm, w_kv, w_o, post)


def kernel(x, a_pre_norm, a_w_in, a_conv_w, a_w_out, a_post_norm, ffn_pre_norm, ffn_w_gate_up,
           ffn_w_down, ffn_post_norm, kv_norm, w_kv, b_pre_norm, b_w_q, b_sinks, b_w_o,
           b_post_norm):
    bsz, s, d = x.shape
    n_a = a_w_in.shape[0]
    depth = ffn_w_gate_up.shape[0]
    rope_table = _rope_tables(s)
    vec = lambda g: g.reshape(1, -1)

    h = x.reshape(bsz * s, d)
    for l in range(depth):
        if l < n_a:
            h = _conv_mixer(h, vec(a_pre_norm[l]), a_w_in, a_conv_w[l], a_w_out,
                            vec(a_post_norm[l]), l, s)
        else:
            if l == n_a:
                h_kv = h
            assert l == n_a, "K/V are projected inside the first attention layer's call only"
            j = l - n_a
            h = _attention(h_kv, vec(b_pre_norm[j]), b_w_q, vec(kv_norm), w_kv, rope_table,
                           b_sinks[j], b_w_o, vec(b_post_norm[j]), j, s)
        h = _ffn(h, vec(ffn_pre_norm[l]), ffn_w_gate_up, ffn_w_down, vec(ffn_post_norm[l]), l)
    return h.reshape(bsz, s, d)
```

```python
import functools
import math

import jax
import jax.numpy as jnp
import numpy as np
from jax import lax
from jax.experimental import pallas as pl
from jax.experimental.pallas import tpu as pltpu

CONV_W = 3
HEAD_DIM = 64
N_KV_HEADS = 4
WINDOW = 128
BLOCK = 128
ROT_DIM = HEAD_DIM // 4
ROPE_THETA = 500000.0
EPS = 1e-6
NEG = -1e30
LOG2E = math.log2(math.e)

LANES = 128
SUBLANES = 8
BF16_ROWS = 16
ROW_TILE = 512
FFN_TILE = 512
ATTN_TILE = 1024
ATTN_PIECE = 256
SCORE_LOOKAHEAD = 3
VMEM_LIMIT = 60 * 1024 * 1024

F32 = jnp.float32
BF16 = jnp.bfloat16

assert WINDOW == BLOCK and LANES == 2 * HEAD_DIM


def _rms(x, g):
    r = lax.rsqrt(jnp.mean(x * x, axis=-1, keepdims=True) + EPS)
    return (x * r) * g


def _dot(a, b):
    return jnp.dot(a, b, preferred_element_type=F32)


def _dot_nt(a, b):
    return lax.dot_general(a, b, (((1,), (1,)), ((), ())), preferred_element_type=F32)


def _const_spec(shape):
    return pl.BlockSpec(shape, lambda *_: (0,) * len(shape), pipeline_mode=pl.Buffered(1))


def _layer_spec(shape, layer):
    return pl.BlockSpec((None,) + tuple(shape), lambda *_: (layer,) + (0,) * len(shape),
                        pipeline_mode=pl.Buffered(1))


def _params(semantics):
    return pltpu.CompilerParams(dimension_semantics=semantics, vmem_limit_bytes=VMEM_LIMIT)


def _rope(t, table):
    tb = table.astype(BF16)
    cos_t, sin_up, sin_dn = (tb[:, j * LANES:(j + 1) * LANES] for j in range(3))
    outs = []
    for j in range(t.shape[1] // LANES):
        c = t[:, j * LANES:(j + 1) * LANES].astype(BF16)
        outs.append(c * cos_t
                    + pltpu.roll(c, ROT_DIM // 2, axis=1) * sin_up
                    + pltpu.roll(c, LANES - ROT_DIM // 2, axis=1) * sin_dn)
    return jnp.concatenate(outs, axis=1)


def _rope_tables(seq_len):
    half = ROT_DIM // 2
    f32 = np.float32
    inv_freq = f32(ROPE_THETA) ** (-np.arange(0, ROT_DIM, 2, dtype=f32) / f32(ROT_DIM))
    ang = np.arange(seq_len, dtype=f32)[:, None] * inv_freq[None, :]
    cos, sin = np.cos(ang), np.sin(ang)
    ones = np.ones((seq_len, HEAD_DIM - ROT_DIM), f32)
    zeros = np.zeros((seq_len, HEAD_DIM - ROT_DIM), f32)
    zh = np.zeros((seq_len, half), f32)
    reps = LANES // HEAD_DIM
    cos_t = np.tile(np.concatenate([cos, cos, ones], axis=1), (1, reps))
    sin_up = np.tile(np.concatenate([zh, sin, zeros], axis=1), (1, reps))
    sin_dn = np.tile(np.concatenate([-sin, zh, zeros], axis=1), (1, reps))
    return jnp.asarray(np.concatenate([cos_t, sin_up, sin_dn], axis=1), F32)


def _conv_mixer_kernel(h_ref, pre_ref, win_ref, cw_ref, wout_ref, post_ref, o_ref, cu_ref,
                       *, tiles_per_seq, sub_rows):
    tm, d = h_ref.shape
    i = pl.program_id(0)

    @pl.when(i % tiles_per_seq == 0)
    def _():
        cu_ref[0:SUBLANES, :] = jnp.zeros((SUBLANES, d), F32)

    @pl.when(i % tiles_per_seq != 0)
    def _():
        cu_ref[0:SUBLANES, :] = cu_ref[tm:tm + SUBLANES, :]

    cw = cw_ref[...]
    for r0 in range(0, tm, sub_rows):
        h = h_ref[r0:r0 + sub_rows, :]
        hn = _rms(h, pre_ref[...])
        cu = _dot(hn, win_ref[:, d:2 * d]) * _dot(hn, win_ref[:, 2 * d:3 * d])
        base = SUBLANES + r0
        cu_ref[base:base + sub_rows, :] = cu
        conv = (cu_ref[base - 2:base - 2 + sub_rows, :] * cw[0:1, :]
                + cu_ref[base - 1:base - 1 + sub_rows, :] * cw[1:2, :]
                + cu * cw[2:3, :])
        y = _dot(hn, win_ref[:, 0:d]) * conv
        mix = _dot(y, wout_ref[...])
        o_ref[r0:r0 + sub_rows, :] = h + _rms(mix, post_ref[...])


def _conv_mixer(h, pre, w_in, conv_w, w_out, post, layer, seq_len):
    t, d = h.shape
    tm = 2 * ROW_TILE
    kern = functools.partial(_conv_mixer_kernel, tiles_per_seq=seq_len // tm, sub_rows=ROW_TILE)
    return pl.pallas_call(
        kern,
        grid=(t // tm,),
        in_specs=[
            pl.BlockSpec((tm, d), lambda i: (i, 0)),
            _const_spec((1, d)),
            _layer_spec((d, 3 * d), layer),
            _const_spec((CONV_W, d)),
            _layer_spec((d, d), layer),
            _const_spec((1, d)),
        ],
        out_specs=pl.BlockSpec((tm, d), lambda i: (i, 0)),
        out_shape=jax.ShapeDtypeStruct((t, d), F32),
        scratch_shapes=[pltpu.VMEM((tm + SUBLANES, d), F32)],
        compiler_params=_params(("arbitrary",)),
        name="conv_mixer",
    )(h, pre, w_in, conv_w, w_out, post)


def _ffn_kernel(h_ref, pre_ref, wgu_ref, wd_ref, post_ref, o_ref, *, chunks):
    f = wd_ref.shape[0]
    h = h_ref[...]
    hn = _rms(h, pre_ref[...])
    acc = None
    for lo, hi in chunks:
        g = _dot(hn, wgu_ref[:, lo:hi])
        u = _dot(hn, wgu_ref[:, f + lo:f + hi])
        a = g * jax.nn.sigmoid(g) * u
        part = _dot(a, wd_ref[lo:hi, :])
        acc = part if acc is None else acc + part
    o_ref[...] = h + _rms(acc, post_ref[...])


def _ffn_chunks(f):
    step = 768
    return tuple((lo, min(lo + step, f)) for lo in range(0, f, step))


def _ffn(h, pre, w_gate_up, w_down, post, layer):
    t, d = h.shape
    f = w_down.shape[1]
    tm = FFN_TILE
    row = lambda i: (i, 0)
    return pl.pallas_call(
        functools.partial(_ffn_kernel, chunks=_ffn_chunks(f)),
        grid=(t // tm,),
        in_specs=[
            pl.BlockSpec((tm, d), row),
            _const_spec((1, d)),
            _layer_spec((d, 2 * f), layer),
            _layer_spec((f, d), layer),
            _const_spec((1, d)),
        ],
        out_specs=pl.BlockSpec((tm, d), row),
        out_shape=jax.ShapeDtypeStruct((t, d), F32),
        compiler_params=_params(("parallel",)),
        name="swiglu",
    )(h, pre, w_gate_up, w_down, post)


def _attn_kernel(sink_ref, hp_ref, h_ref, ropep_ref, rope_ref, qn_ref, wq_ref, kvn_ref, wkv_ref,
                 wo_ref, post_ref, o_ref, q_ref, k_ref, vt_ref, att_t_ref, wo_bf_ref,
                 *, tiles_per_seq):
    tq = q_ref.shape[0]
    kvd = vt_ref.shape[0]
    n_heads = q_ref.shape[1] // HEAD_DIM
    group = n_heads // N_KV_HEADS
    assert group * HEAD_DIM == 2 * LANES, "one unit = the two q lane chunks of a kv head"
    first_tile = (pl.program_id(0) % tiles_per_seq) == 0

    @pl.when(pl.program_id(0) == 0)
    def _():
        wo_bf_ref[...] = wo_ref[...].astype(BF16)

    lo_half = lax.broadcasted_iota(jnp.int32, (BLOCK, LANES), 1) < HEAD_DIM

    def unit_rms(x):
        return x * lax.rsqrt(jnp.mean(x * x, axis=-1, keepdims=True) + EPS)

    def project_kv(rows_unit, rows_rope, r0):
        n = rows_unit.shape[0]
        kv = _dot(rows_unit * kvn_ref[...], wkv_ref[...])
        k = _rope(kv[:, :kvd], rows_rope)
        lo = lax.broadcasted_iota(jnp.int32, (n, LANES), 1) < HEAD_DIM
        for c in range(kvd // LANES):
            pair = k[:, c * LANES:(c + 1) * LANES]
            swapped = pltpu.roll(pair, HEAD_DIM, axis=1)
            k_ref[r0:r0 + n, (2 * c) * LANES:(2 * c + 1) * LANES] = jnp.where(lo, pair, swapped)
            k_ref[r0:r0 + n, (2 * c + 1) * LANES:(2 * c + 2) * LANES] = jnp.where(lo, swapped, pair)
        vt_ref[:, r0:r0 + n] = kv[:, kvd:].T.astype(BF16)

    q_gain = qn_ref[...] * (LOG2E / math.sqrt(HEAD_DIM))

    def project(piece):
        rows = slice(piece * ATTN_PIECE, (piece + 1) * ATTN_PIECE)
        if piece == 0:
            project_kv(unit_rms(hp_ref[...]), ropep_ref[...], 0)
        h_unit = unit_rms(h_ref[rows, :])
        project_kv(h_unit, rope_ref[rows, :], BLOCK + piece * ATTN_PIECE)
        q_ref[rows, :] = _rope(_dot(h_unit * q_gain, wq_ref[...]), rope_ref[rows, :])

    def output(piece):
        rows = slice(piece * ATTN_PIECE, (piece + 1) * ATTN_PIECE)
        att = att_t_ref[:, rows].T
        proj = _dot(att, wo_bf_ref[...])
        o_ref[rows, :] = h_ref[rows, :] + _rms(proj, post_ref[...])

    unit_heads = (0, 2, 1, 3)
    width = group * BLOCK
    key_j = lax.broadcasted_iota(jnp.int32, (BLOCK, width), 0)
    lane = lax.broadcasted_iota(jnp.int32, (BLOCK, width), 1)
    from_prev = key_j > (lane % BLOCK)
    seg = lax.broadcasted_iota(jnp.int32, (1, width), 1) // BLOCK
    ones_rows = jnp.ones((BF16_ROWS, 2 * BLOCK), BF16)

    prev_bias = jnp.where(first_tile, NEG, 0.0).astype(F32)

    def scores(blk, g):
        r0 = blk * BLOCK
        k_win = k_ref[r0:r0 + 2 * BLOCK, g * LANES:(g + 1) * LANES]
        qc0 = q_ref[r0:r0 + BLOCK, (2 * g) * LANES:(2 * g + 1) * LANES]
        qc1 = q_ref[r0:r0 + BLOCK, (2 * g + 1) * LANES:(2 * g + 2) * LANES]
        zero = jnp.zeros_like(qc0)
        q_split = jnp.concatenate(
            [jnp.where(lo_half, qc0, zero), jnp.where(lo_half, qc1, zero),
             jnp.where(lo_half, zero, qc0), jnp.where(lo_half, zero, qc1)], axis=0)
        return _dot_nt(k_win, q_split)

    def attend(blk, g, s_t):
        r0 = blk * BLOCK
        v_win = vt_ref[g * HEAD_DIM:(g + 1) * HEAD_DIM, r0:r0 + 2 * BLOCK]
        s_prev = s_t[0:BLOCK]
        if blk == 0:
            s_prev = s_prev + prev_bias
        s_win = jnp.where(from_prev, s_prev, s_t[BLOCK:])
        m = jnp.max(s_win, axis=0, keepdims=True)
        e = jnp.exp2(s_win - m)
        p_t = jnp.concatenate([jnp.where(from_prev, e, 0.0),
                               jnp.where(from_prev, 0.0, e)], axis=0).astype(BF16)
        v_aug = jnp.concatenate([v_win, ones_rows], axis=0)
        o_aug = _dot(v_aug, p_t)
        heads = [g * group + u for u in unit_heads]
        sink = sink_ref[heads[-1]] * LOG2E
        for n in range(group - 2, -1, -1):
            sink = jnp.where(seg == n, sink_ref[heads[n]] * LOG2E, sink)
        inv = 1.0 / (o_aug[HEAD_DIM:HEAD_DIM + 1, :] + jnp.exp2(sink - m))
        o_t = (o_aug[0:HEAD_DIM, :] * inv).astype(BF16)
        for n, head in enumerate(heads):
            att_t_ref[head * HEAD_DIM:(head + 1) * HEAD_DIM, r0:r0 + BLOCK] = (
                o_t[:, n * BLOCK:(n + 1) * BLOCK])

    n_pieces = tq // ATTN_PIECE
    units = [(blk, g) for blk in range(tq // BLOCK) for g in range(N_KV_HEADS)]
    per_piece = len(units) // n_pieces
    assert SCORE_LOOKAHEAD <= per_piece
    project(0)
    if n_pieces > 1:
        project(1)
    pending = [scores(*u) for u in units[:SCORE_LOOKAHEAD]]
    for piece in range(n_pieces):
        for n in range(piece * per_piece, (piece + 1) * per_piece):
            if n + SCORE_LOOKAHEAD < len(units):
                pending.append(scores(*units[n + SCORE_LOOKAHEAD]))
            attend(*units[n], pending.pop(0))
        if piece + 2 < n_pieces:
            project(piece + 2)
        if piece >= 1:
            output(piece - 1)
    output(n_pieces - 1)


def _attention(h, q_norm, w_q, kv_norm, w_kv, rope_table, sinks, w_o, post, layer, seq_len):
    t, d = h.shape
    qd = w_q.shape[2]
    kvd = w_kv.shape[1] // 2
    tq = ATTN_TILE
    tps = seq_len // tq
    bpt = tq // BLOCK
    row = lambda i: (i, 0)
    prev_blk = lambda i: (jnp.maximum(i * bpt - 1, (i // tps) * tps * bpt), 0)
    prev_pos = lambda i: (jnp.maximum((i % tps) * bpt - 1, 0), 0)
    rope_w = rope_table.shape[1]
    kern = functools.partial(_attn_kernel, tiles_per_seq=tps)
    return pl.pallas_call(
        kern,
        grid=(t // tq,),
        in_specs=[
            pl.BlockSpec(memory_space=pltpu.SMEM),
            pl.BlockSpec((BLOCK, d), prev_blk),
            pl.BlockSpec((tq, d), row),
            pl.BlockSpec((BLOCK, rope_w), prev_pos),
            pl.BlockSpec((tq, rope_w), lambda i: (i % tps, 0)),
            _const_spec((1, d)),
            _layer_spec((d, qd), layer),
            _const_spec((1, d)),
            _const_spec((d, 2 * kvd)),
            _layer_spec((qd, d), layer),
            _const_spec((1, d)),
        ],
        out_specs=pl.BlockSpec((tq, d), row),
        out_shape=jax.ShapeDtypeStruct((t, d), F32),
        scratch_shapes=[
            pltpu.VMEM((tq, qd), BF16),
            pltpu.VMEM((tq + BLOCK, 2 * kvd), BF16),
            pltpu.VMEM((kvd, tq + BLOCK), BF16),
            pltpu.VMEM((qd, tq), BF16),
            pltpu.VMEM((qd, d), BF16),
        ],
        compiler_params=_params(("arbitrary",)),
        name="swa_attention",
    )(sinks, h, h, rope_table, rope_table, q_norm, w_q, kv_norm, w_kv, w_o, post)


def kernel(x, a_pre_norm, a_w_in, a_conv_w, a_w_out, a_post_norm, ffn_pre_norm, ffn_w_gate_up,
           ffn_w_down, ffn_post_norm, kv_norm, w_kv, b_pre_norm, b_w_q, b_sinks, b_w_o,
           b_post_norm):
    bsz, s, d = x.shape
    n_a = a_w_in.shape[0]
    depth = ffn_w_gate_up.shape[0]
    rope_table = _rope_tables(s)
    vec = lambda g: g.reshape(1, -1)

    h = x.reshape(bsz * s, d)
    for l in range(depth):
        if l < n_a:
            h = _conv_mixer(h, vec(a_pre_norm[l]), a_w_in, a_conv_w[l], a_w_out,
                            vec(a_post_norm[l]), l, s)
        else:
            if l == n_a:
                h_kv = h
            assert l == n_a, "K/V are projected inside the first attention layer's call only"
            j = l - n_a
            h = _attention(h_kv, vec(b_pre_norm[j]), b_w_q, vec(kv_norm), w_kv, rope_table,
                           b_sinks[j], b_w_o, vec(b_post_norm[j]), j, s)
        h = _ffn(h, vec(ffn_pre_norm[l]), ffn_w_gate_up, ffn_w_down, vec(ffn_post_norm[l]), l)
    return h.reshape(bsz, s, d)
```

```python
import functools
import math

import jax
import jax.numpy as jnp
import numpy as np
from jax import lax
from jax.experimental import pallas as pl
from jax.experimental.pallas import tpu as pltpu

CONV_W = 3
HEAD_DIM = 64
N_KV_HEADS = 4
WINDOW = 128
BLOCK = 128
ROT_DIM = HEAD_DIM // 4
ROPE_THETA = 500000.0
EPS = 1e-6
NEG = -1e30
LOG2E = math.log2(math.e)

LANES = 128
SUBLANES = 8
BF16_ROWS = 16
ROW_TILE = 512
FFN_TILE = 1024
ATTN_TILE = 1024
ATTN_PIECE = 256
SCORE_LOOKAHEAD = 3
VMEM_LIMIT = 60 * 1024 * 1024

F32 = jnp.float32
BF16 = jnp.bfloat16

assert WINDOW == BLOCK and LANES == 2 * HEAD_DIM


def _rms(x, g):
    r = lax.rsqrt(jnp.mean(x * x, axis=-1, keepdims=True) + EPS)
    return (x * r) * g


def _dot(a, b):
    return jnp.dot(a, b, preferred_element_type=F32)


def _dot_nt(a, b):
    return lax.dot_general(a, b, (((1,), (1,)), ((), ())), preferred_element_type=F32)


def _const_spec(shape):
    return pl.BlockSpec(shape, lambda *_: (0,) * len(shape), pipeline_mode=pl.Buffered(1))


def _layer_spec(shape, layer):
    return pl.BlockSpec((None,) + tuple(shape), lambda *_: (layer,) + (0,) * len(shape),
                        pipeline_mode=pl.Buffered(1))


def _params(semantics):
    return pltpu.CompilerParams(dimension_semantics=semantics, vmem_limit_bytes=VMEM_LIMIT)


def _rope(t, table):
    tb = table.astype(BF16)
    cos_t, sin_up, sin_dn = (tb[:, j * LANES:(j + 1) * LANES] for j in range(3))
    outs = []
    for j in range(t.shape[1] // LANES):
        c = t[:, j * LANES:(j + 1) * LANES].astype(BF16)
        outs.append(c * cos_t
                    + pltpu.roll(c, ROT_DIM // 2, axis=1) * sin_up
                    + pltpu.roll(c, LANES - ROT_DIM // 2, axis=1) * sin_dn)
    return jnp.concatenate(outs, axis=1)


def _rope_tables(seq_len):
    half = ROT_DIM // 2
    f32 = np.float32
    inv_freq = f32(ROPE_THETA) ** (-np.arange(0, ROT_DIM, 2, dtype=f32) / f32(ROT_DIM))
    ang = np.arange(seq_len, dtype=f32)[:, None] * inv_freq[None, :]
    cos, sin = np.cos(ang), np.sin(ang)
    ones = np.ones((seq_len, HEAD_DIM - ROT_DIM), f32)
    zeros = np.zeros((seq_len, HEAD_DIM - ROT_DIM), f32)
    zh = np.zeros((seq_len, half), f32)
    reps = LANES // HEAD_DIM
    cos_t = np.tile(np.concatenate([cos, cos, ones], axis=1), (1, reps))
    sin_up = np.tile(np.concatenate([zh, sin, zeros], axis=1), (1, reps))
    sin_dn = np.tile(np.concatenate([-sin, zh, zeros], axis=1), (1, reps))
    return jnp.asarray(np.concatenate([cos_t, sin_up, sin_dn], axis=1), F32)


def _ffn_weight_cast_specs(w_gate_up, w_down, layer, n_steps):
    _, d, f2 = w_gate_up.shape
    _, f, _ = w_down.shape
    assert d % (n_steps * BF16_ROWS) == 0 and f % (n_steps * BF16_ROWS) == 0
    slab = lambda i: (layer, i, 0)
    in_specs = [pl.BlockSpec((None, d // n_steps, f2), slab),
                pl.BlockSpec((None, f // n_steps, d), slab)]
    out_specs = [pl.BlockSpec((d // n_steps, f2), lambda i: (i, 0)),
                 pl.BlockSpec((f // n_steps, d), lambda i: (i, 0))]
    out_shape = [jax.ShapeDtypeStruct((d, f2), BF16), jax.ShapeDtypeStruct((f, d), BF16)]
    return in_specs, out_specs, out_shape


def _cast_ffn_weights(wgu_src, wd_src, wgu_dst, wd_dst):
    wgu_dst[...] = wgu_src[...].astype(BF16)
    wd_dst[...] = wd_src[...].astype(BF16)


def _conv_mixer_kernel(h_ref, pre_ref, win_ref, cw_ref, wout_ref, post_ref, wgu_src, wd_src,
                       o_ref, wgu_dst, wd_dst, cu_ref, *, tiles_per_seq, sub_rows):
    tm, d = h_ref.shape
    i = pl.program_id(0)
    _cast_ffn_weights(wgu_src, wd_src, wgu_dst, wd_dst)

    @pl.when(i % tiles_per_seq == 0)
    def _():
        cu_ref[0:SUBLANES, :] = jnp.zeros((SUBLANES, d), F32)

    @pl.when(i % tiles_per_seq != 0)
    def _():
        cu_ref[0:SUBLANES, :] = cu_ref[tm:tm + SUBLANES, :]

    cw = cw_ref[...]
    for r0 in range(0, tm, sub_rows):
        h = h_ref[r0:r0 + sub_rows, :]
        hn = _rms(h, pre_ref[...])
        cu = _dot(hn, win_ref[:, d:2 * d]) * _dot(hn, win_ref[:, 2 * d:3 * d])
        base = SUBLANES + r0
        cu_ref[base:base + sub_rows, :] = cu
        conv = (cu_ref[base - 2:base - 2 + sub_rows, :] * cw[0:1, :]
                + cu_ref[base - 1:base - 1 + sub_rows, :] * cw[1:2, :]
                + cu * cw[2:3, :])
        y = _dot(hn, win_ref[:, 0:d]) * conv
        mix = _dot(y, wout_ref[...])
        o_ref[r0:r0 + sub_rows, :] = h + _rms(mix, post_ref[...])


def _conv_mixer(h, pre, w_in, conv_w, w_out, post, ffn_w_gate_up, ffn_w_down, layer, seq_len):
    t, d = h.shape
    tm = 2 * ROW_TILE
    n_steps = t // tm
    cast_in, cast_out, cast_shape = _ffn_weight_cast_specs(ffn_w_gate_up, ffn_w_down, layer, n_steps)
    kern = functools.partial(_conv_mixer_kernel, tiles_per_seq=seq_len // tm, sub_rows=ROW_TILE)
    return pl.pallas_call(
        kern,
        grid=(n_steps,),
        in_specs=[
            pl.BlockSpec((tm, d), lambda i: (i, 0)),
            _const_spec((1, d)),
            _layer_spec((d, 3 * d), layer),
            _const_spec((CONV_W, d)),
            _layer_spec((d, d), layer),
            _const_spec((1, d)),
            *cast_in,
        ],
        out_specs=[pl.BlockSpec((tm, d), lambda i: (i, 0)), *cast_out],
        out_shape=[jax.ShapeDtypeStruct((t, d), F32), *cast_shape],
        scratch_shapes=[pltpu.VMEM((tm + SUBLANES, d), F32)],
        compiler_params=_params(("arbitrary",)),
        name="conv_mixer",
    )(h, pre, w_in, conv_w, w_out, post, ffn_w_gate_up, ffn_w_down)


def _ffn_kernel(h_ref, pre_ref, wgu_ref, wd_ref, post_ref, o_ref, *, chunks, sub_rows):
    f = wd_ref.shape[0]
    for r0 in range(0, h_ref.shape[0], sub_rows):
        h = h_ref[r0:r0 + sub_rows, :]
        hn = _rms(h, pre_ref[...]).astype(BF16)
        acc = None
        for lo, hi in chunks:
            g = _dot(hn, wgu_ref[:, lo:hi])
            u = _dot(hn, wgu_ref[:, f + lo:f + hi])
            a = (g * jax.nn.sigmoid(g) * u).astype(BF16)
            part = _dot(a, wd_ref[lo:hi, :])
            acc = part if acc is None else acc + part
        o_ref[r0:r0 + sub_rows, :] = h + _rms(acc, post_ref[...])


def _ffn_chunks(f):
    step = 768
    return tuple((lo, min(lo + step, f)) for lo in range(0, f, step))


def _ffn(h, pre, w_gate_up, w_down, post):
    t, d = h.shape
    f = w_down.shape[0]
    tm = FFN_TILE
    row = lambda i: (i, 0)
    return pl.pallas_call(
        functools.partial(_ffn_kernel, chunks=_ffn_chunks(f), sub_rows=ROW_TILE),
        grid=(t // tm,),
        in_specs=[
            pl.BlockSpec((tm, d), row),
            _const_spec((1, d)),
            _const_spec((d, 2 * f)),
            _const_spec((f, d)),
            _const_spec((1, d)),
        ],
        out_specs=pl.BlockSpec((tm, d), row),
        out_shape=jax.ShapeDtypeStruct((t, d), F32),
        compiler_params=_params(("parallel",)),
        name="swiglu",
    )(h, pre, w_gate_up, w_down, post)


def _attn_kernel(sink_ref, hp_ref, h_ref, ropep_ref, rope_ref, qn_ref, wq_ref, kvn_ref, wkv_ref,
                 wo_ref, post_ref, wgu_src, wd_src, o_ref, wgu_dst, wd_dst,
                 q_ref, k_ref, vt_ref, att_t_ref, wo_bf_ref, *, tiles_per_seq):
    _cast_ffn_weights(wgu_src, wd_src, wgu_dst, wd_dst)
    tq = q_ref.shape[0]
    kvd = vt_ref.shape[0]
    n_heads = q_ref.shape[1] // HEAD_DIM
    group = n_heads // N_KV_HEADS
    assert group * HEAD_DIM == 2 * LANES, "one unit = the two q lane chunks of a kv head"
    first_tile = (pl.program_id(0) % tiles_per_seq) == 0

    @pl.when(pl.program_id(0) == 0)
    def _():
        wo_bf_ref[...] = wo_ref[...].astype(BF16)

    lo_half = lax.broadcasted_iota(jnp.int32, (BLOCK, LANES), 1) < HEAD_DIM

    def unit_rms(x):
        return x * lax.rsqrt(jnp.mean(x * x, axis=-1, keepdims=True) + EPS)

    def project_kv(rows_unit, rows_rope, r0):
        n = rows_unit.shape[0]
        kv = _dot(rows_unit * kvn_ref[...], wkv_ref[...])
        k = _rope(kv[:, :kvd], rows_rope)
        lo = lax.broadcasted_iota(jnp.int32, (n, LANES), 1) < HEAD_DIM
        for c in range(kvd // LANES):
            pair = k[:, c * LANES:(c + 1) * LANES]
            swapped = pltpu.roll(pair, HEAD_DIM, axis=1)
            k_ref[r0:r0 + n, (2 * c) * LANES:(2 * c + 1) * LANES] = jnp.where(lo, pair, swapped)
            k_ref[r0:r0 + n, (2 * c + 1) * LANES:(2 * c + 2) * LANES] = jnp.where(lo, swapped, pair)
        vt_ref[:, r0:r0 + n] = kv[:, kvd:].T.astype(BF16)

    q_gain = qn_ref[...] * (LOG2E / math.sqrt(HEAD_DIM))

    def project(piece):
        rows = slice(piece * ATTN_PIECE, (piece + 1) * ATTN_PIECE)
        if piece == 0:
            project_kv(unit_rms(hp_ref[...]), ropep_ref[...], 0)
        h_unit = unit_rms(h_ref[rows, :])
        project_kv(h_unit, rope_ref[rows, :], BLOCK + piece * ATTN_PIECE)
        q_ref[rows, :] = _rope(_dot(h_unit * q_gain, wq_ref[...]), rope_ref[rows, :])

    def output(piece):
        rows = slice(piece * ATTN_PIECE, (piece + 1) * ATTN_PIECE)
        att = att_t_ref[:, rows].T
        proj = _dot(att, wo_bf_ref[...])
        o_ref[rows, :] = h_ref[rows, :] + _rms(proj, post_ref[...])

    unit_heads = (0, 2, 1, 3)
    width = group * BLOCK
    key_j = lax.broadcasted_iota(jnp.int32, (BLOCK, width), 0)
    lane = lax.broadcasted_iota(jnp.int32, (BLOCK, width), 1)
    from_prev = key_j > (lane % BLOCK)
    seg = lax.broadcasted_iota(jnp.int32, (1, width), 1) // BLOCK
    ones_rows = jnp.ones((BF16_ROWS, 2 * BLOCK), BF16)

    prev_bias = jnp.where(first_tile, NEG, 0.0).astype(F32)

    def scores(blk, g):
        r0 = blk * BLOCK
        k_win = k_ref[r0:r0 + 2 * BLOCK, g * LANES:(g + 1) * LANES]
        qc0 = q_ref[r0:r0 + BLOCK, (2 * g) * LANES:(2 * g + 1) * LANES]
        qc1 = q_ref[r0:r0 + BLOCK, (2 * g + 1) * LANES:(2 * g + 2) * LANES]
        zero = jnp.zeros_like(qc0)
        q_split = jnp.concatenate(
            [jnp.where(lo_half, qc0, zero), jnp.where(lo_half, qc1, zero),
             jnp.where(lo_half, zero, qc0), jnp.where(lo_half, zero, qc1)], axis=0)
        return _dot_nt(k_win, q_split)

    def attend(blk, g, s_t):
        r0 = blk * BLOCK
        v_win = vt_ref[g * HEAD_DIM:(g + 1) * HEAD_DIM, r0:r0 + 2 * BLOCK]
        s_prev = s_t[0:BLOCK]
        if blk == 0:
            s_prev = s_prev + prev_bias
        s_win = jnp.where(from_prev, s_prev, s_t[BLOCK:])
        m = jnp.max(s_win, axis=0, keepdims=True)
        e = jnp.exp2(s_win - m)
        p_t = jnp.concatenate([jnp.where(from_prev, e, 0.0),
                               jnp.where(from_prev, 0.0, e)], axis=0).astype(BF16)
        v_aug = jnp.concatenate([v_win, ones_rows], axis=0)
        o_aug = _dot(v_aug, p_t)
        heads = [g * group + u for u in unit_heads]
        sink = sink_ref[heads[-1]] * LOG2E
        for n in range(group - 2, -1, -1):
            sink = jnp.where(seg == n, sink_ref[heads[n]] * LOG2E, sink)
        inv = 1.0 / (o_aug[HEAD_DIM:HEAD_DIM + 1, :] + jnp.exp2(sink - m))
        o_t = (o_aug[0:HEAD_DIM, :] * inv).astype(BF16)
        for n, head in enumerate(heads):
            att_t_ref[head * HEAD_DIM:(head + 1) * HEAD_DIM, r0:r0 + BLOCK] = (
                o_t[:, n * BLOCK:(n + 1) * BLOCK])

    n_pieces = tq // ATTN_PIECE
    units = [(blk, g) for blk in range(tq // BLOCK) for g in range(N_KV_HEADS)]
    per_piece = len(units) // n_pieces
    assert SCORE_LOOKAHEAD <= per_piece
    project(0)
    if n_pieces > 1:
        project(1)
    pending = [scores(*u) for u in units[:SCORE_LOOKAHEAD]]
    for piece in range(n_pieces):
        for n in range(piece * per_piece, (piece + 1) * per_piece):
            if n + SCORE_LOOKAHEAD < len(units):
                pending.append(scores(*units[n + SCORE_LOOKAHEAD]))
            attend(*units[n], pending.pop(0))
        if piece + 2 < n_pieces:
            project(piece + 2)
        if piece >= 1:
            output(piece - 1)
    output(n_pieces - 1)


def _attention(h, q_norm, w_q, kv_norm, w_kv, rope_table, sinks, w_o, post, layer,
               ffn_w_gate_up, ffn_w_down, ffn_layer, seq_len):
    t, d = h.shape
    qd = w_q.shape[2]
    kvd = w_kv.shape[1] // 2
    tq = ATTN_TILE
    tps = seq_len // tq
    bpt = tq // BLOCK
    row = lambda i: (i, 0)
    prev_blk = lambda i: (jnp.maximum(i * bpt - 1, (i // tps) * tps * bpt), 0)
    prev_pos = lambda i: (jnp.maximum((i % tps) * bpt - 1, 0), 0)
    rope_w = rope_table.shape[1]
    n_steps = t // tq
    cast_in, cast_out, cast_shape = _ffn_weight_cast_specs(ffn_w_gate_up, ffn_w_down, ffn_layer,
                                                           n_steps)
    kern = functools.partial(_attn_kernel, tiles_per_seq=tps)
    return pl.pallas_call(
        kern,
        grid=(n_steps,),
        in_specs=[
            pl.BlockSpec(memory_space=pltpu.SMEM),
            pl.BlockSpec((BLOCK, d), prev_blk),
            pl.BlockSpec((tq, d), row),
            pl.BlockSpec((BLOCK, rope_w), prev_pos),
            pl.BlockSpec((tq, rope_w), lambda i: (i % tps, 0)),
            _const_spec((1, d)),
            _layer_spec((d, qd), layer),
            _const_spec((1, d)),
            _const_spec((d, 2 * kvd)),
            _layer_spec((qd, d), layer),
            _const_spec((1, d)),
            *cast_in,
        ],
        out_specs=[pl.BlockSpec((tq, d), row), *cast_out],
        out_shape=[jax.ShapeDtypeStruct((t, d), F32), *cast_shape],
        scratch_shapes=[
            pltpu.VMEM((tq, qd), BF16),
            pltpu.VMEM((tq + BLOCK, 2 * kvd), BF16),
            pltpu.VMEM((kvd, tq + BLOCK), BF16),
            pltpu.VMEM((qd, tq), BF16),
            pltpu.VMEM((qd, d), BF16),
        ],
        compiler_params=_params(("arbitrary",)),
        name="swa_attention",
    )(sinks, h, h, rope_table, rope_table, q_norm, w_q, kv_norm, w_kv, w_o, post,
      ffn_w_gate_up, ffn_w_down)


def kernel(x, a_pre_norm, a_w_in, a_conv_w, a_w_out, a_post_norm, ffn_pre_norm, ffn_w_gate_up,
           ffn_w_down, ffn_post_norm, kv_norm, w_kv, b_pre_norm, b_w_q, b_sinks, b_w_o,
           b_post_norm):
    bsz, s, d = x.shape
    n_a = a_w_in.shape[0]
    depth = ffn_w_gate_up.shape[0]
    rope_table = _rope_tables(s)
    vec = lambda g: g.reshape(1, -1)

    h = x.reshape(bsz * s, d)
    for l in range(depth):
        if l < n_a:
            h, w_gu, w_dn = _conv_mixer(h, vec(a_pre_norm[l]), a_w_in, a_conv_w[l], a_w_out,
                                        vec(a_post_norm[l]), ffn_w_gate_up, ffn_w_down, l, s)
        else:
            assert l == n_a, "K/V are projected inside the first attention layer's call only"
            j = l - n_a
            h, w_gu, w_dn = _attention(h, vec(b_pre_norm[j]), b_w_q, vec(kv_norm), w_kv, rope_table,
                                       b_sinks[j], b_w_o, vec(b_post_norm[j]), j,
                                       ffn_w_gate_up, ffn_w_down, l, s)
        h = _ffn(h, vec(ffn_pre_norm[l]), w_gu, w_dn, vec(ffn_post_norm[l]))
    return h.reshape(bsz, s, d)
```

```python
import functools
import math

import jax
import jax.numpy as jnp
import numpy as np
from jax import lax
from jax.experimental import pallas as pl
from jax.experimental.pallas import tpu as pltpu

CONV_W = 3
HEAD_DIM = 64
N_KV_HEADS = 4
WINDOW = 128
BLOCK = 128
ROT_DIM = HEAD_DIM // 4
ROPE_THETA = 500000.0
EPS = 1e-6
NEG = -1e30
LOG2E = math.log2(math.e)

LANES = 128
SUBLANES = 8
BF16_ROWS = 16
ROW_TILE = 512
FFN_TILE = 1024
ATTN_TILE = 1024
ATTN_PIECE = 256
SCORE_LOOKAHEAD = 3
VMEM_LIMIT = 60 * 1024 * 1024

F32 = jnp.float32
BF16 = jnp.bfloat16

assert WINDOW == BLOCK and LANES == 2 * HEAD_DIM


def _rms(x, g):
    r = lax.rsqrt(jnp.mean(x * x, axis=-1, keepdims=True) + EPS)
    return (x * r) * g


def _dot(a, b):
    return jnp.dot(a, b, preferred_element_type=F32)


def _dot_nt(a, b):
    return lax.dot_general(a, b, (((1,), (1,)), ((), ())), preferred_element_type=F32)


def _const_spec(shape):
    return pl.BlockSpec(shape, lambda *_: (0,) * len(shape), pipeline_mode=pl.Buffered(1))


def _layer_spec(shape, layer):
    return pl.BlockSpec((None,) + tuple(shape), lambda *_: (layer,) + (0,) * len(shape),
                        pipeline_mode=pl.Buffered(1))


def _params(semantics):
    return pltpu.CompilerParams(dimension_semantics=semantics, vmem_limit_bytes=VMEM_LIMIT)


def _rope(t, table):
    tb = table.astype(BF16)
    cos_t, sin_up, sin_dn = (tb[:, j * LANES:(j + 1) * LANES] for j in range(3))
    outs = []
    for j in range(t.shape[1] // LANES):
        c = t[:, j * LANES:(j + 1) * LANES].astype(BF16)
        outs.append(c * cos_t
                    + pltpu.roll(c, ROT_DIM // 2, axis=1) * sin_up
                    + pltpu.roll(c, LANES - ROT_DIM // 2, axis=1) * sin_dn)
    return jnp.concatenate(outs, axis=1)


def _rope_tables(seq_len):
    half = ROT_DIM // 2
    f32 = np.float32
    inv_freq = f32(ROPE_THETA) ** (-np.arange(0, ROT_DIM, 2, dtype=f32) / f32(ROT_DIM))
    ang = np.arange(seq_len, dtype=f32)[:, None] * inv_freq[None, :]
    cos, sin = np.cos(ang), np.sin(ang)
    ones = np.ones((seq_len, HEAD_DIM - ROT_DIM), f32)
    zeros = np.zeros((seq_len, HEAD_DIM - ROT_DIM), f32)
    zh = np.zeros((seq_len, half), f32)
    reps = LANES // HEAD_DIM
    cos_t = np.tile(np.concatenate([cos, cos, ones], axis=1), (1, reps))
    sin_up = np.tile(np.concatenate([zh, sin, zeros], axis=1), (1, reps))
    sin_dn = np.tile(np.concatenate([-sin, zh, zeros], axis=1), (1, reps))
    return jnp.asarray(np.concatenate([cos_t, sin_up, sin_dn], axis=1), F32)


def _ffn_weight_cast_specs(w_gate_up, w_down, layer, n_steps):
    _, d, f2 = w_gate_up.shape
    _, f, _ = w_down.shape
    assert d % (n_steps * BF16_ROWS) == 0 and f % (n_steps * BF16_ROWS) == 0
    slab = lambda i: (layer, i, 0)
    in_specs = [pl.BlockSpec((None, d // n_steps, f2), slab),
                pl.BlockSpec((None, f // n_steps, d), slab)]
    out_specs = [pl.BlockSpec((d // n_steps, f2), lambda i: (i, 0)),
                 pl.BlockSpec((f // n_steps, d), lambda i: (i, 0))]
    out_shape = [jax.ShapeDtypeStruct((d, f2), BF16), jax.ShapeDtypeStruct((f, d), BF16)]
    return in_specs, out_specs, out_shape


def _cast_ffn_weights(wgu_src, wd_src, wgu_dst, wd_dst):
    wgu_dst[...] = wgu_src[...].astype(BF16)
    wd_dst[...] = wd_src[...].astype(BF16)


def _conv_mixer_kernel(h_ref, pre_ref, win_ref, cw_ref, wout_ref, post_ref, wgu_src, wd_src,
                       o_ref, wgu_dst, wd_dst, cu_ref, win_bf_ref, wout_bf_ref,
                       *, tiles_per_seq, sub_rows):
    tm, d = h_ref.shape
    i = pl.program_id(0)
    _cast_ffn_weights(wgu_src, wd_src, wgu_dst, wd_dst)

    @pl.when(i == 0)
    def _():
        win_bf_ref[...] = win_ref[...].astype(BF16)
        wout_bf_ref[...] = wout_ref[...].astype(BF16)

    @pl.when(i % tiles_per_seq == 0)
    def _():
        cu_ref[0:SUBLANES, :] = jnp.zeros((SUBLANES, d), F32)

    @pl.when(i % tiles_per_seq != 0)
    def _():
        cu_ref[0:SUBLANES, :] = cu_ref[tm:tm + SUBLANES, :]

    cw = cw_ref[...]
    for r0 in range(0, tm, sub_rows):
        h = h_ref[r0:r0 + sub_rows, :]
        hn = _rms(h, pre_ref[...]).astype(BF16)
        cu = _dot(hn, win_bf_ref[:, d:2 * d]) * _dot(hn, win_bf_ref[:, 2 * d:3 * d])
        base = SUBLANES + r0
        cu_ref[base:base + sub_rows, :] = cu
        conv = (cu_ref[base - 2:base - 2 + sub_rows, :] * cw[0:1, :]
                + cu_ref[base - 1:base - 1 + sub_rows, :] * cw[1:2, :]
                + cu * cw[2:3, :])
        y = (_dot(hn, win_bf_ref[:, 0:d]) * conv).astype(BF16)
        mix = _dot(y, wout_bf_ref[...])
        o_ref[r0:r0 + sub_rows, :] = h + _rms(mix, post_ref[...])


def _conv_mixer(h, pre, w_in, conv_w, w_out, post, ffn_w_gate_up, ffn_w_down, layer, seq_len):
    t, d = h.shape
    tm = 2 * ROW_TILE
    n_steps = t // tm
    cast_in, cast_out, cast_shape = _ffn_weight_cast_specs(ffn_w_gate_up, ffn_w_down, layer, n_steps)
    kern = functools.partial(_conv_mixer_kernel, tiles_per_seq=seq_len // tm, sub_rows=ROW_TILE)
    return pl.pallas_call(
        kern,
        grid=(n_steps,),
        in_specs=[
            pl.BlockSpec((tm, d), lambda i: (i, 0)),
            _const_spec((1, d)),
            _layer_spec((d, 3 * d), layer),
            _const_spec((CONV_W, d)),
            _layer_spec((d, d), layer),
            _const_spec((1, d)),
            *cast_in,
        ],
        out_specs=[pl.BlockSpec((tm, d), lambda i: (i, 0)), *cast_out],
        out_shape=[jax.ShapeDtypeStruct((t, d), F32), *cast_shape],
        scratch_shapes=[pltpu.VMEM((tm + SUBLANES, d), F32),
                        pltpu.VMEM((d, 3 * d), BF16), pltpu.VMEM((d, d), BF16)],
        compiler_params=_params(("arbitrary",)),
        name="conv_mixer",
    )(h, pre, w_in, conv_w, w_out, post, ffn_w_gate_up, ffn_w_down)


def _ffn_kernel(h_ref, pre_ref, wgu_ref, wd_ref, post_ref, o_ref, *, chunks, sub_rows):
    f = wd_ref.shape[0]
    for r0 in range(0, h_ref.shape[0], sub_rows):
        h = h_ref[r0:r0 + sub_rows, :]
        hn = _rms(h, pre_ref[...]).astype(BF16)
        acc = None
        for lo, hi in chunks:
            g = _dot(hn, wgu_ref[:, lo:hi])
            u = _dot(hn, wgu_ref[:, f + lo:f + hi])
            a = (g * jax.nn.sigmoid(g) * u).astype(BF16)
            part = _dot(a, wd_ref[lo:hi, :])
            acc = part if acc is None else acc + part
        o_ref[r0:r0 + sub_rows, :] = h + _rms(acc, post_ref[...])


def _ffn_chunks(f):
    step = 768
    return tuple((lo, min(lo + step, f)) for lo in range(0, f, step))


def _ffn(h, pre, w_gate_up, w_down, post):
    t, d = h.shape
    f = w_down.shape[0]
    tm = FFN_TILE
    row = lambda i: (i, 0)
    return pl.pallas_call(
        functools.partial(_ffn_kernel, chunks=_ffn_chunks(f), sub_rows=ROW_TILE),
        grid=(t // tm,),
        in_specs=[
            pl.BlockSpec((tm, d), row),
            _const_spec((1, d)),
            _const_spec((d, 2 * f)),
            _const_spec((f, d)),
            _const_spec((1, d)),
        ],
        out_specs=pl.BlockSpec((tm, d), row),
        out_shape=jax.ShapeDtypeStruct((t, d), F32),
        compiler_params=_params(("parallel",)),
        name="swiglu",
    )(h, pre, w_gate_up, w_down, post)


def _attn_kernel(sink_ref, hp_ref, h_ref, ropep_ref, rope_ref, qn_ref, wq_ref, kvn_ref, wkv_ref,
                 wo_ref, post_ref, wgu_src, wd_src, o_ref, wgu_dst, wd_dst,
                 q_ref, k_ref, vt_ref, att_t_ref, wq_bf_ref, wkv_bf_ref, wo_bf_ref,
                 *, tiles_per_seq):
    _cast_ffn_weights(wgu_src, wd_src, wgu_dst, wd_dst)
    tq = q_ref.shape[0]
    kvd = vt_ref.shape[0]
    n_heads = q_ref.shape[1] // HEAD_DIM
    group = n_heads // N_KV_HEADS
    assert group * HEAD_DIM == 2 * LANES, "one unit = the two q lane chunks of a kv head"
    first_tile = (pl.program_id(0) % tiles_per_seq) == 0

    @pl.when(pl.program_id(0) == 0)
    def _():
        wq_bf_ref[...] = wq_ref[...].astype(BF16)
        wkv_bf_ref[...] = wkv_ref[...].astype(BF16)
        wo_bf_ref[...] = wo_ref[...].astype(BF16)

    lo_half = lax.broadcasted_iota(jnp.int32, (BLOCK, LANES), 1) < HEAD_DIM

    def unit_rms(x):
        return x * lax.rsqrt(jnp.mean(x * x, axis=-1, keepdims=True) + EPS)

    def project_kv(rows_unit, rows_rope, r0):
        n = rows_unit.shape[0]
        kv = _dot((rows_unit * kvn_ref[...]).astype(BF16), wkv_bf_ref[...])
        k = _rope(kv[:, :kvd], rows_rope)
        lo = lax.broadcasted_iota(jnp.int32, (n, LANES), 1) < HEAD_DIM
        for c in range(kvd // LANES):
            pair = k[:, c * LANES:(c + 1) * LANES]
            swapped = pltpu.roll(pair, HEAD_DIM, axis=1)
            k_ref[r0:r0 + n, (2 * c) * LANES:(2 * c + 1) * LANES] = jnp.where(lo, pair, swapped)
            k_ref[r0:r0 + n, (2 * c + 1) * LANES:(2 * c + 2) * LANES] = jnp.where(lo, swapped, pair)
        vt_ref[:, r0:r0 + n] = kv[:, kvd:].T.astype(BF16)

    q_gain = qn_ref[...] * (LOG2E / math.sqrt(HEAD_DIM))

    def project(piece):
        rows = slice(piece * ATTN_PIECE, (piece + 1) * ATTN_PIECE)
        if piece == 0:
            project_kv(unit_rms(hp_ref[...]), ropep_ref[...], 0)
        h_unit = unit_rms(h_ref[rows, :])
        project_kv(h_unit, rope_ref[rows, :], BLOCK + piece * ATTN_PIECE)
        q = _dot((h_unit * q_gain).astype(BF16), wq_bf_ref[...])
        q_ref[rows, :] = _rope(q, rope_ref[rows, :])

    def output(piece):
        rows = slice(piece * ATTN_PIECE, (piece + 1) * ATTN_PIECE)
        att = att_t_ref[:, rows].T
        proj = _dot(att, wo_bf_ref[...])
        o_ref[rows, :] = h_ref[rows, :] + _rms(proj, post_ref[...])

    unit_heads = (0, 2, 1, 3)
    width = group * BLOCK
    key_j = lax.broadcasted_iota(jnp.int32, (BLOCK, width), 0)
    lane = lax.broadcasted_iota(jnp.int32, (BLOCK, width), 1)
    from_prev = key_j > (lane % BLOCK)
    seg = lax.broadcasted_iota(jnp.int32, (1, width), 1) // BLOCK
    ones_rows = jnp.ones((BF16_ROWS, 2 * BLOCK), BF16)

    prev_bias = jnp.where(first_tile, NEG, 0.0).astype(F32)

    def scores(blk, g):
        r0 = blk * BLOCK
        k_win = k_ref[r0:r0 + 2 * BLOCK, g * LANES:(g + 1) * LANES]
        qc0 = q_ref[r0:r0 + BLOCK, (2 * g) * LANES:(2 * g + 1) * LANES]
        qc1 = q_ref[r0:r0 + BLOCK, (2 * g + 1) * LANES:(2 * g + 2) * LANES]
        zero = jnp.zeros_like(qc0)
        q_split = jnp.concatenate(
            [jnp.where(lo_half, qc0, zero), jnp.where(lo_half, qc1, zero),
             jnp.where(lo_half, zero, qc0), jnp.where(lo_half, zero, qc1)], axis=0)
        return _dot_nt(k_win, q_split)

    def attend(blk, g, s_t):
        r0 = blk * BLOCK
        v_win = vt_ref[g * HEAD_DIM:(g + 1) * HEAD_DIM, r0:r0 + 2 * BLOCK]
        s_prev = s_t[0:BLOCK]
        if blk == 0:
            s_prev = s_prev + prev_bias
        s_win = jnp.where(from_prev, s_prev, s_t[BLOCK:])
        m = jnp.max(s_win, axis=0, keepdims=True)
        e = jnp.exp2(s_win - m)
        p_t = jnp.concatenate([jnp.where(from_prev, e, 0.0),
                               jnp.where(from_prev, 0.0, e)], axis=0).astype(BF16)
        v_aug = jnp.concatenate([v_win, ones_rows], axis=0)
        o_aug = _dot(v_aug, p_t)
        heads = [g * group + u for u in unit_heads]
        sink = sink_ref[heads[-1]] * LOG2E
        for n in range(group - 2, -1, -1):
            sink = jnp.where(seg == n, sink_ref[heads[n]] * LOG2E, sink)
        inv = 1.0 / (o_aug[HEAD_DIM:HEAD_DIM + 1, :] + jnp.exp2(sink - m))
        o_t = (o_aug[0:HEAD_DIM, :] * inv).astype(BF16)
        for n, head in enumerate(heads):
            att_t_ref[head * HEAD_DIM:(head + 1) * HEAD_DIM, r0:r0 + BLOCK] = (
                o_t[:, n * BLOCK:(n + 1) * BLOCK])

    n_pieces = tq // ATTN_PIECE
    units = [(blk, g) for blk in range(tq // BLOCK) for g in range(N_KV_HEADS)]
    per_piece = len(units) // n_pieces
    assert SCORE_LOOKAHEAD <= per_piece
    project(0)
    if n_pieces > 1:
        project(1)
    pending = [scores(*u) for u in units[:SCORE_LOOKAHEAD]]
    for piece in range(n_pieces):
        for n in range(piece * per_piece, (piece + 1) * per_piece):
            if n + SCORE_LOOKAHEAD < len(units):
                pending.append(scores(*units[n + SCORE_LOOKAHEAD]))
            attend(*units[n], pending.pop(0))
        if piece + 2 < n_pieces:
            project(piece + 2)
        if piece >= 1:
            output(piece - 1)
    output(n_pieces - 1)


def _attention(h, q_norm, w_q, kv_norm, w_kv, rope_table, sinks, w_o, post, layer,
               ffn_w_gate_up, ffn_w_down, ffn_layer, seq_len):
    t, d = h.shape
    qd = w_q.shape[2]
    kvd = w_kv.shape[1] // 2
    tq = ATTN_TILE
    tps = seq_len // tq
    bpt = tq // BLOCK
    row = lambda i: (i, 0)
    prev_blk = lambda i: (jnp.maximum(i * bpt - 1, (i // tps) * tps * bpt), 0)
    prev_pos = lambda i: (jnp.maximum((i % tps) * bpt - 1, 0), 0)
    rope_w = rope_table.shape[1]
    n_steps = t // tq
    cast_in, cast_out, cast_shape = _ffn_weight_cast_specs(ffn_w_gate_up, ffn_w_down, ffn_layer,
                                                           n_steps)
    kern = functools.partial(_attn_kernel, tiles_per_seq=tps)
    return pl.pallas_call(
        kern,
        grid=(n_steps,),
        in_specs=[
            pl.BlockSpec(memory_space=pltpu.SMEM),
            pl.BlockSpec((BLOCK, d), prev_blk),
            pl.BlockSpec((tq, d), row),
            pl.BlockSpec((BLOCK, rope_w), prev_pos),
            pl.BlockSpec((tq, rope_w), lambda i: (i % tps, 0)),
            _const_spec((1, d)),
            _layer_spec((d, qd), layer),
            _const_spec((1, d)),
            _const_spec((d, 2 * kvd)),
            _layer_spec((qd, d), layer),
            _const_spec((1, d)),
            *cast_in,
        ],
        out_specs=[pl.BlockSpec((tq, d), row), *cast_out],
        out_shape=[jax.ShapeDtypeStruct((t, d), F32), *cast_shape],
        scratch_shapes=[
            pltpu.VMEM((tq, qd), BF16),
            pltpu.VMEM((tq + BLOCK, 2 * kvd), BF16),
            pltpu.VMEM((kvd, tq + BLOCK), BF16),
            pltpu.VMEM((qd, tq), BF16),
            pltpu.VMEM((d, qd), BF16),
            pltpu.VMEM((d, 2 * kvd), BF16),
            pltpu.VMEM((qd, d), BF16),
        ],
        compiler_params=_params(("arbitrary",)),
        name="swa_attention",
    )(sinks, h, h, rope_table, rope_table, q_norm, w_q, kv_norm, w_kv, w_o, post,
      ffn_w_gate_up, ffn_w_down)


def kernel(x, a_pre_norm, a_w_in, a_conv_w, a_w_out, a_post_norm, ffn_pre_norm, ffn_w_gate_up,
           ffn_w_down, ffn_post_norm, kv_norm, w_kv, b_pre_norm, b_w_q, b_sinks, b_w_o,
           b_post_norm):
    bsz, s, d = x.shape
    n_a = a_w_in.shape[0]
    depth = ffn_w_gate_up.shape[0]
    rope_table = _rope_tables(s)
    vec = lambda g: g.reshape(1, -1)

    h = x.reshape(bsz * s, d)
    for l in range(depth):
        if l < n_a:
            h, w_gu, w_dn = _conv_mixer(h, vec(a_pre_norm[l]), a_w_in, a_conv_w[l], a_w_out,
                                        vec(a_post_norm[l]), ffn_w_gate_up, ffn_w_down, l, s)
        else:
            assert l == n_a, "K/V are projected inside the first attention layer's call only"
            j = l - n_a
            h, w_gu, w_dn = _attention(h, vec(b_pre_norm[j]), b_w_q, vec(kv_norm), w_kv, rope_table,
                                       b_sinks[j], b_w_o, vec(b_post_norm[j]), j,
                                       ffn_w_gate_up, ffn_w_down, l, s)
        h = _ffn(h, vec(ffn_pre_norm[l]), w_gu, w_dn, vec(ffn_post_norm[l]))
    return h.reshape(bsz, s, d)
```

```python
import functools
import math

import jax
import jax.numpy as jnp
import numpy as np
from jax import lax
from jax.experimental import pallas as pl
from jax.experimental.pallas import tpu as pltpu

CONV_W = 3
HEAD_DIM = 64
N_KV_HEADS = 4
WINDOW = 128
BLOCK = 128
ROT_DIM = HEAD_DIM // 4
ROPE_THETA = 500000.0
EPS = 1e-6
NEG = -1e30
LOG2E = math.log2(math.e)

LANES = 128
SUBLANES = 8
BF16_ROWS = 16
ROW_TILE = 512
FFN_TILE = 1024
ATTN_TILE = 1024
ATTN_PIECE = 256
SCORE_LOOKAHEAD = 2
VMEM_LIMIT = 60 * 1024 * 1024

F32 = jnp.float32
BF16 = jnp.bfloat16

assert WINDOW == BLOCK and LANES == 2 * HEAD_DIM


def _rms(x, g):
    r = lax.rsqrt(jnp.mean(x * x, axis=-1, keepdims=True) + EPS)
    return (x * r) * g


def _dot(a, b):
    return jnp.dot(a, b, preferred_element_type=F32)


def _dot_nt(a, b):
    return lax.dot_general(a, b, (((1,), (1,)), ((), ())), preferred_element_type=F32)


def _const_spec(shape):
    return pl.BlockSpec(shape, lambda *_: (0,) * len(shape), pipeline_mode=pl.Buffered(1))


def _layer_spec(shape, layer):
    return pl.BlockSpec((None,) + tuple(shape), lambda *_: (layer,) + (0,) * len(shape),
                        pipeline_mode=pl.Buffered(1))


def _params(semantics):
    return pltpu.CompilerParams(dimension_semantics=semantics, vmem_limit_bytes=VMEM_LIMIT)


def _rope(t, table):
    tb = table.astype(BF16)
    cos_t, sin_up, sin_dn = (tb[:, j * LANES:(j + 1) * LANES] for j in range(3))
    outs = []
    for j in range(t.shape[1] // LANES):
        c = t[:, j * LANES:(j + 1) * LANES].astype(BF16)
        outs.append(c * cos_t
                    + pltpu.roll(c, ROT_DIM // 2, axis=1) * sin_up
                    + pltpu.roll(c, LANES - ROT_DIM // 2, axis=1) * sin_dn)
    return jnp.concatenate(outs, axis=1)


def _rope_tables(seq_len):
    half = ROT_DIM // 2
    f32 = np.float32
    inv_freq = f32(ROPE_THETA) ** (-np.arange(0, ROT_DIM, 2, dtype=f32) / f32(ROT_DIM))
    ang = np.arange(seq_len, dtype=f32)[:, None] * inv_freq[None, :]
    cos, sin = np.cos(ang), np.sin(ang)
    ones = np.ones((seq_len, HEAD_DIM - ROT_DIM), f32)
    zeros = np.zeros((seq_len, HEAD_DIM - ROT_DIM), f32)
    zh = np.zeros((seq_len, half), f32)
    reps = LANES // HEAD_DIM
    cos_t = np.tile(np.concatenate([cos, cos, ones], axis=1), (1, reps))
    sin_up = np.tile(np.concatenate([zh, sin, zeros], axis=1), (1, reps))
    sin_dn = np.tile(np.concatenate([-sin, zh, zeros], axis=1), (1, reps))
    return jnp.asarray(np.concatenate([cos_t, sin_up, sin_dn], axis=1), F32)


def _ffn_weight_cast_specs(w_gate_up, w_down, layer, n_steps):
    _, d, f2 = w_gate_up.shape
    _, f, _ = w_down.shape
    assert d % (n_steps * BF16_ROWS) == 0 and f % (n_steps * BF16_ROWS) == 0
    slab = lambda i: (layer, i, 0)
    in_specs = [pl.BlockSpec((None, d // n_steps, f2), slab),
                pl.BlockSpec((None, f // n_steps, d), slab)]
    out_specs = [pl.BlockSpec((d // n_steps, f2), lambda i: (i, 0)),
                 pl.BlockSpec((f // n_steps, d), lambda i: (i, 0))]
    out_shape = [jax.ShapeDtypeStruct((d, f2), BF16), jax.ShapeDtypeStruct((f, d), BF16)]
    return in_specs, out_specs, out_shape


def _cast_ffn_weights(wgu_src, wd_src, wgu_dst, wd_dst):
    wgu_dst[...] = wgu_src[...].astype(BF16)
    wd_dst[...] = wd_src[...].astype(BF16)


def _conv_mixer_kernel(h_ref, pre_ref, win_ref, cw_ref, wout_ref, post_ref, wgu_src, wd_src,
                       o_ref, wgu_dst, wd_dst, cu_ref, win_bf_ref, wout_bf_ref,
                       *, tiles_per_seq, sub_rows):
    tm, d = h_ref.shape
    i = pl.program_id(0)
    _cast_ffn_weights(wgu_src, wd_src, wgu_dst, wd_dst)

    @pl.when(i == 0)
    def _():
        win_bf_ref[...] = win_ref[...].astype(BF16)
        wout_bf_ref[...] = wout_ref[...].astype(BF16)

    @pl.when(i % tiles_per_seq == 0)
    def _():
        cu_ref[0:SUBLANES, :] = jnp.zeros((SUBLANES, d), F32)

    @pl.when(i % tiles_per_seq != 0)
    def _():
        cu_ref[0:SUBLANES, :] = cu_ref[tm:tm + SUBLANES, :]

    cw = cw_ref[...]

    def gated_conv(r0):
        hn = _rms(h_ref[r0:r0 + sub_rows, :], pre_ref[...]).astype(BF16)
        cu = _dot(hn, win_bf_ref[:, d:2 * d]) * _dot(hn, win_bf_ref[:, 2 * d:3 * d])
        base = SUBLANES + r0
        cu_ref[base:base + sub_rows, :] = cu
        conv = (cu_ref[base - 2:base - 2 + sub_rows, :] * cw[0:1, :]
                + cu_ref[base - 1:base - 1 + sub_rows, :] * cw[1:2, :]
                + cu * cw[2:3, :])
        return (_dot(hn, win_bf_ref[:, 0:d]) * conv).astype(BF16)

    starts = range(0, tm, sub_rows)
    ys = [gated_conv(r0) for r0 in starts]
    for r0, y in zip(starts, ys):
        mix = _dot(y, wout_bf_ref[...])
        o_ref[r0:r0 + sub_rows, :] = h_ref[r0:r0 + sub_rows, :] + _rms(mix, post_ref[...])


def _conv_mixer(h, pre, w_in, conv_w, w_out, post, ffn_w_gate_up, ffn_w_down, layer, seq_len):
    t, d = h.shape
    tm = 2 * ROW_TILE
    n_steps = t // tm
    cast_in, cast_out, cast_shape = _ffn_weight_cast_specs(ffn_w_gate_up, ffn_w_down, layer, n_steps)
    kern = functools.partial(_conv_mixer_kernel, tiles_per_seq=seq_len // tm, sub_rows=ROW_TILE)
    return pl.pallas_call(
        kern,
        grid=(n_steps,),
        in_specs=[
            pl.BlockSpec((tm, d), lambda i: (i, 0)),
            _const_spec((1, d)),
            _layer_spec((d, 3 * d), layer),
            _const_spec((CONV_W, d)),
            _layer_spec((d, d), layer),
            _const_spec((1, d)),
            *cast_in,
        ],
        out_specs=[pl.BlockSpec((tm, d), lambda i: (i, 0)), *cast_out],
        out_shape=[jax.ShapeDtypeStruct((t, d), F32), *cast_shape],
        scratch_shapes=[pltpu.VMEM((tm + SUBLANES, d), F32),
                        pltpu.VMEM((d, 3 * d), BF16), pltpu.VMEM((d, d), BF16)],
        compiler_params=_params(("arbitrary",)),
        name="conv_mixer",
    )(h, pre, w_in, conv_w, w_out, post, ffn_w_gate_up, ffn_w_down)


def _ffn_kernel(h_ref, pre_ref, wgu_ref, wd_ref, post_ref, o_ref, *, chunks, sub_rows):
    f = wd_ref.shape[0]
    for r0 in range(0, h_ref.shape[0], sub_rows):
        h = h_ref[r0:r0 + sub_rows, :]
        hn = _rms(h, pre_ref[...]).astype(BF16)
        acc = None
        for lo, hi in chunks:
            g = _dot(hn, wgu_ref[:, lo:hi])
            u = _dot(hn, wgu_ref[:, f + lo:f + hi])
            a = (g * jax.nn.sigmoid(g) * u).astype(BF16)
            part = _dot(a, wd_ref[lo:hi, :])
            acc = part if acc is None else acc + part
        o_ref[r0:r0 + sub_rows, :] = h + _rms(acc, post_ref[...])


def _ffn_chunks(f):
    step = 768
    return tuple((lo, min(lo + step, f)) for lo in range(0, f, step))


def _ffn(h, pre, w_gate_up, w_down, post):
    t, d = h.shape
    f = w_down.shape[0]
    tm = FFN_TILE
    row = lambda i: (i, 0)
    return pl.pallas_call(
        functools.partial(_ffn_kernel, chunks=_ffn_chunks(f), sub_rows=ROW_TILE),
        grid=(t // tm,),
        in_specs=[
            pl.BlockSpec((tm, d), row),
            _const_spec((1, d)),
            _const_spec((d, 2 * f)),
            _const_spec((f, d)),
            _const_spec((1, d)),
        ],
        out_specs=pl.BlockSpec((tm, d), row),
        out_shape=jax.ShapeDtypeStruct((t, d), F32),
        compiler_params=_params(("parallel",)),
        name="swiglu",
    )(h, pre, w_gate_up, w_down, post)


def _attn_kernel(sink_ref, hp_ref, h_ref, ropep_ref, rope_ref, qn_ref, wq_ref, kvn_ref, wkv_ref,
                 wo_ref, post_ref, wgu_src, wd_src, o_ref, wgu_dst, wd_dst,
                 q_ref, k_ref, vt_ref, att_t_ref, wq_bf_ref, wkv_bf_ref, wo_bf_ref,
                 *, tiles_per_seq):
    _cast_ffn_weights(wgu_src, wd_src, wgu_dst, wd_dst)
    tq = q_ref.shape[0]
    kvd = vt_ref.shape[0]
    n_heads = q_ref.shape[1] // HEAD_DIM
    group = n_heads // N_KV_HEADS
    assert group * HEAD_DIM == 2 * LANES, "one unit = the two q lane chunks of a kv head"
    first_tile = (pl.program_id(0) % tiles_per_seq) == 0

    @pl.when(pl.program_id(0) == 0)
    def _():
        wq_bf_ref[...] = wq_ref[...].astype(BF16)
        wkv_bf_ref[...] = wkv_ref[...].astype(BF16)
        wo_bf_ref[...] = wo_ref[...].astype(BF16)

    lo_half = lax.broadcasted_iota(jnp.int32, (BLOCK, LANES), 1) < HEAD_DIM

    def unit_rms(x):
        return x * lax.rsqrt(jnp.mean(x * x, axis=-1, keepdims=True) + EPS)

    def project_kv(rows_unit, rows_rope, r0):
        n = rows_unit.shape[0]
        kv = _dot((rows_unit * kvn_ref[...]).astype(BF16), wkv_bf_ref[...])
        k = _rope(kv[:, :kvd], rows_rope)
        lo = lax.broadcasted_iota(jnp.int32, (n, LANES), 1) < HEAD_DIM
        for c in range(kvd // LANES):
            pair = k[:, c * LANES:(c + 1) * LANES]
            swapped = pltpu.roll(pair, HEAD_DIM, axis=1)
            k_ref[r0:r0 + n, (2 * c) * LANES:(2 * c + 1) * LANES] = jnp.where(lo, pair, swapped)
            k_ref[r0:r0 + n, (2 * c + 1) * LANES:(2 * c + 2) * LANES] = jnp.where(lo, swapped, pair)
        vt_ref[:, r0:r0 + n] = kv[:, kvd:].T.astype(BF16)

    q_gain = qn_ref[...] * (LOG2E / math.sqrt(HEAD_DIM))

    def project(piece):
        rows = slice(piece * ATTN_PIECE, (piece + 1) * ATTN_PIECE)
        if piece == 0:
            project_kv(unit_rms(hp_ref[...]), ropep_ref[...], 0)
        h_unit = unit_rms(h_ref[rows, :])
        project_kv(h_unit, rope_ref[rows, :], BLOCK + piece * ATTN_PIECE)
        q = _dot((h_unit * q_gain).astype(BF16), wq_bf_ref[...])
        q_ref[rows, :] = _rope(q, rope_ref[rows, :])

    def output(piece):
        rows = slice(piece * ATTN_PIECE, (piece + 1) * ATTN_PIECE)
        att = att_t_ref[:, rows].T
        proj = _dot(att, wo_bf_ref[...])
        o_ref[rows, :] = h_ref[rows, :] + _rms(proj, post_ref[...])

    unit_heads = (0, 2, 1, 3)
    width = group * BLOCK
    key_j = lax.broadcasted_iota(jnp.int32, (BLOCK, width), 0)
    lane = lax.broadcasted_iota(jnp.int32, (BLOCK, width), 1)
    from_prev = key_j > (lane % BLOCK)
    seg = lax.broadcasted_iota(jnp.int32, (1, width), 1) // BLOCK
    ones_rows = jnp.ones((BF16_ROWS, 2 * BLOCK), BF16)

    prev_bias = jnp.where(first_tile, NEG, 0.0).astype(F32)

    def scores(blk, g):
        r0 = blk * BLOCK
        k_win = k_ref[r0:r0 + 2 * BLOCK, g * LANES:(g + 1) * LANES]
        qc0 = q_ref[r0:r0 + BLOCK, (2 * g) * LANES:(2 * g + 1) * LANES]
        qc1 = q_ref[r0:r0 + BLOCK, (2 * g + 1) * LANES:(2 * g + 2) * LANES]
        zero = jnp.zeros_like(qc0)
        q_split = jnp.concatenate(
            [jnp.where(lo_half, qc0, zero), jnp.where(lo_half, qc1, zero),
             jnp.where(lo_half, zero, qc0), jnp.where(lo_half, zero, qc1)], axis=0)
        return _dot_nt(k_win, q_split)

    def attend(blk, g, s_t):
        r0 = blk * BLOCK
        v_win = vt_ref[g * HEAD_DIM:(g + 1) * HEAD_DIM, r0:r0 + 2 * BLOCK]
        s_prev = s_t[0:BLOCK]
        if blk == 0:
            s_prev = s_prev + prev_bias
        s_win = jnp.where(from_prev, s_prev, s_t[BLOCK:])
        m = jnp.max(s_win, axis=0, keepdims=True)
        e = jnp.exp2(s_win - m)
        p_t = jnp.concatenate([jnp.where(from_prev, e, 0.0),
                               jnp.where(from_prev, 0.0, e)], axis=0).astype(BF16)
        v_aug = jnp.concatenate([v_win, ones_rows], axis=0)
        o_aug = _dot(v_aug, p_t)
        heads = [g * group + u for u in unit_heads]
        sink = sink_ref[heads[-1]] * LOG2E
        for n in range(group - 2, -1, -1):
            sink = jnp.where(seg == n, sink_ref[heads[n]] * LOG2E, sink)
        inv = 1.0 / (o_aug[HEAD_DIM:HEAD_DIM + 1, :] + jnp.exp2(sink - m))
        o_t = (o_aug[0:HEAD_DIM, :] * inv).astype(BF16)
        for n, head in enumerate(heads):
            att_t_ref[head * HEAD_DIM:(head + 1) * HEAD_DIM, r0:r0 + BLOCK] = (
                o_t[:, n * BLOCK:(n + 1) * BLOCK])

    n_pieces = tq // ATTN_PIECE
    units = [(blk, g) for blk in range(tq // BLOCK) for g in range(N_KV_HEADS)]
    per_piece = len(units) // n_pieces
    assert SCORE_LOOKAHEAD <= per_piece
    project(0)
    if n_pieces > 1:
        project(1)
    pending = [scores(*u) for u in units[:SCORE_LOOKAHEAD]]
    for piece in range(n_pieces):
        for n in range(piece * per_piece, (piece + 1) * per_piece):
            if n + SCORE_LOOKAHEAD < len(units):
                pending.append(scores(*units[n + SCORE_LOOKAHEAD]))
            attend(*units[n], pending.pop(0))
        if piece + 2 < n_pieces:
            project(piece + 2)
        if piece >= 1:
            output(piece - 1)
    output(n_pieces - 1)


def _attention(h, q_norm, w_q, kv_norm, w_kv, rope_table, sinks, w_o, post, layer,
               ffn_w_gate_up, ffn_w_down, ffn_layer, seq_len):
    t, d = h.shape
    qd = w_q.shape[2]
    kvd = w_kv.shape[1] // 2
    tq = ATTN_TILE
    tps = seq_len // tq
    bpt = tq // BLOCK
    row = lambda i: (i, 0)
    prev_blk = lambda i: (jnp.maximum(i * bpt - 1, (i // tps) * tps * bpt), 0)
    prev_pos = lambda i: (jnp.maximum((i % tps) * bpt - 1, 0), 0)
    rope_w = rope_table.shape[1]
    n_steps = t // tq
    cast_in, cast_out, cast_shape = _ffn_weight_cast_specs(ffn_w_gate_up, ffn_w_down, ffn_layer,
                                                           n_steps)
    kern = functools.partial(_attn_kernel, tiles_per_seq=tps)
    return pl.pallas_call(
        kern,
        grid=(n_steps,),
        in_specs=[
            pl.BlockSpec(memory_space=pltpu.SMEM),
            pl.BlockSpec((BLOCK, d), prev_blk),
            pl.BlockSpec((tq, d), row),
            pl.BlockSpec((BLOCK, rope_w), prev_pos),
            pl.BlockSpec((tq, rope_w), lambda i: (i % tps, 0)),
            _const_spec((1, d)),
            _layer_spec((d, qd), layer),
            _const_spec((1, d)),
            _const_spec((d, 2 * kvd)),
            _layer_spec((qd, d), layer),
            _const_spec((1, d)),
            *cast_in,
        ],
        out_specs=[pl.BlockSpec((tq, d), row), *cast_out],
        out_shape=[jax.ShapeDtypeStruct((t, d), F32), *cast_shape],
        scratch_shapes=[
            pltpu.VMEM((tq, qd), BF16),
            pltpu.VMEM((tq + BLOCK, 2 * kvd), BF16),
            pltpu.VMEM((kvd, tq + BLOCK), BF16),
            pltpu.VMEM((qd, tq), BF16),
            pltpu.VMEM((d, qd), BF16),
            pltpu.VMEM((d, 2 * kvd), BF16),
            pltpu.VMEM((qd, d), BF16),
        ],
        compiler_params=_params(("arbitrary",)),
        name="swa_attention",
    )(sinks, h, h, rope_table, rope_table, q_norm, w_q, kv_norm, w_kv, w_o, post,
      ffn_w_gate_up, ffn_w_down)


def kernel(x, a_pre_norm, a_w_in, a_conv_w, a_w_out, a_post_norm, ffn_pre_norm, ffn_w_gate_up,
           ffn_w_down, ffn_post_norm, kv_norm, w_kv, b_pre_norm, b_w_q, b_sinks, b_w_o,
           b_post_norm):
    bsz, s, d = x.shape
    n_a = a_w_in.shape[0]
    depth = ffn_w_gate_up.shape[0]
    rope_table = _rope_tables(s)
    vec = lambda g: g.reshape(1, -1)

    h = x.reshape(bsz * s, d)
    for l in range(depth):
        if l < n_a:
            h, w_gu, w_dn = _conv_mixer(h, vec(a_pre_norm[l]), a_w_in, a_conv_w[l], a_w_out,
                                        vec(a_post_norm[l]), ffn_w_gate_up, ffn_w_down, l, s)
        else:
            assert l == n_a, "K/V are projected inside the first attention layer's call only"
            j = l - n_a
            h, w_gu, w_dn = _attention(h, vec(b_pre_norm[j]), b_w_q, vec(kv_norm), w_kv, rope_table,
                                       b_sinks[j], b_w_o, vec(b_post_norm[j]), j,
                                       ffn_w_gate_up, ffn_w_down, l, s)
        h = _ffn(h, vec(ffn_pre_norm[l]), w_gu, w_dn, vec(ffn_post_norm[l]))
    return h.reshape(bsz, s, d)
```

```python
import functools
import math

import jax
import jax.numpy as jnp
import numpy as np
from jax import lax
from jax.experimental import pallas as pl
from jax.experimental.pallas import tpu as pltpu

CONV_W = 3
HEAD_DIM = 64
N_KV_HEADS = 4
WINDOW = 128
BLOCK = 128
ROT_DIM = HEAD_DIM // 4
ROPE_THETA = 500000.0
EPS = 1e-6
NEG = -1e30
LOG2E = math.log2(math.e)

LANES = 128
SUBLANES = 8
BF16_ROWS = 16
ROW_TILE = 512
FFN_TILE = 2048
ATTN_TILE = 1024
ATTN_PIECE = 256
SCORE_LOOKAHEAD = 2
VMEM_LIMIT = 60 * 1024 * 1024

F32 = jnp.float32
BF16 = jnp.bfloat16

assert WINDOW == BLOCK and LANES == 2 * HEAD_DIM


def _rms(x, g):
    r = lax.rsqrt(jnp.mean(x * x, axis=-1, keepdims=True) + EPS)
    return (x * r) * g


def _dot(a, b):
    return jnp.dot(a, b, preferred_element_type=F32)


def _dot_nt(a, b):
    return lax.dot_general(a, b, (((1,), (1,)), ((), ())), preferred_element_type=F32)


def _const_spec(shape):
    return pl.BlockSpec(shape, lambda *_: (0,) * len(shape), pipeline_mode=pl.Buffered(1))


def _layer_spec(shape, layer):
    return pl.BlockSpec((None,) + tuple(shape), lambda *_: (layer,) + (0,) * len(shape),
                        pipeline_mode=pl.Buffered(1))


def _params(semantics):
    return pltpu.CompilerParams(dimension_semantics=semantics, vmem_limit_bytes=VMEM_LIMIT)


def _rope(t, table):
    tb = table.astype(BF16)
    cos_t, sin_up, sin_dn = (tb[:, j * LANES:(j + 1) * LANES] for j in range(3))
    outs = []
    for j in range(t.shape[1] // LANES):
        c = t[:, j * LANES:(j + 1) * LANES].astype(BF16)
        outs.append(c * cos_t
                    + pltpu.roll(c, ROT_DIM // 2, axis=1) * sin_up
                    + pltpu.roll(c, LANES - ROT_DIM // 2, axis=1) * sin_dn)
    return jnp.concatenate(outs, axis=1)


def _rope_tables(seq_len):
    half = ROT_DIM // 2
    f32 = np.float32
    inv_freq = f32(ROPE_THETA) ** (-np.arange(0, ROT_DIM, 2, dtype=f32) / f32(ROT_DIM))
    ang = np.arange(seq_len, dtype=f32)[:, None] * inv_freq[None, :]
    cos, sin = np.cos(ang), np.sin(ang)
    ones = np.ones((seq_len, HEAD_DIM - ROT_DIM), f32)
    zeros = np.zeros((seq_len, HEAD_DIM - ROT_DIM), f32)
    zh = np.zeros((seq_len, half), f32)
    reps = LANES // HEAD_DIM
    cos_t = np.tile(np.concatenate([cos, cos, ones], axis=1), (1, reps))
    sin_up = np.tile(np.concatenate([zh, sin, zeros], axis=1), (1, reps))
    sin_dn = np.tile(np.concatenate([-sin, zh, zeros], axis=1), (1, reps))
    return jnp.asarray(np.concatenate([cos_t, sin_up, sin_dn], axis=1), F32)


def _ffn_weight_cast_specs(w_gate_up, w_down, layer, n_steps):
    _, d, f2 = w_gate_up.shape
    _, f, _ = w_down.shape
    assert d % (n_steps * BF16_ROWS) == 0 and f % (n_steps * BF16_ROWS) == 0
    slab = lambda i: (layer, i, 0)
    in_specs = [pl.BlockSpec((None, d // n_steps, f2), slab),
                pl.BlockSpec((None, f // n_steps, d), slab)]
    out_specs = [pl.BlockSpec((d // n_steps, f2), lambda i: (i, 0)),
                 pl.BlockSpec((f // n_steps, d), lambda i: (i, 0))]
    out_shape = [jax.ShapeDtypeStruct((d, f2), BF16), jax.ShapeDtypeStruct((f, d), BF16)]
    return in_specs, out_specs, out_shape


def _cast_ffn_weights(wgu_src, wd_src, wgu_dst, wd_dst):
    wgu_dst[...] = wgu_src[...].astype(BF16)
    wd_dst[...] = wd_src[...].astype(BF16)


def _conv_mixer_kernel(h_ref, pre_ref, win_ref, cw_ref, wout_ref, post_ref, wgu_src, wd_src,
                       o_ref, wgu_dst, wd_dst, cu_ref, win_bf_ref, wout_bf_ref,
                       *, tiles_per_seq, sub_rows):
    tm, d = h_ref.shape
    i = pl.program_id(0)
    _cast_ffn_weights(wgu_src, wd_src, wgu_dst, wd_dst)

    @pl.when(i == 0)
    def _():
        win_bf_ref[...] = win_ref[...].astype(BF16)
        wout_bf_ref[...] = wout_ref[...].astype(BF16)

    @pl.when(i % tiles_per_seq == 0)
    def _():
        cu_ref[0:SUBLANES, :] = jnp.zeros((SUBLANES, d), F32)

    @pl.when(i % tiles_per_seq != 0)
    def _():
        cu_ref[0:SUBLANES, :] = cu_ref[tm:tm + SUBLANES, :]

    cw = cw_ref[...]

    def gated_conv(r0):
        hn = _rms(h_ref[r0:r0 + sub_rows, :], pre_ref[...]).astype(BF16)
        cu = _dot(hn, win_bf_ref[:, d:2 * d]) * _dot(hn, win_bf_ref[:, 2 * d:3 * d])
        base = SUBLANES + r0
        cu_ref[base:base + sub_rows, :] = cu
        conv = (cu_ref[base - 2:base - 2 + sub_rows, :] * cw[0:1, :]
                + cu_ref[base - 1:base - 1 + sub_rows, :] * cw[1:2, :]
                + cu * cw[2:3, :])
        return (_dot(hn, win_bf_ref[:, 0:d]) * conv).astype(BF16)

    starts = range(0, tm, sub_rows)
    ys = [gated_conv(r0) for r0 in starts]
    for r0, y in zip(starts, ys):
        mix = _dot(y, wout_bf_ref[...])
        o_ref[r0:r0 + sub_rows, :] = h_ref[r0:r0 + sub_rows, :] + _rms(mix, post_ref[...])


def _conv_mixer(h, pre, w_in, conv_w, w_out, post, ffn_w_gate_up, ffn_w_down, layer, seq_len):
    t, d = h.shape
    tm = 2 * ROW_TILE
    n_steps = t // tm
    cast_in, cast_out, cast_shape = _ffn_weight_cast_specs(ffn_w_gate_up, ffn_w_down, layer, n_steps)
    kern = functools.partial(_conv_mixer_kernel, tiles_per_seq=seq_len // tm, sub_rows=ROW_TILE)
    return pl.pallas_call(
        kern,
        grid=(n_steps,),
        in_specs=[
            pl.BlockSpec((tm, d), lambda i: (i, 0)),
            _const_spec((1, d)),
            _layer_spec((d, 3 * d), layer),
            _const_spec((CONV_W, d)),
            _layer_spec((d, d), layer),
            _const_spec((1, d)),
            *cast_in,
        ],
        out_specs=[pl.BlockSpec((tm, d), lambda i: (i, 0)), *cast_out],
        out_shape=[jax.ShapeDtypeStruct((t, d), F32), *cast_shape],
        scratch_shapes=[pltpu.VMEM((tm + SUBLANES, d), F32),
                        pltpu.VMEM((d, 3 * d), BF16), pltpu.VMEM((d, d), BF16)],
        compiler_params=_params(("arbitrary",)),
        name="conv_mixer",
    )(h, pre, w_in, conv_w, w_out, post, ffn_w_gate_up, ffn_w_down)


def _ffn_kernel(h_ref, pre_ref, wgu_ref, wd_ref, post_ref, o_ref, *, chunks, sub_rows):
    f = wd_ref.shape[0]
    for r0 in range(0, h_ref.shape[0], sub_rows):
        h = h_ref[r0:r0 + sub_rows, :]
        hn = _rms(h, pre_ref[...]).astype(BF16)
        acc = None
        for lo, hi in chunks:
            g = _dot(hn, wgu_ref[:, lo:hi])
            u = _dot(hn, wgu_ref[:, f + lo:f + hi])
            a = (g * jax.nn.sigmoid(g) * u).astype(BF16)
            part = _dot(a, wd_ref[lo:hi, :])
            acc = part if acc is None else acc + part
        o_ref[r0:r0 + sub_rows, :] = h + _rms(acc, post_ref[...])


def _ffn_chunks(f):
    step = 768
    return tuple((lo, min(lo + step, f)) for lo in range(0, f, step))


def _ffn(h, pre, w_gate_up, w_down, post):
    t, d = h.shape
    f = w_down.shape[0]
    tm = FFN_TILE
    row = lambda i: (i, 0)
    return pl.pallas_call(
        functools.partial(_ffn_kernel, chunks=_ffn_chunks(f), sub_rows=ROW_TILE),
        grid=(t // tm,),
        in_specs=[
            pl.BlockSpec((tm, d), row),
            _const_spec((1, d)),
            _const_spec((d, 2 * f)),
            _const_spec((f, d)),
            _const_spec((1, d)),
        ],
        out_specs=pl.BlockSpec((tm, d), row),
        out_shape=jax.ShapeDtypeStruct((t, d), F32),
        compiler_params=_params(("parallel",)),
        name="swiglu",
    )(h, pre, w_gate_up, w_down, post)


def _attn_kernel(sink_ref, hp_ref, h_ref, ropep_ref, rope_ref, qn_ref, wq_ref, kvn_ref, wkv_ref,
                 wo_ref, post_ref, wgu_src, wd_src, o_ref, wgu_dst, wd_dst,
                 q_ref, k_ref, vt_ref, att_t_ref, wq_bf_ref, wkv_bf_ref, wo_bf_ref,
                 *, tiles_per_seq):
    _cast_ffn_weights(wgu_src, wd_src, wgu_dst, wd_dst)
    tq = q_ref.shape[0]
    kvd = vt_ref.shape[0]
    n_heads = q_ref.shape[1] // HEAD_DIM
    group = n_heads // N_KV_HEADS
    assert group * HEAD_DIM == 2 * LANES, "one unit = the two q lane chunks of a kv head"
    first_tile = (pl.program_id(0) % tiles_per_seq) == 0

    @pl.when(pl.program_id(0) == 0)
    def _():
        wq_bf_ref[...] = wq_ref[...].astype(BF16)
        wkv_bf_ref[...] = wkv_ref[...].astype(BF16)
        wo_bf_ref[...] = wo_ref[...].astype(BF16)

    lo_half = lax.broadcasted_iota(jnp.int32, (BLOCK, LANES), 1) < HEAD_DIM

    def unit_rms(x):
        return x * lax.rsqrt(jnp.mean(x * x, axis=-1, keepdims=True) + EPS)

    def project_kv(rows_unit, rows_rope, r0):
        n = rows_unit.shape[0]
        kv = _dot((rows_unit * kvn_ref[...]).astype(BF16), wkv_bf_ref[...])
        k = _rope(kv[:, :kvd], rows_rope)
        lo = lax.broadcasted_iota(jnp.int32, (n, LANES), 1) < HEAD_DIM
        for c in range(kvd // LANES):
            pair = k[:, c * LANES:(c + 1) * LANES]
            swapped = pltpu.roll(pair, HEAD_DIM, axis=1)
            k_ref[r0:r0 + n, (2 * c) * LANES:(2 * c + 1) * LANES] = jnp.where(lo, pair, swapped)
            k_ref[r0:r0 + n, (2 * c + 1) * LANES:(2 * c + 2) * LANES] = jnp.where(lo, swapped, pair)
        vt_ref[:, r0:r0 + n] = kv[:, kvd:].T.astype(BF16)

    q_gain = qn_ref[...] * (LOG2E / math.sqrt(HEAD_DIM))

    def project(piece):
        rows = slice(piece * ATTN_PIECE, (piece + 1) * ATTN_PIECE)
        if piece == 0:
            project_kv(unit_rms(hp_ref[...]), ropep_ref[...], 0)
        h_unit = unit_rms(h_ref[rows, :])
        project_kv(h_unit, rope_ref[rows, :], BLOCK + piece * ATTN_PIECE)
        q = _dot((h_unit * q_gain).astype(BF16), wq_bf_ref[...])
        q_ref[rows, :] = _rope(q, rope_ref[rows, :])

    def output(piece):
        rows = slice(piece * ATTN_PIECE, (piece + 1) * ATTN_PIECE)
        att = att_t_ref[:, rows].T
        proj = _dot(att, wo_bf_ref[...])
        o_ref[rows, :] = h_ref[rows, :] + _rms(proj, post_ref[...])

    unit_heads = (0, 2, 1, 3)
    width = group * BLOCK
    key_j = lax.broadcasted_iota(jnp.int32, (BLOCK, width), 0)
    lane = lax.broadcasted_iota(jnp.int32, (BLOCK, width), 1)
    from_prev = key_j > (lane % BLOCK)
    seg = lax.broadcasted_iota(jnp.int32, (1, width), 1) // BLOCK
    ones_rows = jnp.ones((BF16_ROWS, 2 * BLOCK), BF16)

    prev_bias = jnp.where(first_tile, NEG, 0.0).astype(F32)

    def scores(blk, g):
        r0 = blk * BLOCK
        k_win = k_ref[r0:r0 + 2 * BLOCK, g * LANES:(g + 1) * LANES]
        qc0 = q_ref[r0:r0 + BLOCK, (2 * g) * LANES:(2 * g + 1) * LANES]
        qc1 = q_ref[r0:r0 + BLOCK, (2 * g + 1) * LANES:(2 * g + 2) * LANES]
        zero = jnp.zeros_like(qc0)
        q_split = jnp.concatenate(
            [jnp.where(lo_half, qc0, zero), jnp.where(lo_half, qc1, zero),
             jnp.where(lo_half, zero, qc0), jnp.where(lo_half, zero, qc1)], axis=0)
        return _dot_nt(k_win, q_split)

    def attend(blk, g, s_t):
        r0 = blk * BLOCK
        v_win = vt_ref[g * HEAD_DIM:(g + 1) * HEAD_DIM, r0:r0 + 2 * BLOCK]
        s_prev = s_t[0:BLOCK]
        if blk == 0:
            s_prev = s_prev + prev_bias
        s_win = jnp.where(from_prev, s_prev, s_t[BLOCK:])
        m = jnp.max(s_win, axis=0, keepdims=True)
        e = jnp.exp2(s_win - m)
        p_t = jnp.concatenate([jnp.where(from_prev, e, 0.0),
                               jnp.where(from_prev, 0.0, e)], axis=0).astype(BF16)
        v_aug = jnp.concatenate([v_win, ones_rows], axis=0)
        o_aug = _dot(v_aug, p_t)
        heads = [g * group + u for u in unit_heads]
        sink = sink_ref[heads[-1]] * LOG2E
        for n in range(group - 2, -1, -1):
            sink = jnp.where(seg == n, sink_ref[heads[n]] * LOG2E, sink)
        inv = 1.0 / (o_aug[HEAD_DIM:HEAD_DIM + 1, :] + jnp.exp2(sink - m))
        o_t = (o_aug[0:HEAD_DIM, :] * inv).astype(BF16)
        for n, head in enumerate(heads):
            att_t_ref[head * HEAD_DIM:(head + 1) * HEAD_DIM, r0:r0 + BLOCK] = (
                o_t[:, n * BLOCK:(n + 1) * BLOCK])

    n_pieces = tq // ATTN_PIECE
    units = [(blk, g) for blk in range(tq // BLOCK) for g in range(N_KV_HEADS)]
    per_piece = len(units) // n_pieces
    assert SCORE_LOOKAHEAD <= per_piece
    project(0)
    if n_pieces > 1:
        project(1)
    pending = [scores(*u) for u in units[:SCORE_LOOKAHEAD]]
    for piece in range(n_pieces):
        for n in range(piece * per_piece, (piece + 1) * per_piece):
            if n + SCORE_LOOKAHEAD < len(units):
                pending.append(scores(*units[n + SCORE_LOOKAHEAD]))
            attend(*units[n], pending.pop(0))
        if piece + 2 < n_pieces:
            project(piece + 2)
        if piece >= 1:
            output(piece - 1)
    output(n_pieces - 1)


def _attention(h, q_norm, w_q, kv_norm, w_kv, rope_table, sinks, w_o, post, layer,
               ffn_w_gate_up, ffn_w_down, ffn_layer, seq_len):
    t, d = h.shape
    qd = w_q.shape[2]
    kvd = w_kv.shape[1] // 2
    tq = ATTN_TILE
    tps = seq_len // tq
    bpt = tq // BLOCK
    row = lambda i: (i, 0)
    prev_blk = lambda i: (jnp.maximum(i * bpt - 1, (i // tps) * tps * bpt), 0)
    prev_pos = lambda i: (jnp.maximum((i % tps) * bpt - 1, 0), 0)
    rope_w = rope_table.shape[1]
    n_steps = t // tq
    cast_in, cast_out, cast_shape = _ffn_weight_cast_specs(ffn_w_gate_up, ffn_w_down, ffn_layer,
                                                           n_steps)
    kern = functools.partial(_attn_kernel, tiles_per_seq=tps)
    return pl.pallas_call(
        kern,
        grid=(n_steps,),
        in_specs=[
            pl.BlockSpec(memory_space=pltpu.SMEM),
            pl.BlockSpec((BLOCK, d), prev_blk),
            pl.BlockSpec((tq, d), row),
            pl.BlockSpec((BLOCK, rope_w), prev_pos),
            pl.BlockSpec((tq, rope_w), lambda i: (i % tps, 0)),
            _const_spec((1, d)),
            _layer_spec((d, qd), layer),
            _const_spec((1, d)),
            _const_spec((d, 2 * kvd)),
            _layer_spec((qd, d), layer),
            _const_spec((1, d)),
            *cast_in,
        ],
        out_specs=[pl.BlockSpec((tq, d), row), *cast_out],
        out_shape=[jax.ShapeDtypeStruct((t, d), F32), *cast_shape],
        scratch_shapes=[
            pltpu.VMEM((tq, qd), BF16),
            pltpu.VMEM((tq + BLOCK, 2 * kvd), BF16),
            pltpu.VMEM((kvd, tq + BLOCK), BF16),
            pltpu.VMEM((qd, tq), BF16),
            pltpu.VMEM((d, qd), BF16),
            pltpu.VMEM((d, 2 * kvd), BF16),
            pltpu.VMEM((qd, d), BF16),
        ],
        compiler_params=_params(("arbitrary",)),
        name="swa_attention",
    )(sinks, h, h, rope_table, rope_table, q_norm, w_q, kv_norm, w_kv, w_o, post,
      ffn_w_gate_up, ffn_w_down)


def kernel(x, a_pre_norm, a_w_in, a_conv_w, a_w_out, a_post_norm, ffn_pre_norm, ffn_w_gate_up,
           ffn_w_down, ffn_post_norm, kv_norm, w_kv, b_pre_norm, b_w_q, b_sinks, b_w_o,
           b_post_norm):
    bsz, s, d = x.shape
    n_a = a_w_in.shape[0]
    depth = ffn_w_gate_up.shape[0]
    rope_table = _rope_tables(s)
    vec = lambda g: g.reshape(1, -1)

    h = x.reshape(bsz * s, d)
    for l in range(depth):
        if l < n_a:
            h, w_gu, w_dn = _conv_mixer(h, vec(a_pre_norm[l]), a_w_in, a_conv_w[l], a_w_out,
                                        vec(a_post_norm[l]), ffn_w_gate_up, ffn_w_down, l, s)
        else:
            assert l == n_a, "K/V are projected inside the first attention layer's call only"
            j = l - n_a
            h, w_gu, w_dn = _attention(h, vec(b_pre_norm[j]), b_w_q, vec(kv_norm), w_kv, rope_table,
                                       b_sinks[j], b_w_o, vec(b_post_norm[j]), j,
                                       ffn_w_gate_up, ffn_w_down, l, s)
        h = _ffn(h, vec(ffn_pre_norm[l]), w_gu, w_dn, vec(ffn_post_norm[l]))
    return h.reshape(bsz, s, d)
```

```python
import functools
import math

import jax
import jax.numpy as jnp
import numpy as np
from jax import lax
from jax.experimental import pallas as pl
from jax.experimental.pallas import tpu as pltpu

CONV_W = 3
HEAD_DIM = 64
N_KV_HEADS = 4
WINDOW = 128
BLOCK = 128
ROT_DIM = HEAD_DIM // 4
ROPE_THETA = 500000.0
EPS = 1e-6
NEG = -1e30
LOG2E = math.log2(math.e)

LANES = 128
SUBLANES = 8
BF16_ROWS = 16
ROW_TILE = 512
FFN_TILE = 1024
ATTN_TILE = 1024
ATTN_PIECE = 256
SCORE_LOOKAHEAD = 2
VMEM_LIMIT = 60 * 1024 * 1024

F32 = jnp.float32
BF16 = jnp.bfloat16

assert WINDOW == BLOCK and LANES == 2 * HEAD_DIM


def _rms(x, g):
    r = lax.rsqrt(jnp.mean(x * x, axis=-1, keepdims=True) + EPS)
    return (x * r) * g


def _dot(a, b):
    return jnp.dot(a, b, preferred_element_type=F32)


def _dot_nt(a, b):
    return lax.dot_general(a, b, (((1,), (1,)), ((), ())), preferred_element_type=F32)


def _const_spec(shape):
    return pl.BlockSpec(shape, lambda *_: (0,) * len(shape), pipeline_mode=pl.Buffered(1))


def _layer_spec(shape, layer):
    return pl.BlockSpec((None,) + tuple(shape), lambda *_: (layer,) + (0,) * len(shape),
                        pipeline_mode=pl.Buffered(1))


def _params(semantics):
    return pltpu.CompilerParams(dimension_semantics=semantics, vmem_limit_bytes=VMEM_LIMIT)


def _rope(t, table):
    tb = table.astype(BF16)
    cos_t, sin_up, sin_dn = (tb[:, j * LANES:(j + 1) * LANES] for j in range(3))
    outs = []
    for j in range(t.shape[1] // LANES):
        c = t[:, j * LANES:(j + 1) * LANES].astype(BF16)
        outs.append(c * cos_t
                    + pltpu.roll(c, ROT_DIM // 2, axis=1) * sin_up
                    + pltpu.roll(c, LANES - ROT_DIM // 2, axis=1) * sin_dn)
    return jnp.concatenate(outs, axis=1)


def _rope_tables(seq_len):
    half = ROT_DIM // 2
    f32 = np.float32
    inv_freq = f32(ROPE_THETA) ** (-np.arange(0, ROT_DIM, 2, dtype=f32) / f32(ROT_DIM))
    ang = np.arange(seq_len, dtype=f32)[:, None] * inv_freq[None, :]
    cos, sin = np.cos(ang), np.sin(ang)
    ones = np.ones((seq_len, HEAD_DIM - ROT_DIM), f32)
    zeros = np.zeros((seq_len, HEAD_DIM - ROT_DIM), f32)
    zh = np.zeros((seq_len, half), f32)
    reps = LANES // HEAD_DIM
    cos_t = np.tile(np.concatenate([cos, cos, ones], axis=1), (1, reps))
    sin_up = np.tile(np.concatenate([zh, sin, zeros], axis=1), (1, reps))
    sin_dn = np.tile(np.concatenate([-sin, zh, zeros], axis=1), (1, reps))
    return jnp.asarray(np.concatenate([cos_t, sin_up, sin_dn], axis=1), F32)


def _ffn_weight_cast_specs(w_gate_up, w_down, layer, n_steps):
    _, d, f2 = w_gate_up.shape
    _, f, _ = w_down.shape
    assert d % (n_steps * BF16_ROWS) == 0 and f % (n_steps * BF16_ROWS) == 0
    slab = lambda i: (layer, i, 0)
    in_specs = [pl.BlockSpec((None, d // n_steps, f2), slab),
                pl.BlockSpec((None, f // n_steps, d), slab)]
    out_specs = [pl.BlockSpec((d // n_steps, f2), lambda i: (i, 0)),
                 pl.BlockSpec((f // n_steps, d), lambda i: (i, 0))]
    out_shape = [jax.ShapeDtypeStruct((d, f2), BF16), jax.ShapeDtypeStruct((f, d), BF16)]
    return in_specs, out_specs, out_shape


def _cast_ffn_weights(wgu_src, wd_src, wgu_dst, wd_dst):
    wgu_dst[...] = wgu_src[...].astype(BF16)
    wd_dst[...] = wd_src[...].astype(BF16)


def _conv_mixer_kernel(h_ref, pre_ref, win_ref, cw_ref, wout_ref, post_ref, wgu_src, wd_src,
                       o_ref, wgu_dst, wd_dst, cu_ref, win_bf_ref, wout_bf_ref,
                       *, tiles_per_seq, sub_rows):
    tm, d = h_ref.shape
    i = pl.program_id(0)
    _cast_ffn_weights(wgu_src, wd_src, wgu_dst, wd_dst)

    @pl.when(i == 0)
    def _():
        win_bf_ref[...] = win_ref[...].astype(BF16)
        wout_bf_ref[...] = wout_ref[...].astype(BF16)

    @pl.when(i % tiles_per_seq == 0)
    def _():
        cu_ref[0:SUBLANES, :] = jnp.zeros((SUBLANES, d), F32)

    @pl.when(i % tiles_per_seq != 0)
    def _():
        cu_ref[0:SUBLANES, :] = cu_ref[tm:tm + SUBLANES, :]

    cw = cw_ref[...]

    def gated_conv(r0):
        hn = _rms(h_ref[r0:r0 + sub_rows, :], pre_ref[...]).astype(BF16)
        cu = _dot(hn, win_bf_ref[:, d:2 * d]) * _dot(hn, win_bf_ref[:, 2 * d:3 * d])
        base = SUBLANES + r0
        cu_ref[base:base + sub_rows, :] = cu
        conv = (cu_ref[base - 2:base - 2 + sub_rows, :] * cw[0:1, :]
                + cu_ref[base - 1:base - 1 + sub_rows, :] * cw[1:2, :]
                + cu * cw[2:3, :])
        return (_dot(hn, win_bf_ref[:, 0:d]) * conv).astype(BF16)

    starts = range(0, tm, sub_rows)
    ys = [gated_conv(r0) for r0 in starts]
    for r0, y in zip(starts, ys):
        mix = _dot(y, wout_bf_ref[...])
        o_ref[r0:r0 + sub_rows, :] = h_ref[r0:r0 + sub_rows, :] + _rms(mix, post_ref[...])


def _conv_mixer(h, pre, w_in, conv_w, w_out, post, ffn_w_gate_up, ffn_w_down, layer, seq_len):
    t, d = h.shape
    tm = 2 * ROW_TILE
    n_steps = t // tm
    cast_in, cast_out, cast_shape = _ffn_weight_cast_specs(ffn_w_gate_up, ffn_w_down, layer, n_steps)
    kern = functools.partial(_conv_mixer_kernel, tiles_per_seq=seq_len // tm, sub_rows=ROW_TILE)
    return pl.pallas_call(
        kern,
        grid=(n_steps,),
        in_specs=[
            pl.BlockSpec((tm, d), lambda i: (i, 0)),
            _const_spec((1, d)),
            _layer_spec((d, 3 * d), layer),
            _const_spec((CONV_W, d)),
            _layer_spec((d, d), layer),
            _const_spec((1, d)),
            *cast_in,
        ],
        out_specs=[pl.BlockSpec((tm, d), lambda i: (i, 0)), *cast_out],
        out_shape=[jax.ShapeDtypeStruct((t, d), F32), *cast_shape],
        scratch_shapes=[pltpu.VMEM((tm + SUBLANES, d), F32),
                        pltpu.VMEM((d, 3 * d), BF16), pltpu.VMEM((d, d), BF16)],
        compiler_params=_params(("arbitrary",)),
        name="conv_mixer",
    )(h, pre, w_in, conv_w, w_out, post, ffn_w_gate_up, ffn_w_down)


def _ffn_kernel(h_ref, pre_ref, wgu_ref, wd_ref, post_ref, o_ref, *, chunks, sub_rows):
    f = wd_ref.shape[0]
    for r0 in range(0, h_ref.shape[0], sub_rows):
        h = h_ref[r0:r0 + sub_rows, :]
        hn = _rms(h, pre_ref[...]).astype(BF16)
        acc = None
        for lo, hi in chunks:
            g = _dot(hn, wgu_ref[:, lo:hi])
            u = _dot(hn, wgu_ref[:, f + lo:f + hi])
            a = (g * jax.nn.sigmoid(g) * u).astype(BF16)
            part = _dot(a, wd_ref[lo:hi, :])
            acc = part if acc is None else acc + part
        o_ref[r0:r0 + sub_rows, :] = h + _rms(acc, post_ref[...])


def _ffn_chunks(f):
    step = 768
    return tuple((lo, min(lo + step, f)) for lo in range(0, f, step))


def _ffn(h, pre, w_gate_up, w_down, post):
    t, d = h.shape
    f = w_down.shape[0]
    tm = FFN_TILE
    row = lambda i: (i, 0)
    return pl.pallas_call(
        functools.partial(_ffn_kernel, chunks=_ffn_chunks(f), sub_rows=ROW_TILE),
        grid=(t // tm,),
        in_specs=[
            pl.BlockSpec((tm, d), row),
            _const_spec((1, d)),
            _const_spec((d, 2 * f)),
            _const_spec((f, d)),
            _const_spec((1, d)),
        ],
        out_specs=pl.BlockSpec((tm, d), row),
        out_shape=jax.ShapeDtypeStruct((t, d), F32),
        compiler_params=_params(("parallel",)),
        name="swiglu",
    )(h, pre, w_gate_up, w_down, post)


def _attn_kernel(sink_ref, hp_ref, h_ref, ropep_ref, rope_ref, qn_ref, wq_ref, kvn_ref, wkv_ref,
                 wo_ref, post_ref, wgu_src, wd_src, o_ref, wgu_dst, wd_dst,
                 q_ref, k_ref, vt_ref, att_t_ref, wq_bf_ref, wkv_bf_ref, wo_bf_ref,
                 *, tiles_per_seq):
    _cast_ffn_weights(wgu_src, wd_src, wgu_dst, wd_dst)
    tq = q_ref.shape[0]
    kvd = vt_ref.shape[0]
    n_heads = q_ref.shape[1] // HEAD_DIM
    group = n_heads // N_KV_HEADS
    assert group * HEAD_DIM == 2 * LANES, "one unit = the two q lane chunks of a kv head"
    first_tile = (pl.program_id(0) % tiles_per_seq) == 0

    @pl.when(pl.program_id(0) == 0)
    def _():
        wq_bf_ref[...] = wq_ref[...].astype(BF16)
        wkv_bf_ref[...] = wkv_ref[...].astype(BF16)
        wo_bf_ref[...] = wo_ref[...].astype(BF16)

    lo_half = lax.broadcasted_iota(jnp.int32, (BLOCK, LANES), 1) < HEAD_DIM

    def unit_rms(x):
        return x * lax.rsqrt(jnp.mean(x * x, axis=-1, keepdims=True) + EPS)

    def project_kv(rows_unit, rows_rope, r0):
        n = rows_unit.shape[0]
        kv = _dot((rows_unit * kvn_ref[...]).astype(BF16), wkv_bf_ref[...])
        k = _rope(kv[:, :kvd], rows_rope)
        lo = lax.broadcasted_iota(jnp.int32, (n, LANES), 1) < HEAD_DIM
        for c in range(kvd // LANES):
            pair = k[:, c * LANES:(c + 1) * LANES]
            swapped = pltpu.roll(pair, HEAD_DIM, axis=1)
            k_ref[r0:r0 + n, (2 * c) * LANES:(2 * c + 1) * LANES] = jnp.where(lo, pair, swapped)
            k_ref[r0:r0 + n, (2 * c + 1) * LANES:(2 * c + 2) * LANES] = jnp.where(lo, swapped, pair)
        vt_ref[:, r0:r0 + n] = kv[:, kvd:].T.astype(BF16)

    q_gain = qn_ref[...] * (LOG2E / math.sqrt(HEAD_DIM))

    def project(piece):
        rows = slice(piece * ATTN_PIECE, (piece + 1) * ATTN_PIECE)
        if piece == 0:
            project_kv(unit_rms(hp_ref[...]), ropep_ref[...], 0)
        h_unit = unit_rms(h_ref[rows, :])
        project_kv(h_unit, rope_ref[rows, :], BLOCK + piece * ATTN_PIECE)
        q = _dot((h_unit * q_gain).astype(BF16), wq_bf_ref[...])
        q_ref[rows, :] = _rope(q, rope_ref[rows, :])

    def output(piece):
        rows = slice(piece * ATTN_PIECE, (piece + 1) * ATTN_PIECE)
        att = att_t_ref[:, rows].T
        proj = _dot(att, wo_bf_ref[...])
        o_ref[rows, :] = h_ref[rows, :] + _rms(proj, post_ref[...])

    unit_heads = (0, 2, 1, 3)
    width = group * BLOCK
    key_j = lax.broadcasted_iota(jnp.int32, (BLOCK, BLOCK), 0)
    query_i = lax.broadcasted_iota(jnp.int32, (BLOCK, BLOCK), 1)
    from_prev_blk = key_j > query_i

    def by_head(fn, *arrays):
        return jnp.concatenate(
            [fn(from_prev_blk, *(a[:, n * BLOCK:(n + 1) * BLOCK] for a in arrays))
             for n in range(group)], axis=1)
    seg = lax.broadcasted_iota(jnp.int32, (1, width), 1) // BLOCK
    ones_rows = jnp.ones((BF16_ROWS, 2 * BLOCK), BF16)

    prev_bias = jnp.where(first_tile, NEG, 0.0).astype(F32)

    def scores(blk, g):
        r0 = blk * BLOCK
        k_win = k_ref[r0:r0 + 2 * BLOCK, g * LANES:(g + 1) * LANES]
        qc0 = q_ref[r0:r0 + BLOCK, (2 * g) * LANES:(2 * g + 1) * LANES]
        qc1 = q_ref[r0:r0 + BLOCK, (2 * g + 1) * LANES:(2 * g + 2) * LANES]
        zero = jnp.zeros_like(qc0)
        q_split = jnp.concatenate(
            [jnp.where(lo_half, qc0, zero), jnp.where(lo_half, qc1, zero),
             jnp.where(lo_half, zero, qc0), jnp.where(lo_half, zero, qc1)], axis=0)
        return _dot_nt(k_win, q_split)

    def attend(blk, g, s_t):
        r0 = blk * BLOCK
        v_win = vt_ref[g * HEAD_DIM:(g + 1) * HEAD_DIM, r0:r0 + 2 * BLOCK]
        s_prev = s_t[0:BLOCK]
        if blk == 0:
            s_prev = s_prev + prev_bias
        s_win = by_head(jnp.where, s_prev, s_t[BLOCK:])
        m = jnp.max(s_win, axis=0, keepdims=True)
        e = jnp.exp2(s_win - m)
        p_t = jnp.concatenate([by_head(lambda mask, x: jnp.where(mask, x, 0.0), e),
                               by_head(lambda mask, x: jnp.where(mask, 0.0, x), e)],
                              axis=0).astype(BF16)
        v_aug = jnp.concatenate([v_win, ones_rows], axis=0)
        o_aug = _dot(v_aug, p_t)
        heads = [g * group + u for u in unit_heads]
        sink = sink_ref[heads[-1]] * LOG2E
        for n in range(group - 2, -1, -1):
            sink = jnp.where(seg == n, sink_ref[heads[n]] * LOG2E, sink)
        inv = 1.0 / (o_aug[HEAD_DIM:HEAD_DIM + 1, :] + jnp.exp2(sink - m))
        o_t = (o_aug[0:HEAD_DIM, :] * inv).astype(BF16)
        for n, head in enumerate(heads):
            att_t_ref[head * HEAD_DIM:(head + 1) * HEAD_DIM, r0:r0 + BLOCK] = (
                o_t[:, n * BLOCK:(n + 1) * BLOCK])

    n_pieces = tq // ATTN_PIECE
    units = [(blk, g) for blk in range(tq // BLOCK) for g in range(N_KV_HEADS)]
    per_piece = len(units) // n_pieces
    assert SCORE_LOOKAHEAD <= per_piece
    project(0)
    if n_pieces > 1:
        project(1)
    pending = [scores(*u) for u in units[:SCORE_LOOKAHEAD]]
    for piece in range(n_pieces):
        for n in range(piece * per_piece, (piece + 1) * per_piece):
            if n + SCORE_LOOKAHEAD < len(units):
                pending.append(scores(*units[n + SCORE_LOOKAHEAD]))
            attend(*units[n], pending.pop(0))
        if piece + 2 < n_pieces:
            project(piece + 2)
        if piece >= 1:
            output(piece - 1)
    output(n_pieces - 1)


def _attention(h, q_norm, w_q, kv_norm, w_kv, rope_table, sinks, w_o, post, layer,
               ffn_w_gate_up, ffn_w_down, ffn_layer, seq_len):
    t, d = h.shape
    qd = w_q.shape[2]
    kvd = w_kv.shape[1] // 2
    tq = ATTN_TILE
    tps = seq_len // tq
    bpt = tq // BLOCK
    row = lambda i: (i, 0)
    prev_blk = lambda i: (jnp.maximum(i * bpt - 1, (i // tps) * tps * bpt), 0)
    prev_pos = lambda i: (jnp.maximum((i % tps) * bpt - 1, 0), 0)
    rope_w = rope_table.shape[1]
    n_steps = t // tq
    cast_in, cast_out, cast_shape = _ffn_weight_cast_specs(ffn_w_gate_up, ffn_w_down, ffn_layer,
                                                           n_steps)
    kern = functools.partial(_attn_kernel, tiles_per_seq=tps)
    return pl.pallas_call(
        kern,
        grid=(n_steps,),
        in_specs=[
            pl.BlockSpec(memory_space=pltpu.SMEM),
            pl.BlockSpec((BLOCK, d), prev_blk),
            pl.BlockSpec((tq, d), row),
            pl.BlockSpec((BLOCK, rope_w), prev_pos),
            pl.BlockSpec((tq, rope_w), lambda i: (i % tps, 0)),
            _const_spec((1, d)),
            _layer_spec((d, qd), layer),
            _const_spec((1, d)),
            _const_spec((d, 2 * kvd)),
            _layer_spec((qd, d), layer),
            _const_spec((1, d)),
            *cast_in,
        ],
        out_specs=[pl.BlockSpec((tq, d), row), *cast_out],
        out_shape=[jax.ShapeDtypeStruct((t, d), F32), *cast_shape],
        scratch_shapes=[
            pltpu.VMEM((tq, qd), BF16),
            pltpu.VMEM((tq + BLOCK, 2 * kvd), BF16),
            pltpu.VMEM((kvd, tq + BLOCK), BF16),
            pltpu.VMEM((qd, tq), BF16),
            pltpu.VMEM((d, qd), BF16),
            pltpu.VMEM((d, 2 * kvd), BF16),
            pltpu.VMEM((qd, d), BF16),
        ],
        compiler_params=_params(("arbitrary",)),
        name="swa_attention",
    )(sinks, h, h, rope_table, rope_table, q_norm, w_q, kv_norm, w_kv, w_o, post,
      ffn_w_gate_up, ffn_w_down)


def kernel(x, a_pre_norm, a_w_in, a_conv_w, a_w_out, a_post_norm, ffn_pre_norm, ffn_w_gate_up,
           ffn_w_down, ffn_post_norm, kv_norm, w_kv, b_pre_norm, b_w_q, b_sinks, b_w_o,
           b_post_norm):
    bsz, s, d = x.shape
    n_a = a_w_in.shape[0]
    depth = ffn_w_gate_up.shape[0]
    rope_table = _rope_tables(s)
    vec = lambda g: g.reshape(1, -1)

    h = x.reshape(bsz * s, d)
    for l in range(depth):
        if l < n_a:
            h, w_gu, w_dn = _conv_mixer(h, vec(a_pre_norm[l]), a_w_in, a_conv_w[l], a_w_out,
                                        vec(a_post_norm[l]), ffn_w_gate_up, ffn_w_down, l, s)
        else:
            assert l == n_a, "K/V are projected inside the first attention layer's call only"
            j = l - n_a
            h, w_gu, w_dn = _attention(h, vec(b_pre_norm[j]), b_w_q, vec(kv_norm), w_kv, rope_table,
                                       b_sinks[j], b_w_o, vec(b_post_norm[j]), j,
                                       ffn_w_gate_up, ffn_w_down, l, s)
        h = _ffn(h, vec(ffn_pre_norm[l]), w_gu, w_dn, vec(ffn_post_norm[l]))
    return h.reshape(bsz, s, d)
```

```python
import functools
import math

import jax
import jax.numpy as jnp
import numpy as np
from jax import lax
from jax.experimental import pallas as pl
from jax.experimental.pallas import tpu as pltpu

CONV_W = 3
HEAD_DIM = 64
N_KV_HEADS = 4
WINDOW = 128
BLOCK = 128
ROT_DIM = HEAD_DIM // 4
ROPE_THETA = 500000.0
EPS = 1e-6
NEG = -1e30
LOG2E = math.log2(math.e)

LANES = 128
SUBLANES = 8
BF16_ROWS = 16
ROW_TILE = 512
CONV_SUB_ROWS = 256
CONV_LOOKAHEAD = 2
FFN_TILE = 1024
ATTN_TILE = 1024
ATTN_PIECE = 256
SCORE_LOOKAHEAD = 2
VMEM_LIMIT = 60 * 1024 * 1024

F32 = jnp.float32
BF16 = jnp.bfloat16

assert WINDOW == BLOCK and LANES == 2 * HEAD_DIM


def _rms(x, g):
    r = lax.rsqrt(jnp.mean(x * x, axis=-1, keepdims=True) + EPS)
    return (x * r) * g


def _dot(a, b):
    return jnp.dot(a, b, preferred_element_type=F32)


def _dot_nt(a, b):
    return lax.dot_general(a, b, (((1,), (1,)), ((), ())), preferred_element_type=F32)


def _const_spec(shape):
    return pl.BlockSpec(shape, lambda *_: (0,) * len(shape), pipeline_mode=pl.Buffered(1))


def _layer_spec(shape, layer):
    return pl.BlockSpec((None,) + tuple(shape), lambda *_: (layer,) + (0,) * len(shape),
                        pipeline_mode=pl.Buffered(1))


def _params(semantics):
    return pltpu.CompilerParams(dimension_semantics=semantics, vmem_limit_bytes=VMEM_LIMIT)


def _rope(t, table):
    tb = table.astype(BF16)
    cos_t, sin_up, sin_dn = (tb[:, j * LANES:(j + 1) * LANES] for j in range(3))
    outs = []
    for j in range(t.shape[1] // LANES):
        c = t[:, j * LANES:(j + 1) * LANES].astype(BF16)
        outs.append(c * cos_t
                    + pltpu.roll(c, ROT_DIM // 2, axis=1) * sin_up
                    + pltpu.roll(c, LANES - ROT_DIM // 2, axis=1) * sin_dn)
    return jnp.concatenate(outs, axis=1)


def _rope_tables(seq_len):
    half = ROT_DIM // 2
    f32 = np.float32
    inv_freq = f32(ROPE_THETA) ** (-np.arange(0, ROT_DIM, 2, dtype=f32) / f32(ROT_DIM))
    ang = np.arange(seq_len, dtype=f32)[:, None] * inv_freq[None, :]
    cos, sin = np.cos(ang), np.sin(ang)
    ones = np.ones((seq_len, HEAD_DIM - ROT_DIM), f32)
    zeros = np.zeros((seq_len, HEAD_DIM - ROT_DIM), f32)
    zh = np.zeros((seq_len, half), f32)
    reps = LANES // HEAD_DIM
    cos_t = np.tile(np.concatenate([cos, cos, ones], axis=1), (1, reps))
    sin_up = np.tile(np.concatenate([zh, sin, zeros], axis=1), (1, reps))
    sin_dn = np.tile(np.concatenate([-sin, zh, zeros], axis=1), (1, reps))
    return jnp.asarray(np.concatenate([cos_t, sin_up, sin_dn], axis=1), F32)


def _ffn_weight_cast_specs(w_gate_up, w_down, layer, n_steps):
    _, d, f2 = w_gate_up.shape
    _, f, _ = w_down.shape
    assert d % (n_steps * BF16_ROWS) == 0 and f % (n_steps * BF16_ROWS) == 0
    slab = lambda i: (layer, i, 0)
    in_specs = [pl.BlockSpec((None, d // n_steps, f2), slab),
                pl.BlockSpec((None, f // n_steps, d), slab)]
    out_specs = [pl.BlockSpec((d // n_steps, f2), lambda i: (i, 0)),
                 pl.BlockSpec((f // n_steps, d), lambda i: (i, 0))]
    out_shape = [jax.ShapeDtypeStruct((d, f2), BF16), jax.ShapeDtypeStruct((f, d), BF16)]
    return in_specs, out_specs, out_shape


def _cast_ffn_weights(wgu_src, wd_src, wgu_dst, wd_dst):
    wgu_dst[...] = wgu_src[...].astype(BF16)
    wd_dst[...] = wd_src[...].astype(BF16)


def _conv_mixer_kernel(h_ref, pre_ref, win_ref, cw_ref, wout_ref, post_ref, wgu_src, wd_src,
                       o_ref, wgu_dst, wd_dst, cu_ref, win_bf_ref, wout_bf_ref,
                       *, tiles_per_seq, sub_rows):
    tm, d = h_ref.shape
    i = pl.program_id(0)
    _cast_ffn_weights(wgu_src, wd_src, wgu_dst, wd_dst)

    @pl.when(i == 0)
    def _():
        win_bf_ref[...] = win_ref[...].astype(BF16)
        wout_bf_ref[...] = wout_ref[...].astype(BF16)

    @pl.when(i % tiles_per_seq == 0)
    def _():
        cu_ref[0:SUBLANES, :] = jnp.zeros((SUBLANES, d), F32)

    @pl.when(i % tiles_per_seq != 0)
    def _():
        cu_ref[0:SUBLANES, :] = cu_ref[tm:tm + SUBLANES, :]

    cw = cw_ref[...]

    def gated_conv(r0):
        hn = _rms(h_ref[r0:r0 + sub_rows, :], pre_ref[...]).astype(BF16)
        cu = _dot(hn, win_bf_ref[:, d:2 * d]) * _dot(hn, win_bf_ref[:, 2 * d:3 * d])
        base = SUBLANES + r0
        cu_ref[base:base + sub_rows, :] = cu
        conv = (cu_ref[base - 2:base - 2 + sub_rows, :] * cw[0:1, :]
                + cu_ref[base - 1:base - 1 + sub_rows, :] * cw[1:2, :]
                + cu * cw[2:3, :])
        return (_dot(hn, win_bf_ref[:, 0:d]) * conv).astype(BF16)

    def output(r0, y):
        mix = _dot(y, wout_bf_ref[...])
        o_ref[r0:r0 + sub_rows, :] = h_ref[r0:r0 + sub_rows, :] + _rms(mix, post_ref[...])

    starts = list(range(0, tm, sub_rows))
    pending = [gated_conv(r0) for r0 in starts[:CONV_LOOKAHEAD]]
    for n, r0 in enumerate(starts):
        if n + CONV_LOOKAHEAD < len(starts):
            pending.append(gated_conv(starts[n + CONV_LOOKAHEAD]))
        output(r0, pending.pop(0))


def _conv_mixer(h, pre, w_in, conv_w, w_out, post, ffn_w_gate_up, ffn_w_down, layer, seq_len):
    t, d = h.shape
    tm = 2 * ROW_TILE
    n_steps = t // tm
    cast_in, cast_out, cast_shape = _ffn_weight_cast_specs(ffn_w_gate_up, ffn_w_down, layer, n_steps)
    kern = functools.partial(_conv_mixer_kernel, tiles_per_seq=seq_len // tm,
                             sub_rows=CONV_SUB_ROWS)
    return pl.pallas_call(
        kern,
        grid=(n_steps,),
        in_specs=[
            pl.BlockSpec((tm, d), lambda i: (i, 0)),
            _const_spec((1, d)),
            _layer_spec((d, 3 * d), layer),
            _const_spec((CONV_W, d)),
            _layer_spec((d, d), layer),
            _const_spec((1, d)),
            *cast_in,
        ],
        out_specs=[pl.BlockSpec((tm, d), lambda i: (i, 0)), *cast_out],
        out_shape=[jax.ShapeDtypeStruct((t, d), F32), *cast_shape],
        scratch_shapes=[pltpu.VMEM((tm + SUBLANES, d), F32),
                        pltpu.VMEM((d, 3 * d), BF16), pltpu.VMEM((d, d), BF16)],
        compiler_params=_params(("arbitrary",)),
        name="conv_mixer",
    )(h, pre, w_in, conv_w, w_out, post, ffn_w_gate_up, ffn_w_down)


def _ffn_kernel(h_ref, pre_ref, wgu_ref, wd_ref, post_ref, o_ref, *, chunks, sub_rows):
    f = wd_ref.shape[0]
    for r0 in range(0, h_ref.shape[0], sub_rows):
        h = h_ref[r0:r0 + sub_rows, :]
        hn = _rms(h, pre_ref[...]).astype(BF16)
        acc = None
        for lo, hi in chunks:
            g = _dot(hn, wgu_ref[:, lo:hi])
            u = _dot(hn, wgu_ref[:, f + lo:f + hi])
            a = (g * jax.nn.sigmoid(g) * u).astype(BF16)
            part = _dot(a, wd_ref[lo:hi, :])
            acc = part if acc is None else acc + part
        o_ref[r0:r0 + sub_rows, :] = h + _rms(acc, post_ref[...])


def _ffn_chunks(f):
    step = 768
    return tuple((lo, min(lo + step, f)) for lo in range(0, f, step))


def _ffn(h, pre, w_gate_up, w_down, post):
    t, d = h.shape
    f = w_down.shape[0]
    tm = FFN_TILE
    row = lambda i: (i, 0)
    return pl.pallas_call(
        functools.partial(_ffn_kernel, chunks=_ffn_chunks(f), sub_rows=ROW_TILE),
        grid=(t // tm,),
        in_specs=[
            pl.BlockSpec((tm, d), row),
            _const_spec((1, d)),
            _const_spec((d, 2 * f)),
            _const_spec((f, d)),
            _const_spec((1, d)),
        ],
        out_specs=pl.BlockSpec((tm, d), row),
        out_shape=jax.ShapeDtypeStruct((t, d), F32),
        compiler_params=_params(("parallel",)),
        name="swiglu",
    )(h, pre, w_gate_up, w_down, post)


def _attn_kernel(sink_ref, hp_ref, h_ref, ropep_ref, rope_ref, qn_ref, wq_ref, kvn_ref, wkv_ref,
                 wo_ref, post_ref, wgu_src, wd_src, o_ref, wgu_dst, wd_dst,
                 q_ref, k_ref, vt_ref, att_t_ref, wq_bf_ref, wkv_bf_ref, wo_bf_ref,
                 *, tiles_per_seq):
    _cast_ffn_weights(wgu_src, wd_src, wgu_dst, wd_dst)
    tq = q_ref.shape[0]
    kvd = vt_ref.shape[0]
    n_heads = q_ref.shape[1] // HEAD_DIM
    group = n_heads // N_KV_HEADS
    assert group * HEAD_DIM == 2 * LANES, "one unit = the two q lane chunks of a kv head"
    first_tile = (pl.program_id(0) % tiles_per_seq) == 0

    @pl.when(pl.program_id(0) == 0)
    def _():
        wq_bf_ref[...] = wq_ref[...].astype(BF16)
        wkv_bf_ref[...] = wkv_ref[...].astype(BF16)
        wo_bf_ref[...] = wo_ref[...].astype(BF16)

    lo_half = lax.broadcasted_iota(jnp.int32, (BLOCK, LANES), 1) < HEAD_DIM

    def unit_rms(x):
        return x * lax.rsqrt(jnp.mean(x * x, axis=-1, keepdims=True) + EPS)

    def project_kv(rows_unit, rows_rope, r0):
        n = rows_unit.shape[0]
        kv = _dot((rows_unit * kvn_ref[...]).astype(BF16), wkv_bf_ref[...])
        k = _rope(kv[:, :kvd], rows_rope)
        lo = lax.broadcasted_iota(jnp.int32, (n, LANES), 1) < HEAD_DIM
        for c in range(kvd // LANES):
            pair = k[:, c * LANES:(c + 1) * LANES]
            swapped = pltpu.roll(pair, HEAD_DIM, axis=1)
            k_ref[r0:r0 + n, (2 * c) * LANES:(2 * c + 1) * LANES] = jnp.where(lo, pair, swapped)
            k_ref[r0:r0 + n, (2 * c + 1) * LANES:(2 * c + 2) * LANES] = jnp.where(lo, swapped, pair)
        vt_ref[:, r0:r0 + n] = kv[:, kvd:].T.astype(BF16)

    q_gain = qn_ref[...] * (LOG2E / math.sqrt(HEAD_DIM))

    def project(piece):
        rows = slice(piece * ATTN_PIECE, (piece + 1) * ATTN_PIECE)
        if piece == 0:
            project_kv(unit_rms(hp_ref[...]), ropep_ref[...], 0)
        h_unit = unit_rms(h_ref[rows, :])
        project_kv(h_unit, rope_ref[rows, :], BLOCK + piece * ATTN_PIECE)
        q = _dot((h_unit * q_gain).astype(BF16), wq_bf_ref[...])
        q_ref[rows, :] = _rope(q, rope_ref[rows, :])

    def output(piece):
        rows = slice(piece * ATTN_PIECE, (piece + 1) * ATTN_PIECE)
        att = att_t_ref[:, rows].T
        proj = _dot(att, wo_bf_ref[...])
        o_ref[rows, :] = h_ref[rows, :] + _rms(proj, post_ref[...])

    unit_heads = (0, 2, 1, 3)
    width = group * BLOCK
    key_j = lax.broadcasted_iota(jnp.int32, (BLOCK, BLOCK), 0)
    query_i = lax.broadcasted_iota(jnp.int32, (BLOCK, BLOCK), 1)
    from_prev_blk = key_j > query_i

    def by_head(fn, *arrays):
        return jnp.concatenate(
            [fn(from_prev_blk, *(a[:, n * BLOCK:(n + 1) * BLOCK] for a in arrays))
             for n in range(group)], axis=1)
    seg = lax.broadcasted_iota(jnp.int32, (1, width), 1) // BLOCK
    ones_rows = jnp.ones((BF16_ROWS, 2 * BLOCK), BF16)

    prev_bias = jnp.where(first_tile, NEG, 0.0).astype(F32)

    def scores(blk, g):
        r0 = blk * BLOCK
        k_win = k_ref[r0:r0 + 2 * BLOCK, g * LANES:(g + 1) * LANES]
        qc0 = q_ref[r0:r0 + BLOCK, (2 * g) * LANES:(2 * g + 1) * LANES]
        qc1 = q_ref[r0:r0 + BLOCK, (2 * g + 1) * LANES:(2 * g + 2) * LANES]
        zero = jnp.zeros_like(qc0)
        q_split = jnp.concatenate(
            [jnp.where(lo_half, qc0, zero), jnp.where(lo_half, qc1, zero),
             jnp.where(lo_half, zero, qc0), jnp.where(lo_half, zero, qc1)], axis=0)
        return _dot_nt(k_win, q_split)

    def attend(blk, g, s_t):
        r0 = blk * BLOCK
        v_win = vt_ref[g * HEAD_DIM:(g + 1) * HEAD_DIM, r0:r0 + 2 * BLOCK]
        s_prev = s_t[0:BLOCK]
        if blk == 0:
            s_prev = s_prev + prev_bias
        s_win = by_head(jnp.where, s_prev, s_t[BLOCK:])
        m = jnp.max(s_win, axis=0, keepdims=True)
        e = jnp.exp2(s_win - m)
        p_t = jnp.concatenate([by_head(lambda mask, x: jnp.where(mask, x, 0.0), e),
                               by_head(lambda mask, x: jnp.where(mask, 0.0, x), e)],
                              axis=0).astype(BF16)
        v_aug = jnp.concatenate([v_win, ones_rows], axis=0)
        o_aug = _dot(v_aug, p_t)
        heads = [g * group + u for u in unit_heads]
        sink = sink_ref[heads[-1]] * LOG2E
        for n in range(group - 2, -1, -1):
            sink = jnp.where(seg == n, sink_ref[heads[n]] * LOG2E, sink)
        inv = 1.0 / (o_aug[HEAD_DIM:HEAD_DIM + 1, :] + jnp.exp2(sink - m))
        o_t = (o_aug[0:HEAD_DIM, :] * inv).astype(BF16)
        for n, head in enumerate(heads):
            att_t_ref[head * HEAD_DIM:(head + 1) * HEAD_DIM, r0:r0 + BLOCK] = (
                o_t[:, n * BLOCK:(n + 1) * BLOCK])

    n_pieces = tq // ATTN_PIECE
    units = [(blk, g) for blk in range(tq // BLOCK) for g in range(N_KV_HEADS)]
    per_piece = len(units) // n_pieces
    assert SCORE_LOOKAHEAD <= per_piece
    project(0)
    if n_pieces > 1:
        project(1)
    pending = [scores(*u) for u in units[:SCORE_LOOKAHEAD]]
    for piece in range(n_pieces):
        for n in range(piece * per_piece, (piece + 1) * per_piece):
            if n + SCORE_LOOKAHEAD < len(units):
                pending.append(scores(*units[n + SCORE_LOOKAHEAD]))
            attend(*units[n], pending.pop(0))
        if piece + 2 < n_pieces:
            project(piece + 2)
        if piece >= 1:
            output(piece - 1)
    output(n_pieces - 1)


def _attention(h, q_norm, w_q, kv_norm, w_kv, rope_table, sinks, w_o, post, layer,
               ffn_w_gate_up, ffn_w_down, ffn_layer, seq_len):
    t, d = h.shape
    qd = w_q.shape[2]
    kvd = w_kv.shape[1] // 2
    tq = ATTN_TILE
    tps = seq_len // tq
    bpt = tq // BLOCK
    row = lambda i: (i, 0)
    prev_blk = lambda i: (jnp.maximum(i * bpt - 1, (i // tps) * tps * bpt), 0)
    prev_pos = lambda i: (jnp.maximum((i % tps) * bpt - 1, 0), 0)
    rope_w = rope_table.shape[1]
    n_steps = t // tq
    cast_in, cast_out, cast_shape = _ffn_weight_cast_specs(ffn_w_gate_up, ffn_w_down, ffn_layer,
                                                           n_steps)
    kern = functools.partial(_attn_kernel, tiles_per_seq=tps)
    return pl.pallas_call(
        kern,
        grid=(n_steps,),
        in_specs=[
            pl.BlockSpec(memory_space=pltpu.SMEM),
            pl.BlockSpec((BLOCK, d), prev_blk),
            pl.BlockSpec((tq, d), row),
            pl.BlockSpec((BLOCK, rope_w), prev_pos),
            pl.BlockSpec((tq, rope_w), lambda i: (i % tps, 0)),
            _const_spec((1, d)),
            _layer_spec((d, qd), layer),
            _const_spec((1, d)),
            _const_spec((d, 2 * kvd)),
            _layer_spec((qd, d), layer),
            _const_spec((1, d)),
            *cast_in,
        ],
        out_specs=[pl.BlockSpec((tq, d), row), *cast_out],
        out_shape=[jax.ShapeDtypeStruct((t, d), F32), *cast_shape],
        scratch_shapes=[
            pltpu.VMEM((tq, qd), BF16),
            pltpu.VMEM((tq + BLOCK, 2 * kvd), BF16),
            pltpu.VMEM((kvd, tq + BLOCK), BF16),
            pltpu.VMEM((qd, tq), BF16),
            pltpu.VMEM((d, qd), BF16),
            pltpu.VMEM((d, 2 * kvd), BF16),
            pltpu.VMEM((qd, d), BF16),
        ],
        compiler_params=_params(("arbitrary",)),
        name="swa_attention",
    )(sinks, h, h, rope_table, rope_table, q_norm, w_q, kv_norm, w_kv, w_o, post,
      ffn_w_gate_up, ffn_w_down)


def kernel(x, a_pre_norm, a_w_in, a_conv_w, a_w_out, a_post_norm, ffn_pre_norm, ffn_w_gate_up,
           ffn_w_down, ffn_post_norm, kv_norm, w_kv, b_pre_norm, b_w_q, b_sinks, b_w_o,
           b_post_norm):
    bsz, s, d = x.shape
    n_a = a_w_in.shape[0]
    depth = ffn_w_gate_up.shape[0]
    rope_table = _rope_tables(s)
    vec = lambda g: g.reshape(1, -1)

    h = x.reshape(bsz * s, d)
    for l in range(depth):
        if l < n_a:
            h, w_gu, w_dn = _conv_mixer(h, vec(a_pre_norm[l]), a_w_in, a_conv_w[l], a_w_out,
                                        vec(a_post_norm[l]), ffn_w_gate_up, ffn_w_down, l, s)
        else:
            assert l == n_a, "K/V are projected inside the first attention layer's call only"
            j = l - n_a
            h, w_gu, w_dn = _attention(h, vec(b_pre_norm[j]), b_w_q, vec(kv_norm), w_kv, rope_table,
                                       b_sinks[j], b_w_o, vec(b_post_norm[j]), j,
                                       ffn_w_gate_up, ffn_w_down, l, s)
        h = _ffn(h, vec(ffn_pre_norm[l]), w_gu, w_dn, vec(ffn_post_norm[l]))
    return h.reshape(bsz, s, d)
```

```python
import functools
import math

import jax
import jax.numpy as jnp
import numpy as np
from jax import lax
from jax.experimental import pallas as pl
from jax.experimental.pallas import tpu as pltpu

CONV_W = 3
HEAD_DIM = 64
N_KV_HEADS = 4
WINDOW = 128
BLOCK = 128
ROT_DIM = HEAD_DIM // 4
ROPE_THETA = 500000.0
EPS = 1e-6
NEG = -1e30
LOG2E = math.log2(math.e)

LANES = 128
SUBLANES = 8
BF16_ROWS = 16
ROW_TILE = 512
CONV_SUB_ROWS = 512
CONV_LOOKAHEAD = 1
FFN_TILE = 1024
ATTN_TILE = 1024
ATTN_PIECE = 256
SCORE_LOOKAHEAD = 2
VMEM_LIMIT = 60 * 1024 * 1024

F32 = jnp.float32
BF16 = jnp.bfloat16

assert WINDOW == BLOCK and LANES == 2 * HEAD_DIM


def _rms(x, g):
    r = lax.rsqrt(jnp.mean(x * x, axis=-1, keepdims=True) + EPS)
    return (x * r) * g


def _dot(a, b):
    return jnp.dot(a, b, preferred_element_type=F32)


def _dot_nt(a, b):
    return lax.dot_general(a, b, (((1,), (1,)), ((), ())), preferred_element_type=F32)


def _const_spec(shape):
    return pl.BlockSpec(shape, lambda *_: (0,) * len(shape), pipeline_mode=pl.Buffered(1))


def _layer_spec(shape, layer):
    return pl.BlockSpec((None,) + tuple(shape), lambda *_: (layer,) + (0,) * len(shape),
                        pipeline_mode=pl.Buffered(1))


def _params(semantics):
    return pltpu.CompilerParams(dimension_semantics=semantics, vmem_limit_bytes=VMEM_LIMIT)


def _rope(t, table):
    tb = table.astype(BF16)
    cos_t, sin_up, sin_dn = (tb[:, j * LANES:(j + 1) * LANES] for j in range(3))
    outs = []
    for j in range(t.shape[1] // LANES):
        c = t[:, j * LANES:(j + 1) * LANES].astype(BF16)
        outs.append(c * cos_t
                    + pltpu.roll(c, ROT_DIM // 2, axis=1) * sin_up
                    + pltpu.roll(c, LANES - ROT_DIM // 2, axis=1) * sin_dn)
    return jnp.concatenate(outs, axis=1)


def _rope_tables(seq_len):
    half = ROT_DIM // 2
    f32 = np.float32
    inv_freq = f32(ROPE_THETA) ** (-np.arange(0, ROT_DIM, 2, dtype=f32) / f32(ROT_DIM))
    ang = np.arange(seq_len, dtype=f32)[:, None] * inv_freq[None, :]
    cos, sin = np.cos(ang), np.sin(ang)
    ones = np.ones((seq_len, HEAD_DIM - ROT_DIM), f32)
    zeros = np.zeros((seq_len, HEAD_DIM - ROT_DIM), f32)
    zh = np.zeros((seq_len, half), f32)
    reps = LANES // HEAD_DIM
    cos_t = np.tile(np.concatenate([cos, cos, ones], axis=1), (1, reps))
    sin_up = np.tile(np.concatenate([zh, sin, zeros], axis=1), (1, reps))
    sin_dn = np.tile(np.concatenate([-sin, zh, zeros], axis=1), (1, reps))
    return jnp.asarray(np.concatenate([cos_t, sin_up, sin_dn], axis=1), F32)


def _ffn_weight_cast_specs(w_gate_up, w_down, layer, n_steps):
    _, d, f2 = w_gate_up.shape
    _, f, _ = w_down.shape
    assert d % (n_steps * BF16_ROWS) == 0 and f % (n_steps * BF16_ROWS) == 0
    slab = lambda i: (layer, i, 0)
    in_specs = [pl.BlockSpec((None, d // n_steps, f2), slab),
                pl.BlockSpec((None, f // n_steps, d), slab)]
    out_specs = [pl.BlockSpec((d // n_steps, f2), lambda i: (i, 0)),
                 pl.BlockSpec((f // n_steps, d), lambda i: (i, 0))]
    out_shape = [jax.ShapeDtypeStruct((d, f2), BF16), jax.ShapeDtypeStruct((f, d), BF16)]
    return in_specs, out_specs, out_shape


def _cast_ffn_weights(wgu_src, wd_src, wgu_dst, wd_dst):
    wgu_dst[...] = wgu_src[...].astype(BF16)
    wd_dst[...] = wd_src[...].astype(BF16)


def _conv_mixer_kernel(h_ref, pre_ref, win_ref, cw_ref, wout_ref, post_ref, wgu_src, wd_src,
                       o_ref, wgu_dst, wd_dst, cu_ref, win_bf_ref, wout_bf_ref,
                       *, tiles_per_seq, sub_rows):
    tm, d = h_ref.shape
    i = pl.program_id(0)
    _cast_ffn_weights(wgu_src, wd_src, wgu_dst, wd_dst)

    @pl.when(i == 0)
    def _():
        win_bf_ref[...] = win_ref[...].astype(BF16)
        wout_bf_ref[...] = wout_ref[...].astype(BF16)

    @pl.when(i % tiles_per_seq == 0)
    def _():
        cu_ref[0:SUBLANES, :] = jnp.zeros((SUBLANES, d), F32)

    @pl.when(i % tiles_per_seq != 0)
    def _():
        cu_ref[0:SUBLANES, :] = cu_ref[tm:tm + SUBLANES, :]

    cw = cw_ref[...]

    def gated_conv(r0):
        hn = _rms(h_ref[r0:r0 + sub_rows, :], pre_ref[...]).astype(BF16)
        cu = _dot(hn, win_bf_ref[:, d:2 * d]) * _dot(hn, win_bf_ref[:, 2 * d:3 * d])
        base = SUBLANES + r0
        cu_ref[base:base + sub_rows, :] = cu
        conv = (cu_ref[base - 2:base - 2 + sub_rows, :] * cw[0:1, :]
                + cu_ref[base - 1:base - 1 + sub_rows, :] * cw[1:2, :]
                + cu * cw[2:3, :])
        return (_dot(hn, win_bf_ref[:, 0:d]) * conv).astype(BF16)

    def output(r0, y):
        mix = _dot(y, wout_bf_ref[...])
        o_ref[r0:r0 + sub_rows, :] = h_ref[r0:r0 + sub_rows, :] + _rms(mix, post_ref[...])

    starts = list(range(0, tm, sub_rows))
    pending = [gated_conv(r0) for r0 in starts[:CONV_LOOKAHEAD]]
    for n, r0 in enumerate(starts):
        if n + CONV_LOOKAHEAD < len(starts):
            pending.append(gated_conv(starts[n + CONV_LOOKAHEAD]))
        output(r0, pending.pop(0))


def _conv_mixer(h, pre, w_in, conv_w, w_out, post, ffn_w_gate_up, ffn_w_down, layer, seq_len):
    t, d = h.shape
    tm = 2 * ROW_TILE
    n_steps = t // tm
    cast_in, cast_out, cast_shape = _ffn_weight_cast_specs(ffn_w_gate_up, ffn_w_down, layer, n_steps)
    kern = functools.partial(_conv_mixer_kernel, tiles_per_seq=seq_len // tm,
                             sub_rows=CONV_SUB_ROWS)
    return pl.pallas_call(
        kern,
        grid=(n_steps,),
        in_specs=[
            pl.BlockSpec((tm, d), lambda i: (i, 0)),
            _const_spec((1, d)),
            _layer_spec((d, 3 * d), layer),
            _const_spec((CONV_W, d)),
            _layer_spec((d, d), layer),
            _const_spec((1, d)),
            *cast_in,
        ],
        out_specs=[pl.BlockSpec((tm, d), lambda i: (i, 0)), *cast_out],
        out_shape=[jax.ShapeDtypeStruct((t, d), F32), *cast_shape],
        scratch_shapes=[pltpu.VMEM((tm + SUBLANES, d), F32),
                        pltpu.VMEM((d, 3 * d), BF16), pltpu.VMEM((d, d), BF16)],
        compiler_params=_params(("arbitrary",)),
        name="conv_mixer",
    )(h, pre, w_in, conv_w, w_out, post, ffn_w_gate_up, ffn_w_down)


def _ffn_kernel(h_ref, pre_ref, wgu_ref, wd_ref, post_ref, o_ref, *, chunks, sub_rows):
    f = wd_ref.shape[0]
    for r0 in range(0, h_ref.shape[0], sub_rows):
        h = h_ref[r0:r0 + sub_rows, :]
        hn = _rms(h, pre_ref[...]).astype(BF16)
        acc = None
        for lo, hi in chunks:
            g = _dot(hn, wgu_ref[:, lo:hi])
            u = _dot(hn, wgu_ref[:, f + lo:f + hi])
            a = (g * jax.nn.sigmoid(g) * u).astype(BF16)
            part = _dot(a, wd_ref[lo:hi, :])
            acc = part if acc is None else acc + part
        o_ref[r0:r0 + sub_rows, :] = h + _rms(acc, post_ref[...])


def _ffn_chunks(f):
    step = 768
    return tuple((lo, min(lo + step, f)) for lo in range(0, f, step))


def _ffn(h, pre, w_gate_up, w_down, post):
    t, d = h.shape
    f = w_down.shape[0]
    tm = FFN_TILE
    row = lambda i: (i, 0)
    return pl.pallas_call(
        functools.partial(_ffn_kernel, chunks=_ffn_chunks(f), sub_rows=FFN_TILE),
        grid=(t // tm,),
        in_specs=[
            pl.BlockSpec((tm, d), row),
            _const_spec((1, d)),
            _const_spec((d, 2 * f)),
            _const_spec((f, d)),
            _const_spec((1, d)),
        ],
        out_specs=pl.BlockSpec((tm, d), row),
        out_shape=jax.ShapeDtypeStruct((t, d), F32),
        compiler_params=_params(("parallel",)),
        name="swiglu",
    )(h, pre, w_gate_up, w_down, post)


def _attn_kernel(sink_ref, hp_ref, h_ref, ropep_ref, rope_ref, qn_ref, wq_ref, kvn_ref, wkv_ref,
                 wo_ref, post_ref, wgu_src, wd_src, o_ref, wgu_dst, wd_dst,
                 q_ref, k_ref, vt_ref, att_t_ref, wq_bf_ref, wkv_bf_ref, wo_bf_ref,
                 *, tiles_per_seq):
    _cast_ffn_weights(wgu_src, wd_src, wgu_dst, wd_dst)
    tq = q_ref.shape[0]
    kvd = vt_ref.shape[0]
    n_heads = q_ref.shape[1] // HEAD_DIM
    group = n_heads // N_KV_HEADS
    assert group * HEAD_DIM == 2 * LANES, "one unit = the two q lane chunks of a kv head"
    first_tile = (pl.program_id(0) % tiles_per_seq) == 0

    @pl.when(pl.program_id(0) == 0)
    def _():
        wq_bf_ref[...] = wq_ref[...].astype(BF16)
        wkv_bf_ref[...] = wkv_ref[...].astype(BF16)
        wo_bf_ref[...] = wo_ref[...].astype(BF16)

    lo_half = lax.broadcasted_iota(jnp.int32, (BLOCK, LANES), 1) < HEAD_DIM

    def unit_rms(x):
        return x * lax.rsqrt(jnp.mean(x * x, axis=-1, keepdims=True) + EPS)

    def project_kv(rows_unit, rows_rope, r0):
        n = rows_unit.shape[0]
        kv = _dot((rows_unit * kvn_ref[...]).astype(BF16), wkv_bf_ref[...])
        k = _rope(kv[:, :kvd], rows_rope)
        lo = lax.broadcasted_iota(jnp.int32, (n, LANES), 1) < HEAD_DIM
        for c in range(kvd // LANES):
            pair = k[:, c * LANES:(c + 1) * LANES]
            swapped = pltpu.roll(pair, HEAD_DIM, axis=1)
            k_ref[r0:r0 + n, (2 * c) * LANES:(2 * c + 1) * LANES] = jnp.where(lo, pair, swapped)
            k_ref[r0:r0 + n, (2 * c + 1) * LANES:(2 * c + 2) * LANES] = jnp.where(lo, swapped, pair)
        vt_ref[:, r0:r0 + n] = kv[:, kvd:].T.astype(BF16)

    q_gain = qn_ref[...] * (LOG2E / math.sqrt(HEAD_DIM))

    def project(piece):
        rows = slice(piece * ATTN_PIECE, (piece + 1) * ATTN_PIECE)
        if piece == 0:
            project_kv(unit_rms(hp_ref[...]), ropep_ref[...], 0)
        h_unit = unit_rms(h_ref[rows, :])
        project_kv(h_unit, rope_ref[rows, :], BLOCK + piece * ATTN_PIECE)
        q = _dot((h_unit * q_gain).astype(BF16), wq_bf_ref[...])
        q_ref[rows, :] = _rope(q, rope_ref[rows, :])

    def output(piece):
        rows = slice(piece * ATTN_PIECE, (piece + 1) * ATTN_PIECE)
        att = att_t_ref[:, rows].T
        proj = _dot(att, wo_bf_ref[...])
        o_ref[rows, :] = h_ref[rows, :] + _rms(proj, post_ref[...])

    unit_heads = (0, 2, 1, 3)
    width = group * BLOCK
    key_j = lax.broadcasted_iota(jnp.int32, (BLOCK, BLOCK), 0)
    query_i = lax.broadcasted_iota(jnp.int32, (BLOCK, BLOCK), 1)
    from_prev_blk = key_j > query_i

    def by_head(fn, *arrays):
        return jnp.concatenate(
            [fn(from_prev_blk, *(a[:, n * BLOCK:(n + 1) * BLOCK] for a in arrays))
             for n in range(group)], axis=1)
    seg = lax.broadcasted_iota(jnp.int32, (1, width), 1) // BLOCK
    ones_rows = jnp.ones((BF16_ROWS, 2 * BLOCK), BF16)

    prev_bias = jnp.where(first_tile, NEG, 0.0).astype(F32)

    def scores(blk, g):
        r0 = blk * BLOCK
        k_win = k_ref[r0:r0 + 2 * BLOCK, g * LANES:(g + 1) * LANES]
        qc0 = q_ref[r0:r0 + BLOCK, (2 * g) * LANES:(2 * g + 1) * LANES]
        qc1 = q_ref[r0:r0 + BLOCK, (2 * g + 1) * LANES:(2 * g + 2) * LANES]
        zero = jnp.zeros_like(qc0)
        q_split = jnp.concatenate(
            [jnp.where(lo_half, qc0, zero), jnp.where(lo_half, qc1, zero),
             jnp.where(lo_half, zero, qc0), jnp.where(lo_half, zero, qc1)], axis=0)
        return _dot_nt(k_win, q_split)

    def attend(blk, g, s_t):
        r0 = blk * BLOCK
        v_win = vt_ref[g * HEAD_DIM:(g + 1) * HEAD_DIM, r0:r0 + 2 * BLOCK]
        s_prev = s_t[0:BLOCK]
        if blk == 0:
            s_prev = s_prev + prev_bias
        s_win = by_head(jnp.where, s_prev, s_t[BLOCK:])
        m = jnp.max(s_win, axis=0, keepdims=True)
        e = jnp.exp2(s_win - m)
        p_t = jnp.concatenate([by_head(lambda mask, x: jnp.where(mask, x, 0.0), e),
                               by_head(lambda mask, x: jnp.where(mask, 0.0, x), e)],
                              axis=0).astype(BF16)
        v_aug = jnp.concatenate([v_win, ones_rows], axis=0)
        o_aug = _dot(v_aug, p_t)
        heads = [g * group + u for u in unit_heads]
        sink = sink_ref[heads[-1]] * LOG2E
        for n in range(group - 2, -1, -1):
            sink = jnp.where(seg == n, sink_ref[heads[n]] * LOG2E, sink)
        inv = 1.0 / (o_aug[HEAD_DIM:HEAD_DIM + 1, :] + jnp.exp2(sink - m))
        o_t = (o_aug[0:HEAD_DIM, :] * inv).astype(BF16)
        for n, head in enumerate(heads):
            att_t_ref[head * HEAD_DIM:(head + 1) * HEAD_DIM, r0:r0 + BLOCK] = (
                o_t[:, n * BLOCK:(n + 1) * BLOCK])

    n_pieces = tq // ATTN_PIECE
    units = [(blk, g) for blk in range(tq // BLOCK) for g in range(N_KV_HEADS)]
    per_piece = len(units) // n_pieces
    assert SCORE_LOOKAHEAD <= per_piece
    project(0)
    if n_pieces > 1:
        project(1)
    pending = [scores(*u) for u in units[:SCORE_LOOKAHEAD]]
    for piece in range(n_pieces):
        for n in range(piece * per_piece, (piece + 1) * per_piece):
            if n + SCORE_LOOKAHEAD < len(units):
                pending.append(scores(*units[n + SCORE_LOOKAHEAD]))
            attend(*units[n], pending.pop(0))
        if piece + 2 < n_pieces:
            project(piece + 2)
        if piece >= 1:
            output(piece - 1)
    output(n_pieces - 1)


def _attention(h, q_norm, w_q, kv_norm, w_kv, rope_table, sinks, w_o, post, layer,
               ffn_w_gate_up, ffn_w_down, ffn_layer, seq_len):
    t, d = h.shape
    qd = w_q.shape[2]
    kvd = w_kv.shape[1] // 2
    tq = ATTN_TILE
    tps = seq_len // tq
    bpt = tq // BLOCK
    row = lambda i: (i, 0)
    prev_blk = lambda i: (jnp.maximum(i * bpt - 1, (i // tps) * tps * bpt), 0)
    prev_pos = lambda i: (jnp.maximum((i % tps) * bpt - 1, 0), 0)
    rope_w = rope_table.shape[1]
    n_steps = t // tq
    cast_in, cast_out, cast_shape = _ffn_weight_cast_specs(ffn_w_gate_up, ffn_w_down, ffn_layer,
                                                           n_steps)
    kern = functools.partial(_attn_kernel, tiles_per_seq=tps)
    return pl.pallas_call(
        kern,
        grid=(n_steps,),
        in_specs=[
            pl.BlockSpec(memory_space=pltpu.SMEM),
            pl.BlockSpec((BLOCK, d), prev_blk),
            pl.BlockSpec((tq, d), row),
            pl.BlockSpec((BLOCK, rope_w), prev_pos),
            pl.BlockSpec((tq, rope_w), lambda i: (i % tps, 0)),
            _const_spec((1, d)),
            _layer_spec((d, qd), layer),
            _const_spec((1, d)),
            _const_spec((d, 2 * kvd)),
            _layer_spec((qd, d), layer),
            _const_spec((1, d)),
            *cast_in,
        ],
        out_specs=[pl.BlockSpec((tq, d), row), *cast_out],
        out_shape=[jax.ShapeDtypeStruct((t, d), F32), *cast_shape],
        scratch_shapes=[
            pltpu.VMEM((tq, qd), BF16),
            pltpu.VMEM((tq + BLOCK, 2 * kvd), BF16),
            pltpu.VMEM((kvd, tq + BLOCK), BF16),
            pltpu.VMEM((qd, tq), BF16),
            pltpu.VMEM((d, qd), BF16),
            pltpu.VMEM((d, 2 * kvd), BF16),
            pltpu.VMEM((qd, d), BF16),
        ],
        compiler_params=_params(("arbitrary",)),
        name="swa_attention",
    )(sinks, h, h, rope_table, rope_table, q_norm, w_q, kv_norm, w_kv, w_o, post,
      ffn_w_gate_up, ffn_w_down)


def kernel(x, a_pre_norm, a_w_in, a_conv_w, a_w_out, a_post_norm, ffn_pre_norm, ffn_w_gate_up,
           ffn_w_down, ffn_post_norm, kv_norm, w_kv, b_pre_norm, b_w_q, b_sinks, b_w_o,
           b_post_norm):
    bsz, s, d = x.shape
    n_a = a_w_in.shape[0]
    depth = ffn_w_gate_up.shape[0]
    rope_table = _rope_tables(s)
    vec = lambda g: g.reshape(1, -1)

    h = x.reshape(bsz * s, d)
    for l in range(depth):
        if l < n_a:
            h, w_gu, w_dn = _conv_mixer(h, vec(a_pre_norm[l]), a_w_in, a_conv_w[l], a_w_out,
                                        vec(a_post_norm[l]), ffn_w_gate_up, ffn_w_down, l, s)
        else:
            assert l == n_a, "K/V are projected inside the first attention layer's call only"
            j = l - n_a
            h, w_gu, w_dn = _attention(h, vec(b_pre_norm[j]), b_w_q, vec(kv_norm), w_kv, rope_table,
                                       b_sinks[j], b_w_o, vec(b_post_norm[j]), j,
                                       ffn_w_gate_up, ffn_w_down, l, s)
        h = _ffn(h, vec(ffn_pre_norm[l]), w_gu, w_dn, vec(ffn_post_norm[l]))
    return h.reshape(bsz, s, d)
```

```python
import functools
import math

import jax
import jax.numpy as jnp
import numpy as np
from jax import lax
from jax.experimental import pallas as pl
from jax.experimental.pallas import tpu as pltpu

CONV_W = 3
HEAD_DIM = 64
N_KV_HEADS = 4
WINDOW = 128
BLOCK = 128
ROT_DIM = HEAD_DIM // 4
ROPE_THETA = 500000.0
EPS = 1e-6
NEG = -1e30
LOG2E = math.log2(math.e)

LANES = 128
SUBLANES = 8
BF16_ROWS = 16
ROW_TILE = 512
CONV_SUB_ROWS = 512
CONV_LOOKAHEAD = 1
FFN_TILE = 1024
ATTN_TILE = 1024
ATTN_PIECE = 256
SCORE_LOOKAHEAD = 2
VMEM_LIMIT = 60 * 1024 * 1024

F32 = jnp.float32
BF16 = jnp.bfloat16

assert WINDOW == BLOCK and LANES == 2 * HEAD_DIM


def _rms(x, g):
    r = lax.rsqrt(jnp.mean(x * x, axis=-1, keepdims=True) + EPS)
    return (x * r) * g


def _dot(a, b):
    return jnp.dot(a, b, preferred_element_type=F32)


def _dot_nt(a, b):
    return lax.dot_general(a, b, (((1,), (1,)), ((), ())), preferred_element_type=F32)


def _const_spec(shape):
    return pl.BlockSpec(shape, lambda *_: (0,) * len(shape), pipeline_mode=pl.Buffered(1))


def _layer_spec(shape, layer):
    return pl.BlockSpec((None,) + tuple(shape), lambda *_: (layer,) + (0,) * len(shape),
                        pipeline_mode=pl.Buffered(1))


def _params(semantics):
    return pltpu.CompilerParams(dimension_semantics=semantics, vmem_limit_bytes=VMEM_LIMIT)


def _rope(t, table):
    tb = table.astype(BF16)
    cos_t, sin_up, sin_dn = (tb[:, j * LANES:(j + 1) * LANES] for j in range(3))
    outs = []
    for j in range(t.shape[1] // LANES):
        c = t[:, j * LANES:(j + 1) * LANES].astype(BF16)
        outs.append(c * cos_t
                    + pltpu.roll(c, ROT_DIM // 2, axis=1) * sin_up
                    + pltpu.roll(c, LANES - ROT_DIM // 2, axis=1) * sin_dn)
    return jnp.concatenate(outs, axis=1)


def _rope_tables(seq_len):
    half = ROT_DIM // 2
    f32 = np.float32
    inv_freq = f32(ROPE_THETA) ** (-np.arange(0, ROT_DIM, 2, dtype=f32) / f32(ROT_DIM))
    ang = np.arange(seq_len, dtype=f32)[:, None] * inv_freq[None, :]
    cos, sin = np.cos(ang), np.sin(ang)
    ones = np.ones((seq_len, HEAD_DIM - ROT_DIM), f32)
    zeros = np.zeros((seq_len, HEAD_DIM - ROT_DIM), f32)
    zh = np.zeros((seq_len, half), f32)
    reps = LANES // HEAD_DIM
    cos_t = np.tile(np.concatenate([cos, cos, ones], axis=1), (1, reps))
    sin_up = np.tile(np.concatenate([zh, sin, zeros], axis=1), (1, reps))
    sin_dn = np.tile(np.concatenate([-sin, zh, zeros], axis=1), (1, reps))
    return jnp.asarray(np.concatenate([cos_t, sin_up, sin_dn], axis=1), F32)


def _ffn_weight_cast_specs(w_gate_up, w_down, layer, n_steps):
    _, d, f2 = w_gate_up.shape
    _, f, _ = w_down.shape
    assert d % (n_steps * BF16_ROWS) == 0 and f % (n_steps * BF16_ROWS) == 0
    slab = lambda i: (layer, i, 0)
    in_specs = [pl.BlockSpec((None, d // n_steps, f2), slab),
                pl.BlockSpec((None, f // n_steps, d), slab)]
    out_specs = [pl.BlockSpec((d // n_steps, f2), lambda i: (i, 0)),
                 pl.BlockSpec((f // n_steps, d), lambda i: (i, 0))]
    out_shape = [jax.ShapeDtypeStruct((d, f2), BF16), jax.ShapeDtypeStruct((f, d), BF16)]
    return in_specs, out_specs, out_shape


def _cast_ffn_weights(wgu_src, wd_src, wgu_dst, wd_dst):
    wgu_dst[...] = wgu_src[...].astype(BF16)
    wd_dst[...] = wd_src[...].astype(BF16)


def _conv_mixer_kernel(h_ref, pre_ref, win_ref, cw_ref, wout_ref, post_ref, wgu_src, wd_src,
                       o_ref, wgu_dst, wd_dst, cu_ref, win_bf_ref, wout_bf_ref,
                       *, tiles_per_seq, sub_rows):
    tm, d = h_ref.shape
    i = pl.program_id(0)
    _cast_ffn_weights(wgu_src, wd_src, wgu_dst, wd_dst)

    @pl.when(i == 0)
    def _():
        win_bf_ref[...] = win_ref[...].astype(BF16)
        wout_bf_ref[...] = wout_ref[...].astype(BF16)

    @pl.when(i % tiles_per_seq == 0)
    def _():
        cu_ref[0:SUBLANES, :] = jnp.zeros((SUBLANES, d), F32)

    @pl.when(i % tiles_per_seq != 0)
    def _():
        cu_ref[0:SUBLANES, :] = cu_ref[tm:tm + SUBLANES, :]

    cw = cw_ref[...]

    def gated_conv(r0):
        hn = _rms(h_ref[r0:r0 + sub_rows, :], pre_ref[...]).astype(BF16)
        cu = _dot(hn, win_bf_ref[:, d:2 * d]) * _dot(hn, win_bf_ref[:, 2 * d:3 * d])
        base = SUBLANES + r0
        cu_ref[base:base + sub_rows, :] = cu
        conv = (cu_ref[base - 2:base - 2 + sub_rows, :] * cw[0:1, :]
                + cu_ref[base - 1:base - 1 + sub_rows, :] * cw[1:2, :]
                + cu * cw[2:3, :])
        return (_dot(hn, win_bf_ref[:, 0:d]) * conv).astype(BF16)

    def output(r0, y):
        mix = _dot(y, wout_bf_ref[...])
        o_ref[r0:r0 + sub_rows, :] = h_ref[r0:r0 + sub_rows, :] + _rms(mix, post_ref[...])

    starts = list(range(0, tm, sub_rows))
    pending = [gated_conv(r0) for r0 in starts[:CONV_LOOKAHEAD]]
    for n, r0 in enumerate(starts):
        if n + CONV_LOOKAHEAD < len(starts):
            pending.append(gated_conv(starts[n + CONV_LOOKAHEAD]))
        output(r0, pending.pop(0))


def _conv_mixer(h, pre, w_in, conv_w, w_out, post, ffn_w_gate_up, ffn_w_down, layer, seq_len):
    t, d = h.shape
    tm = 2 * ROW_TILE
    n_steps = t // tm
    cast_in, cast_out, cast_shape = _ffn_weight_cast_specs(ffn_w_gate_up, ffn_w_down, layer, n_steps)
    kern = functools.partial(_conv_mixer_kernel, tiles_per_seq=seq_len // tm,
                             sub_rows=CONV_SUB_ROWS)
    return pl.pallas_call(
        kern,
        grid=(n_steps,),
        in_specs=[
            pl.BlockSpec((tm, d), lambda i: (i, 0)),
            _const_spec((1, d)),
            _layer_spec((d, 3 * d), layer),
            _const_spec((CONV_W, d)),
            _layer_spec((d, d), layer),
            _const_spec((1, d)),
            *cast_in,
        ],
        out_specs=[pl.BlockSpec((tm, d), lambda i: (i, 0)), *cast_out],
        out_shape=[jax.ShapeDtypeStruct((t, d), F32), *cast_shape],
        scratch_shapes=[pltpu.VMEM((tm + SUBLANES, d), F32),
                        pltpu.VMEM((d, 3 * d), BF16), pltpu.VMEM((d, d), BF16)],
        compiler_params=_params(("arbitrary",)),
        name="conv_mixer",
    )(h, pre, w_in, conv_w, w_out, post, ffn_w_gate_up, ffn_w_down)


def _ffn_kernel(h_ref, pre_ref, wgu_ref, wd_ref, post_ref, o_ref, *, chunks, sub_rows):
    f = wd_ref.shape[0]
    for r0 in range(0, h_ref.shape[0], sub_rows):
        h = h_ref[r0:r0 + sub_rows, :]
        hn = _rms(h, pre_ref[...]).astype(BF16)
        acc = None
        for lo, hi in chunks:
            g = _dot(hn, wgu_ref[:, lo:hi])
            u = _dot(hn, wgu_ref[:, f + lo:f + hi])
            a = (g * jax.nn.sigmoid(g) * u).astype(BF16)
            part = _dot(a, wd_ref[lo:hi, :])
            acc = part if acc is None else acc + part
        o_ref[r0:r0 + sub_rows, :] = h + _rms(acc, post_ref[...])


def _ffn_chunks(f):
    step = 768
    return tuple((lo, min(lo + step, f)) for lo in range(0, f, step))


def _ffn(h, pre, w_gate_up, w_down, post):
    t, d = h.shape
    f = w_down.shape[0]
    tm = FFN_TILE
    row = lambda i: (i, 0)
    return pl.pallas_call(
        functools.partial(_ffn_kernel, chunks=_ffn_chunks(f), sub_rows=ROW_TILE),
        grid=(t // tm,),
        in_specs=[
            pl.BlockSpec((tm, d), row),
            _const_spec((1, d)),
            _const_spec((d, 2 * f)),
            _const_spec((f, d)),
            _const_spec((1, d)),
        ],
        out_specs=pl.BlockSpec((tm, d), row),
        out_shape=jax.ShapeDtypeStruct((t, d), F32),
        compiler_params=_params(("parallel",)),
        name="swiglu",
    )(h, pre, w_gate_up, w_down, post)


def _attn_kernel(sink_ref, hp_ref, h_ref, ropep_ref, rope_ref, qn_ref, wq_ref, kvn_ref, wkv_ref,
                 wo_ref, post_ref, wgu_src, wd_src, o_ref, wgu_dst, wd_dst,
                 q_ref, k_ref, vt_ref, att_t_ref, wq_bf_ref, wkv_bf_ref, wo_bf_ref,
                 *, tiles_per_seq):
    _cast_ffn_weights(wgu_src, wd_src, wgu_dst, wd_dst)
    tq = q_ref.shape[0]
    kvd = vt_ref.shape[0]
    n_heads = q_ref.shape[1] // HEAD_DIM
    group = n_heads // N_KV_HEADS
    assert group * HEAD_DIM == 2 * LANES, "one unit = the two q lane chunks of a kv head"
    first_tile = (pl.program_id(0) % tiles_per_seq) == 0

    @pl.when(pl.program_id(0) == 0)
    def _():
        wq_bf_ref[...] = wq_ref[...].astype(BF16)
        wkv_bf_ref[...] = wkv_ref[...].astype(BF16)
        wo_bf_ref[...] = wo_ref[...].astype(BF16)

    lo_half = lax.broadcasted_iota(jnp.int32, (BLOCK, LANES), 1) < HEAD_DIM

    def unit_rms(x):
        return x * lax.rsqrt(jnp.mean(x * x, axis=-1, keepdims=True) + EPS)

    n_windows = tq // BLOCK

    def store_k(block, lanes, kb):
        for j in range(BLOCK // BF16_ROWS):
            rows16 = kb[j * BF16_ROWS:(j + 1) * BF16_ROWS]
            if block >= 1:
                k_ref[block - 1, (2 * j + 1) * BF16_ROWS:(2 * j + 2) * BF16_ROWS, lanes] = rows16
            if block < n_windows:
                k_ref[block, (2 * j) * BF16_ROWS:(2 * j + 1) * BF16_ROWS, lanes] = rows16

    def project_kv(rows_unit, rows_rope, r0):
        n = rows_unit.shape[0]
        kv = _dot((rows_unit * kvn_ref[...]).astype(BF16), wkv_bf_ref[...])
        k = _rope(kv[:, :kvd], rows_rope)
        lo = lax.broadcasted_iota(jnp.int32, (n, LANES), 1) < HEAD_DIM
        for c in range(kvd // LANES):
            pair = k[:, c * LANES:(c + 1) * LANES]
            swapped = pltpu.roll(pair, HEAD_DIM, axis=1)
            for half, dup in enumerate((jnp.where(lo, pair, swapped), jnp.where(lo, swapped, pair))):
                lanes = slice((2 * c + half) * LANES, (2 * c + half + 1) * LANES)
                for t in range(n // BLOCK):
                    store_k(r0 // BLOCK + t, lanes, dup[t * BLOCK:(t + 1) * BLOCK])
        vt_ref[:, r0:r0 + n] = kv[:, kvd:].T.astype(BF16)

    q_gain = qn_ref[...] * (LOG2E / math.sqrt(HEAD_DIM))

    def project(piece):
        rows = slice(piece * ATTN_PIECE, (piece + 1) * ATTN_PIECE)
        if piece == 0:
            project_kv(unit_rms(hp_ref[...]), ropep_ref[...], 0)
        h_unit = unit_rms(h_ref[rows, :])
        project_kv(h_unit, rope_ref[rows, :], BLOCK + piece * ATTN_PIECE)
        q = _dot((h_unit * q_gain).astype(BF16), wq_bf_ref[...])
        q_ref[rows, :] = _rope(q, rope_ref[rows, :])

    def output(piece):
        rows = slice(piece * ATTN_PIECE, (piece + 1) * ATTN_PIECE)
        att = att_t_ref[:, rows].T
        proj = _dot(att, wo_bf_ref[...])
        o_ref[rows, :] = h_ref[rows, :] + _rms(proj, post_ref[...])

    unit_heads = (0, 2, 1, 3)
    width = group * BLOCK
    key_j = lax.broadcasted_iota(jnp.int32, (BLOCK, BLOCK), 0)
    query_i = lax.broadcasted_iota(jnp.int32, (BLOCK, BLOCK), 1)
    from_prev_blk = key_j > query_i

    def by_head(fn, *arrays):
        return jnp.concatenate(
            [fn(from_prev_blk, *(a[:, n * BLOCK:(n + 1) * BLOCK] for a in arrays))
             for n in range(group)], axis=1)
    seg = lax.broadcasted_iota(jnp.int32, (1, width), 1) // BLOCK
    ones_rows = jnp.ones((BF16_ROWS, 2 * BLOCK), BF16)

    prev_bias = jnp.where(first_tile, NEG, 0.0).astype(F32)

    def scores(blk, g):
        r0 = blk * BLOCK
        k_win = k_ref[blk, :, g * LANES:(g + 1) * LANES]
        qc0 = q_ref[r0:r0 + BLOCK, (2 * g) * LANES:(2 * g + 1) * LANES]
        qc1 = q_ref[r0:r0 + BLOCK, (2 * g + 1) * LANES:(2 * g + 2) * LANES]
        zero = jnp.zeros_like(qc0)
        q_split = jnp.concatenate(
            [jnp.where(lo_half, qc0, zero), jnp.where(lo_half, qc1, zero),
             jnp.where(lo_half, zero, qc0), jnp.where(lo_half, zero, qc1)], axis=0)
        return _dot_nt(k_win, q_split)

    def attend(blk, g, s_t):
        r0 = blk * BLOCK
        v_win = vt_ref[g * HEAD_DIM:(g + 1) * HEAD_DIM, r0:r0 + 2 * BLOCK]
        groups = [s_t[j * BF16_ROWS:(j + 1) * BF16_ROWS] for j in range(2 * BLOCK // BF16_ROWS)]
        s_prev = jnp.concatenate(groups[0::2], axis=0)
        s_cur = jnp.concatenate(groups[1::2], axis=0)
        if blk == 0:
            s_prev = s_prev + prev_bias
        s_win = by_head(jnp.where, s_prev, s_cur)
        m = jnp.max(s_win, axis=0, keepdims=True)
        e = jnp.exp2(s_win - m)
        p_t = jnp.concatenate([by_head(lambda mask, x: jnp.where(mask, x, 0.0), e),
                               by_head(lambda mask, x: jnp.where(mask, 0.0, x), e)],
                              axis=0).astype(BF16)
        v_aug = jnp.concatenate([v_win, ones_rows], axis=0)
        o_aug = _dot(v_aug, p_t)
        heads = [g * group + u for u in unit_heads]
        sink = sink_ref[heads[-1]] * LOG2E
        for n in range(group - 2, -1, -1):
            sink = jnp.where(seg == n, sink_ref[heads[n]] * LOG2E, sink)
        inv = 1.0 / (o_aug[HEAD_DIM:HEAD_DIM + 1, :] + jnp.exp2(sink - m))
        o_t = (o_aug[0:HEAD_DIM, :] * inv).astype(BF16)
        for n, head in enumerate(heads):
            att_t_ref[head * HEAD_DIM:(head + 1) * HEAD_DIM, r0:r0 + BLOCK] = (
                o_t[:, n * BLOCK:(n + 1) * BLOCK])

    n_pieces = tq // ATTN_PIECE
    units = [(blk, g) for blk in range(tq // BLOCK) for g in range(N_KV_HEADS)]
    per_piece = len(units) // n_pieces
    assert SCORE_LOOKAHEAD <= per_piece
    project(0)
    if n_pieces > 1:
        project(1)
    pending = [scores(*u) for u in units[:SCORE_LOOKAHEAD]]
    for piece in range(n_pieces):
        for n in range(piece * per_piece, (piece + 1) * per_piece):
            if n + SCORE_LOOKAHEAD < len(units):
                pending.append(scores(*units[n + SCORE_LOOKAHEAD]))
            attend(*units[n], pending.pop(0))
        if piece + 2 < n_pieces:
            project(piece + 2)
        if piece >= 1:
            output(piece - 1)
    output(n_pieces - 1)


def _attention(h, q_norm, w_q, kv_norm, w_kv, rope_table, sinks, w_o, post, layer,
               ffn_w_gate_up, ffn_w_down, ffn_layer, seq_len):
    t, d = h.shape
    qd = w_q.shape[2]
    kvd = w_kv.shape[1] // 2
    tq = ATTN_TILE
    tps = seq_len // tq
    bpt = tq // BLOCK
    row = lambda i: (i, 0)
    prev_blk = lambda i: (jnp.maximum(i * bpt - 1, (i // tps) * tps * bpt), 0)
    prev_pos = lambda i: (jnp.maximum((i % tps) * bpt - 1, 0), 0)
    rope_w = rope_table.shape[1]
    n_steps = t // tq
    cast_in, cast_out, cast_shape = _ffn_weight_cast_specs(ffn_w_gate_up, ffn_w_down, ffn_layer,
                                                           n_steps)
    kern = functools.partial(_attn_kernel, tiles_per_seq=tps)
    return pl.pallas_call(
        kern,
        grid=(n_steps,),
        in_specs=[
            pl.BlockSpec(memory_space=pltpu.SMEM),
            pl.BlockSpec((BLOCK, d), prev_blk),
            pl.BlockSpec((tq, d), row),
            pl.BlockSpec((BLOCK, rope_w), prev_pos),
            pl.BlockSpec((tq, rope_w), lambda i: (i % tps, 0)),
            _const_spec((1, d)),
            _layer_spec((d, qd), layer),
            _const_spec((1, d)),
            _const_spec((d, 2 * kvd)),
            _layer_spec((qd, d), layer),
            _const_spec((1, d)),
            *cast_in,
        ],
        out_specs=[pl.BlockSpec((tq, d), row), *cast_out],
        out_shape=[jax.ShapeDtypeStruct((t, d), F32), *cast_shape],
        scratch_shapes=[
            pltpu.VMEM((tq, qd), BF16),
            pltpu.VMEM((tq // BLOCK, 2 * BLOCK, 2 * kvd), BF16),
            pltpu.VMEM((kvd, tq + BLOCK), BF16),
            pltpu.VMEM((qd, tq), BF16),
            pltpu.VMEM((d, qd), BF16),
            pltpu.VMEM((d, 2 * kvd), BF16),
            pltpu.VMEM((qd, d), BF16),
        ],
        compiler_params=_params(("arbitrary",)),
        name="swa_attention",
    )(sinks, h, h, rope_table, rope_table, q_norm, w_q, kv_norm, w_kv, w_o, post,
      ffn_w_gate_up, ffn_w_down)


def kernel(x, a_pre_norm, a_w_in, a_conv_w, a_w_out, a_post_norm, ffn_pre_norm, ffn_w_gate_up,
           ffn_w_down, ffn_post_norm, kv_norm, w_kv, b_pre_norm, b_w_q, b_sinks, b_w_o,
           b_post_norm):
    bsz, s, d = x.shape
    n_a = a_w_in.shape[0]
    depth = ffn_w_gate_up.shape[0]
    rope_table = _rope_tables(s)
    vec = lambda g: g.reshape(1, -1)

    h = x.reshape(bsz * s, d)
    for l in range(depth):
        if l < n_a:
            h, w_gu, w_dn = _conv_mixer(h, vec(a_pre_norm[l]), a_w_in, a_conv_w[l], a_w_out,
                                        vec(a_post_norm[l]), ffn_w_gate_up, ffn_w_down, l, s)
        else:
            assert l == n_a, "K/V are projected inside the first attention layer's call only"
            j = l - n_a
            h, w_gu, w_dn = _attention(h, vec(b_pre_norm[j]), b_w_q, vec(kv_norm), w_kv, rope_table,
                                       b_sinks[j], b_w_o, vec(b_post_norm[j]), j,
                                       ffn_w_gate_up, ffn_w_down, l, s)
        h = _ffn(h, vec(ffn_pre_norm[l]), w_gu, w_dn, vec(ffn_post_norm[l]))
    return h.reshape(bsz, s, d)
```

```python
import functools
import math

import jax
import jax.numpy as jnp
import numpy as np
from jax import lax
from jax.experimental import pallas as pl
from jax.experimental.pallas import tpu as pltpu

CONV_W = 3
HEAD_DIM = 64
N_KV_HEADS = 4
WINDOW = 128
BLOCK = 128
ROT_DIM = HEAD_DIM // 4
ROPE_THETA = 500000.0
EPS = 1e-6
NEG = -1e30
LOG2E = math.log2(math.e)

LANES = 128
SUBLANES = 8
BF16_ROWS = 16
ROW_TILE = 512
CONV_SUB_ROWS = 512
CONV_LOOKAHEAD = 1
FFN_TILE = 1024
ATTN_TILE = 1024
ATTN_PIECE = 256
SCORE_LOOKAHEAD = 2
VMEM_LIMIT = 60 * 1024 * 1024

F32 = jnp.float32
BF16 = jnp.bfloat16

assert WINDOW == BLOCK and LANES == 2 * HEAD_DIM


def _rms(x, g):
    r = lax.rsqrt(jnp.mean(x * x, axis=-1, keepdims=True) + EPS)
    return (x * r) * g


def _dot(a, b):
    return jnp.dot(a, b, preferred_element_type=F32)


def _dot_nt(a, b):
    return lax.dot_general(a, b, (((1,), (1,)), ((), ())), preferred_element_type=F32)


def _const_spec(shape):
    return pl.BlockSpec(shape, lambda *_: (0,) * len(shape), pipeline_mode=pl.Buffered(1))


def _layer_spec(shape, layer):
    return pl.BlockSpec((None,) + tuple(shape), lambda *_: (layer,) + (0,) * len(shape),
                        pipeline_mode=pl.Buffered(1))


def _params(semantics):
    return pltpu.CompilerParams(dimension_semantics=semantics, vmem_limit_bytes=VMEM_LIMIT)


def _rope(t, table):
    tb = table.astype(BF16)
    cos_t, sin_up, sin_dn = (tb[:, j * LANES:(j + 1) * LANES] for j in range(3))
    outs = []
    for j in range(t.shape[1] // LANES):
        c = t[:, j * LANES:(j + 1) * LANES].astype(BF16)
        outs.append(c * cos_t
                    + pltpu.roll(c, ROT_DIM // 2, axis=1) * sin_up
                    + pltpu.roll(c, LANES - ROT_DIM // 2, axis=1) * sin_dn)
    return jnp.concatenate(outs, axis=1)


def _rope_tables(seq_len):
    half = ROT_DIM // 2
    f32 = np.float32
    inv_freq = f32(ROPE_THETA) ** (-np.arange(0, ROT_DIM, 2, dtype=f32) / f32(ROT_DIM))
    ang = np.arange(seq_len, dtype=f32)[:, None] * inv_freq[None, :]
    cos, sin = np.cos(ang), np.sin(ang)
    ones = np.ones((seq_len, HEAD_DIM - ROT_DIM), f32)
    zeros = np.zeros((seq_len, HEAD_DIM - ROT_DIM), f32)
    zh = np.zeros((seq_len, half), f32)
    reps = LANES // HEAD_DIM
    cos_t = np.tile(np.concatenate([cos, cos, ones], axis=1), (1, reps))
    sin_up = np.tile(np.concatenate([zh, sin, zeros], axis=1), (1, reps))
    sin_dn = np.tile(np.concatenate([-sin, zh, zeros], axis=1), (1, reps))
    return jnp.asarray(np.concatenate([cos_t, sin_up, sin_dn], axis=1), F32)


def _ffn_weight_cast_specs(w_gate_up, w_down, layer, n_steps):
    _, d, f2 = w_gate_up.shape
    _, f, _ = w_down.shape
    assert d % (n_steps * BF16_ROWS) == 0 and f % (n_steps * BF16_ROWS) == 0
    slab = lambda i: (layer, i, 0)
    in_specs = [pl.BlockSpec((None, d // n_steps, f2), slab),
                pl.BlockSpec((None, f // n_steps, d), slab)]
    out_specs = [pl.BlockSpec((d // n_steps, f2), lambda i: (i, 0)),
                 pl.BlockSpec((f // n_steps, d), lambda i: (i, 0))]
    out_shape = [jax.ShapeDtypeStruct((d, f2), BF16), jax.ShapeDtypeStruct((f, d), BF16)]
    return in_specs, out_specs, out_shape


def _cast_ffn_weights(wgu_src, wd_src, wgu_dst, wd_dst):
    wgu_dst[...] = wgu_src[...].astype(BF16)
    wd_dst[...] = wd_src[...].astype(BF16)


def _conv_mixer_kernel(h_ref, pre_ref, win_ref, cw_ref, wout_ref, post_ref, wgu_src, wd_src,
                       o_ref, wgu_dst, wd_dst, cu_ref, win_bf_ref, wout_bf_ref,
                       *, tiles_per_seq, sub_rows):
    tm, d = h_ref.shape
    i = pl.program_id(0)
    _cast_ffn_weights(wgu_src, wd_src, wgu_dst, wd_dst)

    @pl.when(i == 0)
    def _():
        win_bf_ref[...] = win_ref[...].astype(BF16)
        wout_bf_ref[...] = wout_ref[...].astype(BF16)

    @pl.when(i % tiles_per_seq == 0)
    def _():
        cu_ref[...] = jnp.zeros(cu_ref.shape, F32)

    cw = cw_ref[...]

    def shifted(cu, head16, s):
        return jnp.concatenate([head16[SUBLANES - s:2 * SUBLANES - s],
                                pltpu.roll(cu, s, axis=0)[SUBLANES:]], axis=0)

    def gated_conv(r0):
        hn = _rms(h_ref[r0:r0 + sub_rows, :], pre_ref[...]).astype(BF16)
        cu = _dot(hn, win_bf_ref[:, d:2 * d]) * _dot(hn, win_bf_ref[:, 2 * d:3 * d])
        head16 = jnp.concatenate([cu_ref[...], cu[0:SUBLANES]], axis=0)
        cu_ref[...] = cu[sub_rows - SUBLANES:]
        conv = (shifted(cu, head16, 2) * cw[0:1, :] + shifted(cu, head16, 1) * cw[1:2, :]
                + cu * cw[2:3, :])
        return (_dot(hn, win_bf_ref[:, 0:d]) * conv).astype(BF16)

    def output(r0, y):
        mix = _dot(y, wout_bf_ref[...])
        o_ref[r0:r0 + sub_rows, :] = h_ref[r0:r0 + sub_rows, :] + _rms(mix, post_ref[...])

    starts = list(range(0, tm, sub_rows))
    pending = [gated_conv(r0) for r0 in starts[:CONV_LOOKAHEAD]]
    for n, r0 in enumerate(starts):
        if n + CONV_LOOKAHEAD < len(starts):
            pending.append(gated_conv(starts[n + CONV_LOOKAHEAD]))
        output(r0, pending.pop(0))


def _conv_mixer(h, pre, w_in, conv_w, w_out, post, ffn_w_gate_up, ffn_w_down, layer, seq_len):
    t, d = h.shape
    tm = 2 * ROW_TILE
    n_steps = t // tm
    cast_in, cast_out, cast_shape = _ffn_weight_cast_specs(ffn_w_gate_up, ffn_w_down, layer, n_steps)
    kern = functools.partial(_conv_mixer_kernel, tiles_per_seq=seq_len // tm,
                             sub_rows=CONV_SUB_ROWS)
    return pl.pallas_call(
        kern,
        grid=(n_steps,),
        in_specs=[
            pl.BlockSpec((tm, d), lambda i: (i, 0)),
            _const_spec((1, d)),
            _layer_spec((d, 3 * d), layer),
            _const_spec((CONV_W, d)),
            _layer_spec((d, d), layer),
            _const_spec((1, d)),
            *cast_in,
        ],
        out_specs=[pl.BlockSpec((tm, d), lambda i: (i, 0)), *cast_out],
        out_shape=[jax.ShapeDtypeStruct((t, d), F32), *cast_shape],
        scratch_shapes=[pltpu.VMEM((SUBLANES, d), F32),
                        pltpu.VMEM((d, 3 * d), BF16), pltpu.VMEM((d, d), BF16)],
        compiler_params=_params(("arbitrary",)),
        name="conv_mixer",
    )(h, pre, w_in, conv_w, w_out, post, ffn_w_gate_up, ffn_w_down)


def _ffn_kernel(h_ref, pre_ref, wgu_ref, wd_ref, post_ref, o_ref, *, chunks, sub_rows):
    f = wd_ref.shape[0]
    for r0 in range(0, h_ref.shape[0], sub_rows):
        h = h_ref[r0:r0 + sub_rows, :]
        hn = _rms(h, pre_ref[...]).astype(BF16)
        acc = None
        for lo, hi in chunks:
            g = _dot(hn, wgu_ref[:, lo:hi])
            u = _dot(hn, wgu_ref[:, f + lo:f + hi])
            a = (g * jax.nn.sigmoid(g) * u).astype(BF16)
            part = _dot(a, wd_ref[lo:hi, :])
            acc = part if acc is None else acc + part
        o_ref[r0:r0 + sub_rows, :] = h + _rms(acc, post_ref[...])


def _ffn_chunks(f):
    step = 768
    return tuple((lo, min(lo + step, f)) for lo in range(0, f, step))


def _ffn(h, pre, w_gate_up, w_down, post):
    t, d = h.shape
    f = w_down.shape[0]
    tm = FFN_TILE
    row = lambda i: (i, 0)
    return pl.pallas_call(
        functools.partial(_ffn_kernel, chunks=_ffn_chunks(f), sub_rows=ROW_TILE),
        grid=(t // tm,),
        in_specs=[
            pl.BlockSpec((tm, d), row),
            _const_spec((1, d)),
            _const_spec((d, 2 * f)),
            _const_spec((f, d)),
            _const_spec((1, d)),
        ],
        out_specs=pl.BlockSpec((tm, d), row),
        out_shape=jax.ShapeDtypeStruct((t, d), F32),
        compiler_params=_params(("parallel",)),
        name="swiglu",
    )(h, pre, w_gate_up, w_down, post)


def _attn_kernel(sink_ref, hp_ref, h_ref, ropep_ref, rope_ref, qn_ref, wq_ref, kvn_ref, wkv_ref,
                 wo_ref, post_ref, wgu_src, wd_src, o_ref, wgu_dst, wd_dst,
                 q_ref, k_ref, vt_ref, att_t_ref, wq_bf_ref, wkv_bf_ref, wo_bf_ref,
                 *, tiles_per_seq):
    _cast_ffn_weights(wgu_src, wd_src, wgu_dst, wd_dst)
    tq = q_ref.shape[0]
    kvd = vt_ref.shape[0]
    n_heads = q_ref.shape[1] // HEAD_DIM
    group = n_heads // N_KV_HEADS
    assert group * HEAD_DIM == 2 * LANES, "one unit = the two q lane chunks of a kv head"
    first_tile = (pl.program_id(0) % tiles_per_seq) == 0

    @pl.when(pl.program_id(0) == 0)
    def _():
        wq_bf_ref[...] = wq_ref[...].astype(BF16)
        wkv_bf_ref[...] = wkv_ref[...].astype(BF16)
        wo_bf_ref[...] = wo_ref[...].astype(BF16)

    lo_half = lax.broadcasted_iota(jnp.int32, (BLOCK, LANES), 1) < HEAD_DIM

    def unit_rms(x):
        return x * lax.rsqrt(jnp.mean(x * x, axis=-1, keepdims=True) + EPS)

    n_windows = tq // BLOCK

    def store_k(block, lanes, kb):
        for j in range(BLOCK // BF16_ROWS):
            rows16 = kb[j * BF16_ROWS:(j + 1) * BF16_ROWS]
            if block >= 1:
                k_ref[block - 1, (2 * j + 1) * BF16_ROWS:(2 * j + 2) * BF16_ROWS, lanes] = rows16
            if block < n_windows:
                k_ref[block, (2 * j) * BF16_ROWS:(2 * j + 1) * BF16_ROWS, lanes] = rows16

    def project_kv(rows_unit, rows_rope, r0):
        n = rows_unit.shape[0]
        kv = _dot((rows_unit * kvn_ref[...]).astype(BF16), wkv_bf_ref[...])
        k = _rope(kv[:, :kvd], rows_rope)
        lo = lax.broadcasted_iota(jnp.int32, (n, LANES), 1) < HEAD_DIM
        for c in range(kvd // LANES):
            pair = k[:, c * LANES:(c + 1) * LANES]
            swapped = pltpu.roll(pair, HEAD_DIM, axis=1)
            for half, dup in enumerate((jnp.where(lo, pair, swapped), jnp.where(lo, swapped, pair))):
                lanes = slice((2 * c + half) * LANES, (2 * c + half + 1) * LANES)
                for t in range(n // BLOCK):
                    store_k(r0 // BLOCK + t, lanes, dup[t * BLOCK:(t + 1) * BLOCK])
        vt_ref[:, r0:r0 + n] = kv[:, kvd:].T.astype(BF16)

    q_gain = qn_ref[...] * (LOG2E / math.sqrt(HEAD_DIM))

    def project(piece):
        rows = slice(piece * ATTN_PIECE, (piece + 1) * ATTN_PIECE)
        if piece == 0:
            project_kv(unit_rms(hp_ref[...]), ropep_ref[...], 0)
        h_unit = unit_rms(h_ref[rows, :])
        project_kv(h_unit, rope_ref[rows, :], BLOCK + piece * ATTN_PIECE)
        q = _dot((h_unit * q_gain).astype(BF16), wq_bf_ref[...])
        q_ref[rows, :] = _rope(q, rope_ref[rows, :])

    def output(piece):
        rows = slice(piece * ATTN_PIECE, (piece + 1) * ATTN_PIECE)
        att = att_t_ref[:, rows].T
        proj = _dot(att, wo_bf_ref[...])
        o_ref[rows, :] = h_ref[rows, :] + _rms(proj, post_ref[...])

    unit_heads = (0, 2, 1, 3)
    width = group * BLOCK
    key_j = lax.broadcasted_iota(jnp.int32, (BLOCK, BLOCK), 0)
    query_i = lax.broadcasted_iota(jnp.int32, (BLOCK, BLOCK), 1)
    from_prev_blk = key_j > query_i

    def by_head(fn, *arrays):
        return jnp.concatenate(
            [fn(from_prev_blk, *(a[:, n * BLOCK:(n + 1) * BLOCK] for a in arrays))
             for n in range(group)], axis=1)
    seg = lax.broadcasted_iota(jnp.int32, (1, width), 1) // BLOCK
    ones_rows = jnp.ones((BF16_ROWS, 2 * BLOCK), BF16)

    prev_bias = jnp.where(first_tile, NEG, 0.0).astype(F32)

    def scores(blk, g):
        r0 = blk * BLOCK
        k_win = k_ref[blk, :, g * LANES:(g + 1) * LANES]
        qc0 = q_ref[r0:r0 + BLOCK, (2 * g) * LANES:(2 * g + 1) * LANES]
        qc1 = q_ref[r0:r0 + BLOCK, (2 * g + 1) * LANES:(2 * g + 2) * LANES]
        zero = jnp.zeros_like(qc0)
        q_split = jnp.concatenate(
            [jnp.where(lo_half, qc0, zero), jnp.where(lo_half, qc1, zero),
             jnp.where(lo_half, zero, qc0), jnp.where(lo_half, zero, qc1)], axis=0)
        return _dot_nt(k_win, q_split)

    def attend(blk, g, s_t):
        r0 = blk * BLOCK
        v_win = vt_ref[g * HEAD_DIM:(g + 1) * HEAD_DIM, r0:r0 + 2 * BLOCK]
        groups = [s_t[j * BF16_ROWS:(j + 1) * BF16_ROWS] for j in range(2 * BLOCK // BF16_ROWS)]
        s_prev = jnp.concatenate(groups[0::2], axis=0)
        s_cur = jnp.concatenate(groups[1::2], axis=0)
        if blk == 0:
            s_prev = s_prev + prev_bias
        s_win = by_head(jnp.where, s_prev, s_cur)
        m = jnp.max(s_win, axis=0, keepdims=True)
        e = jnp.exp2(s_win - m)
        p_t = jnp.concatenate([by_head(lambda mask, x: jnp.where(mask, x, 0.0), e),
                               by_head(lambda mask, x: jnp.where(mask, 0.0, x), e)],
                              axis=0).astype(BF16)
        v_aug = jnp.concatenate([v_win, ones_rows], axis=0)
        o_aug = _dot(v_aug, p_t)
        heads = [g * group + u for u in unit_heads]
        sink = sink_ref[heads[-1]] * LOG2E
        for n in range(group - 2, -1, -1):
            sink = jnp.where(seg == n, sink_ref[heads[n]] * LOG2E, sink)
        inv = 1.0 / (o_aug[HEAD_DIM:HEAD_DIM + 1, :] + jnp.exp2(sink - m))
        o_t = (o_aug[0:HEAD_DIM, :] * inv).astype(BF16)
        for n, head in enumerate(heads):
            att_t_ref[head * HEAD_DIM:(head + 1) * HEAD_DIM, r0:r0 + BLOCK] = (
                o_t[:, n * BLOCK:(n + 1) * BLOCK])

    n_pieces = tq // ATTN_PIECE
    units = [(blk, g) for blk in range(tq // BLOCK) for g in range(N_KV_HEADS)]
    per_piece = len(units) // n_pieces
    assert SCORE_LOOKAHEAD <= per_piece
    project(0)
    if n_pieces > 1:
        project(1)
    pending = [scores(*u) for u in units[:SCORE_LOOKAHEAD]]
    for piece in range(n_pieces):
        for n in range(piece * per_piece, (piece + 1) * per_piece):
            if n + SCORE_LOOKAHEAD < len(units):
                pending.append(scores(*units[n + SCORE_LOOKAHEAD]))
            attend(*units[n], pending.pop(0))
        if piece + 2 < n_pieces:
            project(piece + 2)
        if piece >= 1:
            output(piece - 1)
    output(n_pieces - 1)


def _attention(h, q_norm, w_q, kv_norm, w_kv, rope_table, sinks, w_o, post, layer,
               ffn_w_gate_up, ffn_w_down, ffn_layer, seq_len):
    t, d = h.shape
    qd = w_q.shape[2]
    kvd = w_kv.shape[1] // 2
    tq = ATTN_TILE
    tps = seq_len // tq
    bpt = tq // BLOCK
    row = lambda i: (i, 0)
    prev_blk = lambda i: (jnp.maximum(i * bpt - 1, (i // tps) * tps * bpt), 0)
    prev_pos = lambda i: (jnp.maximum((i % tps) * bpt - 1, 0), 0)
    rope_w = rope_table.shape[1]
    n_steps = t // tq
    cast_in, cast_out, cast_shape = _ffn_weight_cast_specs(ffn_w_gate_up, ffn_w_down, ffn_layer,
                                                           n_steps)
    kern = functools.partial(_attn_kernel, tiles_per_seq=tps)
    return pl.pallas_call(
        kern,
        grid=(n_steps,),
        in_specs=[
            pl.BlockSpec(memory_space=pltpu.SMEM),
            pl.BlockSpec((BLOCK, d), prev_blk),
            pl.BlockSpec((tq, d), row),
            pl.BlockSpec((BLOCK, rope_w), prev_pos),
            pl.BlockSpec((tq, rope_w), lambda i: (i % tps, 0)),
            _const_spec((1, d)),
            _layer_spec((d, qd), layer),
            _const_spec((1, d)),
            _const_spec((d, 2 * kvd)),
            _layer_spec((qd, d), layer),
            _const_spec((1, d)),
            *cast_in,
        ],
        out_specs=[pl.BlockSpec((tq, d), row), *cast_out],
        out_shape=[jax.ShapeDtypeStruct((t, d), F32), *cast_shape],
        scratch_shapes=[
            pltpu.VMEM((tq, qd), BF16),
            pltpu.VMEM((tq // BLOCK, 2 * BLOCK, 2 * kvd), BF16),
            pltpu.VMEM((kvd, tq + BLOCK), BF16),
            pltpu.VMEM((qd, tq), BF16),
            pltpu.VMEM((d, qd), BF16),
            pltpu.VMEM((d, 2 * kvd), BF16),
            pltpu.VMEM((qd, d), BF16),
        ],
        compiler_params=_params(("arbitrary",)),
        name="swa_attention",
    )(sinks, h, h, rope_table, rope_table, q_norm, w_q, kv_norm, w_kv, w_o, post,
      ffn_w_gate_up, ffn_w_down)


def kernel(x, a_pre_norm, a_w_in, a_conv_w, a_w_out, a_post_norm, ffn_pre_norm, ffn_w_gate_up,
           ffn_w_down, ffn_post_norm, kv_norm, w_kv, b_pre_norm, b_w_q, b_sinks, b_w_o,
           b_post_norm):
    bsz, s, d = x.shape
    n_a = a_w_in.shape[0]
    depth = ffn_w_gate_up.shape[0]
    rope_table = _rope_tables(s)
    vec = lambda g: g.reshape(1, -1)

    h = x.reshape(bsz * s, d)
    for l in range(depth):
        if l < n_a:
            h, w_gu, w_dn = _conv_mixer(h, vec(a_pre_norm[l]), a_w_in, a_conv_w[l], a_w_out,
                                        vec(a_post_norm[l]), ffn_w_gate_up, ffn_w_down, l, s)
        else:
            assert l == n_a, "K/V are projected inside the first attention layer's call only"
            j = l - n_a
            h, w_gu, w_dn = _attention(h, vec(b_pre_norm[j]), b_w_q, vec(kv_norm), w_kv, rope_table,
                                       b_sinks[j], b_w_o, vec(b_post_norm[j]), j,
                                       ffn_w_gate_up, ffn_w_down, l, s)
        h = _ffn(h, vec(ffn_pre_norm[l]), w_gu, w_dn, vec(ffn_post_norm[l]))
    return h.reshape(bsz, s, d)
```

```python
import functools
import math

import jax
import jax.numpy as jnp
import numpy as np
from jax import lax
from jax.experimental import pallas as pl
from jax.experimental.pallas import tpu as pltpu

CONV_W = 3
HEAD_DIM = 64
N_KV_HEADS = 4
WINDOW = 128
BLOCK = 128
ROT_DIM = HEAD_DIM // 4
ROPE_THETA = 500000.0
EPS = 1e-6
NEG = -1e30
LOG2E = math.log2(math.e)

LANES = 128
SUBLANES = 8
BF16_ROWS = 16
ROW_TILE = 512
CONV_TILE = 1024
CONV_SUB_ROWS = 512
CONV_LOOKAHEAD = 1
FFN_TILE = 1024
ATTN_TILE = 1024
ATTN_PIECE = 256
SCORE_LOOKAHEAD = 2
V7X_VMEM_BYTES = 64 * 1024 * 1024
VMEM_LIMIT = V7X_VMEM_BYTES - 4 * 1024 * 1024

F32 = jnp.float32
BF16 = jnp.bfloat16

assert WINDOW == BLOCK and LANES == 2 * HEAD_DIM


def _rms(x, g):
    r = lax.rsqrt(jnp.mean(x * x, axis=-1, keepdims=True) + EPS)
    return (x * r) * g


def _dot(a, b):
    return jnp.dot(a, b, preferred_element_type=F32)


def _dot_nt(a, b):
    return lax.dot_general(a, b, (((1,), (1,)), ((), ())), preferred_element_type=F32)


def _const_spec(shape):
    return pl.BlockSpec(shape, lambda *_: (0,) * len(shape), pipeline_mode=pl.Buffered(1))


def _layer_spec(shape, layer):
    return pl.BlockSpec((None,) + tuple(shape), lambda *_: (layer,) + (0,) * len(shape),
                        pipeline_mode=pl.Buffered(1))


def _params(semantics):
    return pltpu.CompilerParams(dimension_semantics=semantics, vmem_limit_bytes=VMEM_LIMIT)


def _rope(t, table):
    tb = table.astype(BF16)
    cos_t, sin_up, sin_dn = (tb[:, j * LANES:(j + 1) * LANES] for j in range(3))
    outs = []
    for j in range(t.shape[1] // LANES):
        c = t[:, j * LANES:(j + 1) * LANES].astype(BF16)
        outs.append(c * cos_t
                    + pltpu.roll(c, ROT_DIM // 2, axis=1) * sin_up
                    + pltpu.roll(c, LANES - ROT_DIM // 2, axis=1) * sin_dn)
    return jnp.concatenate(outs, axis=1)


def _rope_tables(seq_len):
    half = ROT_DIM // 2
    f32 = np.float32
    inv_freq = f32(ROPE_THETA) ** (-np.arange(0, ROT_DIM, 2, dtype=f32) / f32(ROT_DIM))
    ang = np.arange(seq_len, dtype=f32)[:, None] * inv_freq[None, :]
    cos, sin = np.cos(ang), np.sin(ang)
    ones = np.ones((seq_len, HEAD_DIM - ROT_DIM), f32)
    zeros = np.zeros((seq_len, HEAD_DIM - ROT_DIM), f32)
    zh = np.zeros((seq_len, half), f32)
    reps = LANES // HEAD_DIM
    cos_t = np.tile(np.concatenate([cos, cos, ones], axis=1), (1, reps))
    sin_up = np.tile(np.concatenate([zh, sin, zeros], axis=1), (1, reps))
    sin_dn = np.tile(np.concatenate([-sin, zh, zeros], axis=1), (1, reps))
    return jnp.asarray(np.concatenate([cos_t, sin_up, sin_dn], axis=1), F32)


def _ffn_weight_cast_specs(w_gate_up, w_down, layer, n_steps):
    _, d, f2 = w_gate_up.shape
    _, f, _ = w_down.shape
    assert d % (n_steps * BF16_ROWS) == 0 and f % (n_steps * BF16_ROWS) == 0
    slab = lambda i: (layer, i, 0)
    in_specs = [pl.BlockSpec((None, d // n_steps, f2), slab),
                pl.BlockSpec((None, f // n_steps, d), slab)]
    out_specs = [pl.BlockSpec((d // n_steps, f2), lambda i: (i, 0)),
                 pl.BlockSpec((f // n_steps, d), lambda i: (i, 0))]
    out_shape = [jax.ShapeDtypeStruct((d, f2), BF16), jax.ShapeDtypeStruct((f, d), BF16)]
    return in_specs, out_specs, out_shape


def _cast_ffn_weights(wgu_src, wd_src, wgu_dst, wd_dst):
    wgu_dst[...] = wgu_src[...].astype(BF16)
    wd_dst[...] = wd_src[...].astype(BF16)


def _conv_mixer_kernel(h_ref, pre_ref, win_ref, cw_ref, wout_ref, post_ref, wgu_src, wd_src,
                       o_ref, wgu_dst, wd_dst, cu_ref, win_bf_ref, wout_bf_ref,
                       *, tiles_per_seq, sub_rows):
    tm, d = h_ref.shape
    i = pl.program_id(0)
    _cast_ffn_weights(wgu_src, wd_src, wgu_dst, wd_dst)

    @pl.when(i == 0)
    def _():
        win_bf_ref[...] = win_ref[...].astype(BF16)
        wout_bf_ref[...] = wout_ref[...].astype(BF16)

    @pl.when(i % tiles_per_seq == 0)
    def _():
        cu_ref[0:SUBLANES, :] = jnp.zeros((SUBLANES, d), F32)

    @pl.when(i % tiles_per_seq != 0)
    def _():
        cu_ref[0:SUBLANES, :] = cu_ref[tm:tm + SUBLANES, :]

    cw = cw_ref[...]

    def gated_conv(r0):
        hn = _rms(h_ref[r0:r0 + sub_rows, :], pre_ref[...]).astype(BF16)
        cu = _dot(hn, win_bf_ref[:, d:2 * d]) * _dot(hn, win_bf_ref[:, 2 * d:3 * d])
        base = SUBLANES + r0
        cu_ref[base:base + sub_rows, :] = cu
        conv = (cu_ref[base - 2:base - 2 + sub_rows, :] * cw[0:1, :]
                + cu_ref[base - 1:base - 1 + sub_rows, :] * cw[1:2, :]
                + cu * cw[2:3, :])
        return (_dot(hn, win_bf_ref[:, 0:d]) * conv).astype(BF16)

    def output(r0, y):
        mix = _dot(y, wout_bf_ref[...])
        o_ref[r0:r0 + sub_rows, :] = h_ref[r0:r0 + sub_rows, :] + _rms(mix, post_ref[...])

    starts = list(range(0, tm, sub_rows))
    pending = [gated_conv(r0) for r0 in starts[:CONV_LOOKAHEAD]]
    for n, r0 in enumerate(starts):
        if n + CONV_LOOKAHEAD < len(starts):
            pending.append(gated_conv(starts[n + CONV_LOOKAHEAD]))
        output(r0, pending.pop(0))


def _conv_mixer(h, pre, w_in, conv_w, w_out, post, ffn_w_gate_up, ffn_w_down, layer, seq_len):
    t, d = h.shape
    tm = CONV_TILE
    n_steps = t // tm
    cast_in, cast_out, cast_shape = _ffn_weight_cast_specs(ffn_w_gate_up, ffn_w_down, layer, n_steps)
    kern = functools.partial(_conv_mixer_kernel, tiles_per_seq=seq_len // tm,
                             sub_rows=CONV_SUB_ROWS)
    return pl.pallas_call(
        kern,
        grid=(n_steps,),
        in_specs=[
            pl.BlockSpec((tm, d), lambda i: (i, 0)),
            _const_spec((1, d)),
            _layer_spec((d, 3 * d), layer),
            _const_spec((CONV_W, d)),
            _layer_spec((d, d), layer),
            _const_spec((1, d)),
            *cast_in,
        ],
        out_specs=[pl.BlockSpec((tm, d), lambda i: (i, 0)), *cast_out],
        out_shape=[jax.ShapeDtypeStruct((t, d), F32), *cast_shape],
        scratch_shapes=[pltpu.VMEM((tm + SUBLANES, d), F32),
                        pltpu.VMEM((d, 3 * d), BF16), pltpu.VMEM((d, d), BF16)],
        compiler_params=_params(("arbitrary",)),
        name="conv_mixer",
    )(h, pre, w_in, conv_w, w_out, post, ffn_w_gate_up, ffn_w_down)


def _ffn_kernel(h_ref, pre_ref, wgu_ref, wd_ref, post_ref, o_ref, *, chunks, sub_rows):
    f = wd_ref.shape[0]
    for r0 in range(0, h_ref.shape[0], sub_rows):
        h = h_ref[r0:r0 + sub_rows, :]
        hn = _rms(h, pre_ref[...]).astype(BF16)
        acc = None
        for lo, hi in chunks:
            g = _dot(hn, wgu_ref[:, lo:hi])
            u = _dot(hn, wgu_ref[:, f + lo:f + hi])
            a = (g * jax.nn.sigmoid(g) * u).astype(BF16)
            part = _dot(a, wd_ref[lo:hi, :])
            acc = part if acc is None else acc + part
        o_ref[r0:r0 + sub_rows, :] = h + _rms(acc, post_ref[...])


def _ffn_chunks(f):
    step = 768
    return tuple((lo, min(lo + step, f)) for lo in range(0, f, step))


def _ffn(h, pre, w_gate_up, w_down, post):
    t, d = h.shape
    f = w_down.shape[0]
    tm = FFN_TILE
    row = lambda i: (i, 0)
    return pl.pallas_call(
        functools.partial(_ffn_kernel, chunks=_ffn_chunks(f), sub_rows=ROW_TILE),
        grid=(t // tm,),
        in_specs=[
            pl.BlockSpec((tm, d), row),
            _const_spec((1, d)),
            _const_spec((d, 2 * f)),
            _const_spec((f, d)),
            _const_spec((1, d)),
        ],
        out_specs=pl.BlockSpec((tm, d), row),
        out_shape=jax.ShapeDtypeStruct((t, d), F32),
        compiler_params=_params(("parallel",)),
        name="swiglu",
    )(h, pre, w_gate_up, w_down, post)


def _attn_kernel(sink_ref, h_ref, rope_ref, qn_ref, wq_ref, kvn_ref, wkv_ref,
                 wo_ref, post_ref, wgu_src, wd_src, o_ref, wgu_dst, wd_dst,
                 q_ref, k_ref, vt_ref, att_t_ref, wq_bf_ref, wkv_bf_ref, wo_bf_ref,
                 *, tiles_per_seq):
    _cast_ffn_weights(wgu_src, wd_src, wgu_dst, wd_dst)
    tq = q_ref.shape[0]
    kvd = vt_ref.shape[0]
    n_heads = q_ref.shape[1] // HEAD_DIM
    group = n_heads // N_KV_HEADS
    assert group * HEAD_DIM == 2 * LANES, "one unit = the two q lane chunks of a kv head"
    first_tile = (pl.program_id(0) % tiles_per_seq) == 0

    @pl.when(pl.program_id(0) == 0)
    def _():
        wq_bf_ref[...] = wq_ref[...].astype(BF16)
        wkv_bf_ref[...] = wkv_ref[...].astype(BF16)
        wo_bf_ref[...] = wo_ref[...].astype(BF16)

    lo_half = lax.broadcasted_iota(jnp.int32, (BLOCK, LANES), 1) < HEAD_DIM

    def unit_rms(x):
        return x * lax.rsqrt(jnp.mean(x * x, axis=-1, keepdims=True) + EPS)

    n_windows = tq // BLOCK

    def prev_group(j):
        return slice((2 * j) * BF16_ROWS, (2 * j + 1) * BF16_ROWS)

    def cur_group(j):
        return slice((2 * j + 1) * BF16_ROWS, (2 * j + 2) * BF16_ROWS)

    @pl.when(first_tile)
    def _():
        for j in range(BLOCK // BF16_ROWS):
            k_ref[0, prev_group(j), :] = jnp.zeros((BF16_ROWS, k_ref.shape[2]), BF16)
        vt_ref[:, 0:BLOCK] = jnp.zeros((kvd, BLOCK), BF16)

    @pl.when(jnp.logical_not(first_tile))
    def _():
        for j in range(BLOCK // BF16_ROWS):
            k_ref[0, prev_group(j), :] = k_ref[n_windows - 1, cur_group(j), :]
        vt_ref[:, 0:BLOCK] = vt_ref[:, tq:tq + BLOCK]

    def store_k(block, lanes, kb):
        for j in range(BLOCK // BF16_ROWS):
            rows16 = kb[j * BF16_ROWS:(j + 1) * BF16_ROWS]
            k_ref[block - 1, cur_group(j), lanes] = rows16
            if block < n_windows:
                k_ref[block, prev_group(j), lanes] = rows16

    def project_kv(rows_unit, rows_rope, r0):
        n = rows_unit.shape[0]
        kv = _dot((rows_unit * kvn_ref[...]).astype(BF16), wkv_bf_ref[...])
        k = _rope(kv[:, :kvd], rows_rope)
        lo = lax.broadcasted_iota(jnp.int32, (n, LANES), 1) < HEAD_DIM
        for c in range(kvd // LANES):
            pair = k[:, c * LANES:(c + 1) * LANES]
            swapped = pltpu.roll(pair, HEAD_DIM, axis=1)
            for half, dup in enumerate((jnp.where(lo, pair, swapped), jnp.where(lo, swapped, pair))):
                lanes = slice((2 * c + half) * LANES, (2 * c + half + 1) * LANES)
                for t in range(n // BLOCK):
                    store_k(r0 // BLOCK + t, lanes, dup[t * BLOCK:(t + 1) * BLOCK])
        vt_ref[:, r0:r0 + n] = kv[:, kvd:].T.astype(BF16)

    q_gain = qn_ref[...] * (LOG2E / math.sqrt(HEAD_DIM))

    def project(piece):
        rows = slice(piece * ATTN_PIECE, (piece + 1) * ATTN_PIECE)
        h_unit = unit_rms(h_ref[rows, :])
        project_kv(h_unit, rope_ref[rows, :], BLOCK + piece * ATTN_PIECE)
        q = _dot((h_unit * q_gain).astype(BF16), wq_bf_ref[...])
        q_ref[rows, :] = _rope(q, rope_ref[rows, :])

    def output(piece):
        rows = slice(piece * ATTN_PIECE, (piece + 1) * ATTN_PIECE)
        att = att_t_ref[:, rows].T
        proj = _dot(att, wo_bf_ref[...])
        o_ref[rows, :] = h_ref[rows, :] + _rms(proj, post_ref[...])

    unit_heads = (0, 2, 1, 3)
    width = group * BLOCK
    key_j = lax.broadcasted_iota(jnp.int32, (BLOCK, BLOCK), 0)
    query_i = lax.broadcasted_iota(jnp.int32, (BLOCK, BLOCK), 1)
    from_prev_blk = key_j > query_i

    def by_head(fn, *arrays):
        return jnp.concatenate(
            [fn(from_prev_blk, *(a[:, n * BLOCK:(n + 1) * BLOCK] for a in arrays))
             for n in range(group)], axis=1)
    seg = lax.broadcasted_iota(jnp.int32, (1, width), 1) // BLOCK
    ones_rows = jnp.ones((BF16_ROWS, 2 * BLOCK), BF16)

    prev_bias = jnp.where(first_tile, NEG, 0.0).astype(F32)

    def scores(blk, g):
        r0 = blk * BLOCK
        k_win = k_ref[blk, :, g * LANES:(g + 1) * LANES]
        qc0 = q_ref[r0:r0 + BLOCK, (2 * g) * LANES:(2 * g + 1) * LANES]
        qc1 = q_ref[r0:r0 + BLOCK, (2 * g + 1) * LANES:(2 * g + 2) * LANES]
        zero = jnp.zeros_like(qc0)
        q_split = jnp.concatenate(
            [jnp.where(lo_half, qc0, zero), jnp.where(lo_half, qc1, zero),
             jnp.where(lo_half, zero, qc0), jnp.where(lo_half, zero, qc1)], axis=0)
        return _dot_nt(k_win, q_split)

    def attend(blk, g, s_t):
        r0 = blk * BLOCK
        v_win = vt_ref[g * HEAD_DIM:(g + 1) * HEAD_DIM, r0:r0 + 2 * BLOCK]
        groups = [s_t[j * BF16_ROWS:(j + 1) * BF16_ROWS] for j in range(2 * BLOCK // BF16_ROWS)]
        s_prev = jnp.concatenate(groups[0::2], axis=0)
        s_cur = jnp.concatenate(groups[1::2], axis=0)
        if blk == 0:
            s_prev = s_prev + prev_bias
        s_win = by_head(jnp.where, s_prev, s_cur)
        m = jnp.max(s_win, axis=0, keepdims=True)
        e = jnp.exp2(s_win - m)
        p_t = jnp.concatenate([by_head(lambda mask, x: jnp.where(mask, x, 0.0), e),
                               by_head(lambda mask, x: jnp.where(mask, 0.0, x), e)],
                              axis=0).astype(BF16)
        v_aug = jnp.concatenate([v_win, ones_rows], axis=0)
        o_aug = _dot(v_aug, p_t)
        heads = [g * group + u for u in unit_heads]
        sink = sink_ref[heads[-1]] * LOG2E
        for n in range(group - 2, -1, -1):
            sink = jnp.where(seg == n, sink_ref[heads[n]] * LOG2E, sink)
        inv = 1.0 / (o_aug[HEAD_DIM:HEAD_DIM + 1, :] + jnp.exp2(sink - m))
        o_t = (o_aug[0:HEAD_DIM, :] * inv).astype(BF16)
        for n, head in enumerate(heads):
            att_t_ref[head * HEAD_DIM:(head + 1) * HEAD_DIM, r0:r0 + BLOCK] = (
                o_t[:, n * BLOCK:(n + 1) * BLOCK])

    n_pieces = tq // ATTN_PIECE
    units = [(blk, g) for blk in range(tq // BLOCK) for g in range(N_KV_HEADS)]
    per_piece = len(units) // n_pieces
    assert SCORE_LOOKAHEAD <= per_piece
    project(0)
    if n_pieces > 1:
        project(1)
    pending = [scores(*u) for u in units[:SCORE_LOOKAHEAD]]
    for piece in range(n_pieces):
        for n in range(piece * per_piece, (piece + 1) * per_piece):
            if n + SCORE_LOOKAHEAD < len(units):
                pending.append(scores(*units[n + SCORE_LOOKAHEAD]))
            attend(*units[n], pending.pop(0))
        if piece + 2 < n_pieces:
            project(piece + 2)
        if piece >= 1:
            output(piece - 1)
    output(n_pieces - 1)


def _attention(h, q_norm, w_q, kv_norm, w_kv, rope_table, sinks, w_o, post, layer,
               ffn_w_gate_up, ffn_w_down, ffn_layer, seq_len):
    t, d = h.shape
    qd = w_q.shape[2]
    kvd = w_kv.shape[1] // 2
    tq = ATTN_TILE
    tps = seq_len // tq
    row = lambda i: (i, 0)
    n_steps = t // tq
    cast_in, cast_out, cast_shape = _ffn_weight_cast_specs(ffn_w_gate_up, ffn_w_down, ffn_layer,
                                                           n_steps)
    kern = functools.partial(_attn_kernel, tiles_per_seq=tps)
    return pl.pallas_call(
        kern,
        grid=(n_steps,),
        in_specs=[
            pl.BlockSpec(memory_space=pltpu.SMEM),
            pl.BlockSpec((tq, d), row),
            pl.BlockSpec((tq, rope_table.shape[1]), lambda i: (i % tps, 0)),
            _const_spec((1, d)),
            _layer_spec((d, qd), layer),
            _const_spec((1, d)),
            _const_spec((d, 2 * kvd)),
            _layer_spec((qd, d), layer),
            _const_spec((1, d)),
            *cast_in,
        ],
        out_specs=[pl.BlockSpec((tq, d), row), *cast_out],
        out_shape=[jax.ShapeDtypeStruct((t, d), F32), *cast_shape],
        scratch_shapes=[
            pltpu.VMEM((tq, qd), BF16),
            pltpu.VMEM((tq // BLOCK, 2 * BLOCK, 2 * kvd), BF16),
            pltpu.VMEM((kvd, tq + BLOCK), BF16),
            pltpu.VMEM((qd, tq), BF16),
            pltpu.VMEM((d, qd), BF16),
            pltpu.VMEM((d, 2 * kvd), BF16),
            pltpu.VMEM((qd, d), BF16),
        ],
        compiler_params=_params(("arbitrary",)),
        name="swa_attention",
    )(sinks, h, rope_table, q_norm, w_q, kv_norm, w_kv, w_o, post, ffn_w_gate_up, ffn_w_down)


def kernel(x, a_pre_norm, a_w_in, a_conv_w, a_w_out, a_post_norm, ffn_pre_norm, ffn_w_gate_up,
           ffn_w_down, ffn_post_norm, kv_norm, w_kv, b_pre_norm, b_w_q, b_sinks, b_w_o,
           b_post_norm):
    bsz, s, d = x.shape
    n_a = a_w_in.shape[0]
    depth = ffn_w_gate_up.shape[0]
    rope_table = _rope_tables(s)
    vec = lambda g: g.reshape(1, -1)

    h = x.reshape(bsz * s, d)
    for l in range(depth):
        if l < n_a:
            h, w_gu, w_dn = _conv_mixer(h, vec(a_pre_norm[l]), a_w_in, a_conv_w[l], a_w_out,
                                        vec(a_post_norm[l]), ffn_w_gate_up, ffn_w_down, l, s)
        else:
            assert l == n_a, "K/V are projected inside the first attention layer's call only"
            j = l - n_a
            h, w_gu, w_dn = _attention(h, vec(b_pre_norm[j]), b_w_q, vec(kv_norm), w_kv, rope_table,
                                       b_sinks[j], b_w_o, vec(b_post_norm[j]), j,
                                       ffn_w_gate_up, ffn_w_down, l, s)
        h = _ffn(h, vec(ffn_pre_norm[l]), w_gu, w_dn, vec(ffn_post_norm[l]))
    return h.reshape(bsz, s, d)
```

```python
import functools
import math

import jax
import jax.numpy as jnp
import numpy as np
from jax import lax
from jax.experimental import pallas as pl
from jax.experimental.pallas import tpu as pltpu

CONV_W = 3
HEAD_DIM = 64
N_KV_HEADS = 4
WINDOW = 128
BLOCK = 128
ROT_DIM = HEAD_DIM // 4
ROPE_THETA = 500000.0
EPS = 1e-6
NEG = -1e30
LOG2E = math.log2(math.e)

LANES = 128
SUBLANES = 8
BF16_ROWS = 16
ROW_TILE = 512
CONV_TILE = 1024
CONV_SUB_ROWS = 512
CONV_LOOKAHEAD = 1
FFN_TILE = 1024
ATTN_TILE = 1024
ATTN_PIECE = 256
SCORE_LOOKAHEAD = 2
V7X_VMEM_BYTES = 64 * 1024 * 1024
VMEM_LIMIT = V7X_VMEM_BYTES - 4 * 1024 * 1024

F32 = jnp.float32
BF16 = jnp.bfloat16

assert WINDOW == BLOCK and LANES == 2 * HEAD_DIM


def _rms(x, g):
    r = lax.rsqrt(jnp.mean(x * x, axis=-1, keepdims=True) + EPS)
    return (x * r) * g


def _dot(a, b):
    return jnp.dot(a, b, preferred_element_type=F32)


def _dot_nt(a, b):
    return lax.dot_general(a, b, (((1,), (1,)), ((), ())), preferred_element_type=F32)


def _const_spec(shape):
    return pl.BlockSpec(shape, lambda *_: (0,) * len(shape), pipeline_mode=pl.Buffered(1))


def _layer_spec(shape, layer):
    return pl.BlockSpec((None,) + tuple(shape), lambda *_: (layer,) + (0,) * len(shape),
                        pipeline_mode=pl.Buffered(1))


def _params(semantics):
    return pltpu.CompilerParams(dimension_semantics=semantics, vmem_limit_bytes=VMEM_LIMIT)


def _rope(t, table):
    tb = table.astype(BF16)
    cos_t, sin_up, sin_dn = (tb[:, j * LANES:(j + 1) * LANES] for j in range(3))
    outs = []
    for j in range(t.shape[1] // LANES):
        c = t[:, j * LANES:(j + 1) * LANES].astype(BF16)
        outs.append(c * cos_t
                    + pltpu.roll(c, ROT_DIM // 2, axis=1) * sin_up
                    + pltpu.roll(c, LANES - ROT_DIM // 2, axis=1) * sin_dn)
    return jnp.concatenate(outs, axis=1)


def _rope_tables(seq_len):
    half = ROT_DIM // 2
    f32 = np.float32
    inv_freq = f32(ROPE_THETA) ** (-np.arange(0, ROT_DIM, 2, dtype=f32) / f32(ROT_DIM))
    ang = np.arange(seq_len, dtype=f32)[:, None] * inv_freq[None, :]
    cos, sin = np.cos(ang), np.sin(ang)
    ones = np.ones((seq_len, HEAD_DIM - ROT_DIM), f32)
    zeros = np.zeros((seq_len, HEAD_DIM - ROT_DIM), f32)
    zh = np.zeros((seq_len, half), f32)
    reps = LANES // HEAD_DIM
    cos_t = np.tile(np.concatenate([cos, cos, ones], axis=1), (1, reps))
    sin_up = np.tile(np.concatenate([zh, sin, zeros], axis=1), (1, reps))
    sin_dn = np.tile(np.concatenate([-sin, zh, zeros], axis=1), (1, reps))
    return jnp.asarray(np.concatenate([cos_t, sin_up, sin_dn], axis=1), F32)


def _ffn_weight_cast_specs(w_gate_up, w_down, layer, n_steps):
    _, d, f2 = w_gate_up.shape
    _, f, _ = w_down.shape
    assert d % (n_steps * BF16_ROWS) == 0 and f % (n_steps * BF16_ROWS) == 0
    slab = lambda i: (layer, i, 0)
    in_specs = [pl.BlockSpec((None, d // n_steps, f2), slab),
                pl.BlockSpec((None, f // n_steps, d), slab)]
    out_specs = [pl.BlockSpec((d // n_steps, f2), lambda i: (i, 0)),
                 pl.BlockSpec((f // n_steps, d), lambda i: (i, 0))]
    out_shape = [jax.ShapeDtypeStruct((d, f2), BF16), jax.ShapeDtypeStruct((f, d), BF16)]
    return in_specs, out_specs, out_shape


def _cast_ffn_weights(wgu_src, wd_src, wgu_dst, wd_dst):
    wgu_dst[...] = wgu_src[...].astype(BF16)
    wd_dst[...] = wd_src[...].astype(BF16)


def _conv_mixer_kernel(h_ref, pre_ref, win_ref, cw_ref, wout_ref, post_ref, wgu_src, wd_src,
                       o_ref, wgu_dst, wd_dst, cu_ref, win_bf_ref, wout_bf_ref,
                       *, tiles_per_seq, sub_rows):
    tm, d = h_ref.shape
    i = pl.program_id(0)
    _cast_ffn_weights(wgu_src, wd_src, wgu_dst, wd_dst)

    @pl.when(i == 0)
    def _():
        win_bf_ref[...] = win_ref[...].astype(BF16)
        wout_bf_ref[...] = wout_ref[...].astype(BF16)

    @pl.when(i % tiles_per_seq == 0)
    def _():
        cu_ref[0:SUBLANES, :] = jnp.zeros((SUBLANES, d), F32)

    @pl.when(i % tiles_per_seq != 0)
    def _():
        cu_ref[0:SUBLANES, :] = cu_ref[tm:tm + SUBLANES, :]

    cw = cw_ref[...]

    def gated_conv(r0):
        hn = _rms(h_ref[r0:r0 + sub_rows, :], pre_ref[...]).astype(BF16)
        cu = _dot(hn, win_bf_ref[:, d:2 * d]) * _dot(hn, win_bf_ref[:, 2 * d:3 * d])
        base = SUBLANES + r0
        cu_ref[base:base + sub_rows, :] = cu
        conv = (cu_ref[base - 2:base - 2 + sub_rows, :] * cw[0:1, :]
                + cu_ref[base - 1:base - 1 + sub_rows, :] * cw[1:2, :]
                + cu * cw[2:3, :])
        return (_dot(hn, win_bf_ref[:, 0:d]) * conv).astype(BF16)

    def output(r0, y):
        mix = _dot(y, wout_bf_ref[...])
        o_ref[r0:r0 + sub_rows, :] = h_ref[r0:r0 + sub_rows, :] + _rms(mix, post_ref[...])

    starts = list(range(0, tm, sub_rows))
    pending = [gated_conv(r0) for r0 in starts[:CONV_LOOKAHEAD]]
    for n, r0 in enumerate(starts):
        if n + CONV_LOOKAHEAD < len(starts):
            pending.append(gated_conv(starts[n + CONV_LOOKAHEAD]))
        output(r0, pending.pop(0))


def _conv_mixer(h, pre, w_in, conv_w, w_out, post, ffn_w_gate_up, ffn_w_down, layer, seq_len):
    t, d = h.shape
    tm = CONV_TILE
    n_steps = t // tm
    cast_in, cast_out, cast_shape = _ffn_weight_cast_specs(ffn_w_gate_up, ffn_w_down, layer, n_steps)
    kern = functools.partial(_conv_mixer_kernel, tiles_per_seq=seq_len // tm,
                             sub_rows=CONV_SUB_ROWS)
    return pl.pallas_call(
        kern,
        grid=(n_steps,),
        in_specs=[
            pl.BlockSpec((tm, d), lambda i: (i, 0)),
            _const_spec((1, d)),
            _layer_spec((d, 3 * d), layer),
            _const_spec((CONV_W, d)),
            _layer_spec((d, d), layer),
            _const_spec((1, d)),
            *cast_in,
        ],
        out_specs=[pl.BlockSpec((tm, d), lambda i: (i, 0)), *cast_out],
        out_shape=[jax.ShapeDtypeStruct((t, d), F32), *cast_shape],
        scratch_shapes=[pltpu.VMEM((tm + SUBLANES, d), F32),
                        pltpu.VMEM((d, 3 * d), BF16), pltpu.VMEM((d, d), BF16)],
        compiler_params=_params(("arbitrary",)),
        name="conv_mixer",
    )(h, pre, w_in, conv_w, w_out, post, ffn_w_gate_up, ffn_w_down)


def _ffn_kernel(h_ref, pre_ref, wgu_ref, wd_ref, post_ref, o_ref, *, chunks, sub_rows):
    f = wd_ref.shape[0]
    starts = list(range(0, h_ref.shape[0], sub_rows))
    hs = [h_ref[r0:r0 + sub_rows, :] for r0 in starts]
    hns = [_rms(h, pre_ref[...]).astype(BF16) for h in hs]
    accs = [None] * len(starts)
    for lo, hi in chunks:
        for n, hn in enumerate(hns):
            g = _dot(hn, wgu_ref[:, lo:hi])
            u = _dot(hn, wgu_ref[:, f + lo:f + hi])
            a = (g * jax.nn.sigmoid(g) * u).astype(BF16)
            part = _dot(a, wd_ref[lo:hi, :])
            accs[n] = part if accs[n] is None else accs[n] + part
    for r0, h, acc in zip(starts, hs, accs):
        o_ref[r0:r0 + sub_rows, :] = h + _rms(acc, post_ref[...])


def _ffn_chunks(f):
    step = 768
    return tuple((lo, min(lo + step, f)) for lo in range(0, f, step))


def _ffn(h, pre, w_gate_up, w_down, post):
    t, d = h.shape
    f = w_down.shape[0]
    tm = FFN_TILE
    row = lambda i: (i, 0)
    return pl.pallas_call(
        functools.partial(_ffn_kernel, chunks=_ffn_chunks(f), sub_rows=ROW_TILE),
        grid=(t // tm,),
        in_specs=[
            pl.BlockSpec((tm, d), row),
            _const_spec((1, d)),
            _const_spec((d, 2 * f)),
            _const_spec((f, d)),
            _const_spec((1, d)),
        ],
        out_specs=pl.BlockSpec((tm, d), row),
        out_shape=jax.ShapeDtypeStruct((t, d), F32),
        compiler_params=_params(("parallel",)),
        name="swiglu",
    )(h, pre, w_gate_up, w_down, post)


def _attn_kernel(sink_ref, h_ref, rope_ref, qn_ref, wq_ref, kvn_ref, wkv_ref,
                 wo_ref, post_ref, wgu_src, wd_src, o_ref, wgu_dst, wd_dst,
                 q_ref, k_ref, vt_ref, att_t_ref, wq_bf_ref, wkv_bf_ref, wo_bf_ref,
                 *, tiles_per_seq):
    _cast_ffn_weights(wgu_src, wd_src, wgu_dst, wd_dst)
    tq = q_ref.shape[0]
    kvd = vt_ref.shape[0]
    n_heads = q_ref.shape[1] // HEAD_DIM
    group = n_heads // N_KV_HEADS
    assert group * HEAD_DIM == 2 * LANES, "one unit = the two q lane chunks of a kv head"
    first_tile = (pl.program_id(0) % tiles_per_seq) == 0

    @pl.when(pl.program_id(0) == 0)
    def _():
        wq_bf_ref[...] = wq_ref[...].astype(BF16)
        wkv_bf_ref[...] = wkv_ref[...].astype(BF16)
        wo_bf_ref[...] = wo_ref[...].astype(BF16)

    lo_half = lax.broadcasted_iota(jnp.int32, (BLOCK, LANES), 1) < HEAD_DIM

    def unit_rms(x):
        return x * lax.rsqrt(jnp.mean(x * x, axis=-1, keepdims=True) + EPS)

    n_windows = tq // BLOCK

    def prev_group(j):
        return slice((2 * j) * BF16_ROWS, (2 * j + 1) * BF16_ROWS)

    def cur_group(j):
        return slice((2 * j + 1) * BF16_ROWS, (2 * j + 2) * BF16_ROWS)

    @pl.when(first_tile)
    def _():
        for j in range(BLOCK // BF16_ROWS):
            k_ref[0, prev_group(j), :] = jnp.zeros((BF16_ROWS, k_ref.shape[2]), BF16)
        vt_ref[:, 0:BLOCK] = jnp.zeros((kvd, BLOCK), BF16)

    @pl.when(jnp.logical_not(first_tile))
    def _():
        for j in range(BLOCK // BF16_ROWS):
            k_ref[0, prev_group(j), :] = k_ref[n_windows - 1, cur_group(j), :]
        vt_ref[:, 0:BLOCK] = vt_ref[:, tq:tq + BLOCK]

    def store_k(block, lanes, kb):
        for j in range(BLOCK // BF16_ROWS):
            rows16 = kb[j * BF16_ROWS:(j + 1) * BF16_ROWS]
            k_ref[block - 1, cur_group(j), lanes] = rows16
            if block < n_windows:
                k_ref[block, prev_group(j), lanes] = rows16

    def project_kv(rows_unit, rows_rope, r0):
        n = rows_unit.shape[0]
        kv = _dot((rows_unit * kvn_ref[...]).astype(BF16), wkv_bf_ref[...])
        k = _rope(kv[:, :kvd], rows_rope)
        lo = lax.broadcasted_iota(jnp.int32, (n, LANES), 1) < HEAD_DIM
        for c in range(kvd // LANES):
            pair = k[:, c * LANES:(c + 1) * LANES]
            swapped = pltpu.roll(pair, HEAD_DIM, axis=1)
            for half, dup in enumerate((jnp.where(lo, pair, swapped), jnp.where(lo, swapped, pair))):
                lanes = slice((2 * c + half) * LANES, (2 * c + half + 1) * LANES)
                for t in range(n // BLOCK):
                    store_k(r0 // BLOCK + t, lanes, dup[t * BLOCK:(t + 1) * BLOCK])
        vt_ref[:, r0:r0 + n] = kv[:, kvd:].T.astype(BF16)

    q_gain = qn_ref[...] * (LOG2E / math.sqrt(HEAD_DIM))

    def project(piece):
        rows = slice(piece * ATTN_PIECE, (piece + 1) * ATTN_PIECE)
        h_unit = unit_rms(h_ref[rows, :])
        project_kv(h_unit, rope_ref[rows, :], BLOCK + piece * ATTN_PIECE)
        q = _dot((h_unit * q_gain).astype(BF16), wq_bf_ref[...])
        q_ref[rows, :] = _rope(q, rope_ref[rows, :])

    def output(piece):
        rows = slice(piece * ATTN_PIECE, (piece + 1) * ATTN_PIECE)
        att = att_t_ref[:, rows].T
        proj = _dot(att, wo_bf_ref[...])
        o_ref[rows, :] = h_ref[rows, :] + _rms(proj, post_ref[...])

    unit_heads = (0, 2, 1, 3)
    width = group * BLOCK
    key_j = lax.broadcasted_iota(jnp.int32, (BLOCK, BLOCK), 0)
    query_i = lax.broadcasted_iota(jnp.int32, (BLOCK, BLOCK), 1)
    from_prev_blk = key_j > query_i

    def by_head(fn, *arrays):
        return jnp.concatenate(
            [fn(from_prev_blk, *(a[:, n * BLOCK:(n + 1) * BLOCK] for a in arrays))
             for n in range(group)], axis=1)
    seg = lax.broadcasted_iota(jnp.int32, (1, width), 1) // BLOCK
    ones_rows = jnp.ones((BF16_ROWS, 2 * BLOCK), BF16)

    prev_bias = jnp.where(first_tile, NEG, 0.0).astype(F32)

    def scores(blk, g):
        r0 = blk * BLOCK
        k_win = k_ref[blk, :, g * LANES:(g + 1) * LANES]
        qc0 = q_ref[r0:r0 + BLOCK, (2 * g) * LANES:(2 * g + 1) * LANES]
        qc1 = q_ref[r0:r0 + BLOCK, (2 * g + 1) * LANES:(2 * g + 2) * LANES]
        zero = jnp.zeros_like(qc0)
        q_split = jnp.concatenate(
            [jnp.where(lo_half, qc0, zero), jnp.where(lo_half, qc1, zero),
             jnp.where(lo_half, zero, qc0), jnp.where(lo_half, zero, qc1)], axis=0)
        return _dot_nt(k_win, q_split)

    def attend(blk, g, s_t):
        r0 = blk * BLOCK
        v_win = vt_ref[g * HEAD_DIM:(g + 1) * HEAD_DIM, r0:r0 + 2 * BLOCK]
        groups = [s_t[j * BF16_ROWS:(j + 1) * BF16_ROWS] for j in range(2 * BLOCK // BF16_ROWS)]
        s_prev = jnp.concatenate(groups[0::2], axis=0)
        s_cur = jnp.concatenate(groups[1::2], axis=0)
        if blk == 0:
            s_prev = s_prev + prev_bias
        s_win = by_head(jnp.where, s_prev, s_cur)
        m = jnp.max(s_win, axis=0, keepdims=True)
        e = jnp.exp2(s_win - m)
        p_t = jnp.concatenate([by_head(lambda mask, x: jnp.where(mask, x, 0.0), e),
                               by_head(lambda mask, x: jnp.where(mask, 0.0, x), e)],
                              axis=0).astype(BF16)
        v_aug = jnp.concatenate([v_win, ones_rows], axis=0)
        o_aug = _dot(v_aug, p_t)
        heads = [g * group + u for u in unit_heads]
        sink = sink_ref[heads[-1]] * LOG2E
        for n in range(group - 2, -1, -1):
            sink = jnp.where(seg == n, sink_ref[heads[n]] * LOG2E, sink)
        inv = 1.0 / (o_aug[HEAD_DIM:HEAD_DIM + 1, :] + jnp.exp2(sink - m))
        o_t = (o_aug[0:HEAD_DIM, :] * inv).astype(BF16)
        for n, head in enumerate(heads):
            att_t_ref[head * HEAD_DIM:(head + 1) * HEAD_DIM, r0:r0 + BLOCK] = (
                o_t[:, n * BLOCK:(n + 1) * BLOCK])

    n_pieces = tq // ATTN_PIECE
    units = [(blk, g) for blk in range(tq // BLOCK) for g in range(N_KV_HEADS)]
    per_piece = len(units) // n_pieces
    assert SCORE_LOOKAHEAD <= per_piece
    project(0)
    if n_pieces > 1:
        project(1)
    pending = [scores(*u) for u in units[:SCORE_LOOKAHEAD]]
    for piece in range(n_pieces):
        for n in range(piece * per_piece, (piece + 1) * per_piece):
            if n + SCORE_LOOKAHEAD < len(units):
                pending.append(scores(*units[n + SCORE_LOOKAHEAD]))
            attend(*units[n], pending.pop(0))
        if piece + 2 < n_pieces:
            project(piece + 2)
        if piece >= 1:
            output(piece - 1)
    output(n_pieces - 1)


def _attention(h, q_norm, w_q, kv_norm, w_kv, rope_table, sinks, w_o, post, layer,
               ffn_w_gate_up, ffn_w_down, ffn_layer, seq_len):
    t, d = h.shape
    qd = w_q.shape[2]
    kvd = w_kv.shape[1] // 2
    tq = ATTN_TILE
    tps = seq_len // tq
    row = lambda i: (i, 0)
    n_steps = t // tq
    cast_in, cast_out, cast_shape = _ffn_weight_cast_specs(ffn_w_gate_up, ffn_w_down, ffn_layer,
                                                           n_steps)
    kern = functools.partial(_attn_kernel, tiles_per_seq=tps)
    return pl.pallas_call(
        kern,
        grid=(n_steps,),
        in_specs=[
            pl.BlockSpec(memory_space=pltpu.SMEM),
            pl.BlockSpec((tq, d), row),
            pl.BlockSpec((tq, rope_table.shape[1]), lambda i: (i % tps, 0)),
            _const_spec((1, d)),
            _layer_spec((d, qd), layer),
            _const_spec((1, d)),
            _const_spec((d, 2 * kvd)),
            _layer_spec((qd, d), layer),
            _const_spec((1, d)),
            *cast_in,
        ],
        out_specs=[pl.BlockSpec((tq, d), row), *cast_out],
        out_shape=[jax.ShapeDtypeStruct((t, d), F32), *cast_shape],
        scratch_shapes=[
            pltpu.VMEM((tq, qd), BF16),
            pltpu.VMEM((tq // BLOCK, 2 * BLOCK, 2 * kvd), BF16),
            pltpu.VMEM((kvd, tq + BLOCK), BF16),
            pltpu.VMEM((qd, tq), BF16),
            pltpu.VMEM((d, qd), BF16),
            pltpu.VMEM((d, 2 * kvd), BF16),
            pltpu.VMEM((qd, d), BF16),
        ],
        compiler_params=_params(("arbitrary",)),
        name="swa_attention",
    )(sinks, h, rope_table, q_norm, w_q, kv_norm, w_kv, w_o, post, ffn_w_gate_up, ffn_w_down)


def kernel(x, a_pre_norm, a_w_in, a_conv_w, a_w_out, a_post_norm, ffn_pre_norm, ffn_w_gate_up,
           ffn_w_down, ffn_post_norm, kv_norm, w_kv, b_pre_norm, b_w_q, b_sinks, b_w_o,
           b_post_norm):
    bsz, s, d = x.shape
    n_a = a_w_in.shape[0]
    depth = ffn_w_gate_up.shape[0]
    rope_table = _rope_tables(s)
    vec = lambda g: g.reshape(1, -1)

    h = x.reshape(bsz * s, d)
    for l in range(depth):
        if l < n_a:
            h, w_gu, w_dn = _conv_mixer(h, vec(a_pre_norm[l]), a_w_in, a_conv_w[l], a_w_out,
                                        vec(a_post_norm[l]), ffn_w_gate_up, ffn_w_down, l, s)
        else:
            assert l == n_a, "K/V are projected inside the first attention layer's call only"
            j = l - n_a
            h, w_gu, w_dn = _attention(h, vec(b_pre_norm[j]), b_w_q, vec(kv_norm), w_kv, rope_table,
                                       b_sinks[j], b_w_o, vec(b_post_norm[j]), j,
                                       ffn_w_gate_up, ffn_w_down, l, s)
        h = _ffn(h, vec(ffn_pre_norm[l]), w_gu, w_dn, vec(ffn_post_norm[l]))
    return h.reshape(bsz, s, d)
```

```python
import functools
import math

import jax
import jax.numpy as jnp
import numpy as np
from jax import lax
from jax.experimental import pallas as pl
from jax.experimental.pallas import tpu as pltpu

CONV_W = 3
HEAD_DIM = 64
N_KV_HEADS = 4
WINDOW = 128
BLOCK = 128
ROT_DIM = HEAD_DIM // 4
ROPE_THETA = 500000.0
EPS = 1e-6
NEG = -1e30
LOG2E = math.log2(math.e)

LANES = 128
SUBLANES = 8
BF16_ROWS = 16
ROW_TILE = 512
CONV_TILE = 1024
CONV_SUB_ROWS = 512
CONV_LOOKAHEAD = 1
FFN_TILE = 1024
ATTN_TILE = 1024
ATTN_PIECE = 256
SCORE_LOOKAHEAD = 2
V7X_VMEM_BYTES = 64 * 1024 * 1024
VMEM_LIMIT = V7X_VMEM_BYTES - 4 * 1024 * 1024

F32 = jnp.float32
BF16 = jnp.bfloat16

assert WINDOW == BLOCK and LANES == 2 * HEAD_DIM


def _rms(x, g):
    r = lax.rsqrt(jnp.mean(x * x, axis=-1, keepdims=True) + EPS)
    return (x * r) * g


def _dot(a, b):
    return jnp.dot(a, b, preferred_element_type=F32)


def _dot_nt(a, b):
    return lax.dot_general(a, b, (((1,), (1,)), ((), ())), preferred_element_type=F32)


def _const_spec(shape):
    return pl.BlockSpec(shape, lambda *_: (0,) * len(shape), pipeline_mode=pl.Buffered(1))


def _layer_spec(shape, layer):
    return pl.BlockSpec((None,) + tuple(shape), lambda *_: (layer,) + (0,) * len(shape),
                        pipeline_mode=pl.Buffered(1))


def _params(semantics):
    return pltpu.CompilerParams(dimension_semantics=semantics, vmem_limit_bytes=VMEM_LIMIT)


def _rope(t, table):
    tb = table.astype(BF16)
    cos_t, sin_up, sin_dn = (tb[:, j * LANES:(j + 1) * LANES] for j in range(3))
    outs = []
    for j in range(t.shape[1] // LANES):
        c = t[:, j * LANES:(j + 1) * LANES].astype(BF16)
        outs.append(c * cos_t
                    + pltpu.roll(c, ROT_DIM // 2, axis=1) * sin_up
                    + pltpu.roll(c, LANES - ROT_DIM // 2, axis=1) * sin_dn)
    return jnp.concatenate(outs, axis=1)


def _rope_tables(seq_len):
    half = ROT_DIM // 2
    f32 = np.float32
    inv_freq = f32(ROPE_THETA) ** (-np.arange(0, ROT_DIM, 2, dtype=f32) / f32(ROT_DIM))
    ang = np.arange(seq_len, dtype=f32)[:, None] * inv_freq[None, :]
    cos, sin = np.cos(ang), np.sin(ang)
    ones = np.ones((seq_len, HEAD_DIM - ROT_DIM), f32)
    zeros = np.zeros((seq_len, HEAD_DIM - ROT_DIM), f32)
    zh = np.zeros((seq_len, half), f32)
    reps = LANES // HEAD_DIM
    cos_t = np.tile(np.concatenate([cos, cos, ones], axis=1), (1, reps))
    sin_up = np.tile(np.concatenate([zh, sin, zeros], axis=1), (1, reps))
    sin_dn = np.tile(np.concatenate([-sin, zh, zeros], axis=1), (1, reps))
    return jnp.asarray(np.concatenate([cos_t, sin_up, sin_dn], axis=1), F32)


def _ffn_weight_cast_specs(w_gate_up, w_down, layer, n_steps):
    _, d, f2 = w_gate_up.shape
    _, f, _ = w_down.shape
    assert d % (n_steps * BF16_ROWS) == 0 and f % (n_steps * BF16_ROWS) == 0
    slab = lambda i: (layer, i, 0)
    in_specs = [pl.BlockSpec((None, d // n_steps, f2), slab),
                pl.BlockSpec((None, f // n_steps, d), slab)]
    out_specs = [pl.BlockSpec((d // n_steps, f2), lambda i: (i, 0)),
                 pl.BlockSpec((f // n_steps, d), lambda i: (i, 0))]
    out_shape = [jax.ShapeDtypeStruct((d, f2), BF16), jax.ShapeDtypeStruct((f, d), BF16)]
    return in_specs, out_specs, out_shape


def _cast_ffn_weights(wgu_src, wd_src, wgu_dst, wd_dst):
    wgu_dst[...] = wgu_src[...].astype(BF16)
    wd_dst[...] = wd_src[...].astype(BF16)


def _conv_mixer_kernel(h_ref, pre_ref, win_ref, cw_ref, wout_ref, post_ref, wgu_src, wd_src,
                       o_ref, wgu_dst, wd_dst, cu_ref, win_bf_ref, wout_bf_ref,
                       *, tiles_per_seq, sub_rows):
    tm, d = h_ref.shape
    i = pl.program_id(0)
    _cast_ffn_weights(wgu_src, wd_src, wgu_dst, wd_dst)

    @pl.when(i == 0)
    def _():
        win_bf_ref[...] = win_ref[...].astype(BF16)
        wout_bf_ref[...] = wout_ref[...].astype(BF16)

    @pl.when(i % tiles_per_seq == 0)
    def _():
        cu_ref[0:SUBLANES, :] = jnp.zeros((SUBLANES, d), F32)

    @pl.when(i % tiles_per_seq != 0)
    def _():
        cu_ref[0:SUBLANES, :] = cu_ref[tm:tm + SUBLANES, :]

    cw = cw_ref[...]

    def gated_conv(r0):
        hn = _rms(h_ref[r0:r0 + sub_rows, :], pre_ref[...]).astype(BF16)
        cu = _dot(hn, win_bf_ref[:, d:2 * d]) * _dot(hn, win_bf_ref[:, 2 * d:3 * d])
        base = SUBLANES + r0
        cu_ref[base:base + sub_rows, :] = cu
        conv = (cu_ref[base - 2:base - 2 + sub_rows, :] * cw[0:1, :]
                + cu_ref[base - 1:base - 1 + sub_rows, :] * cw[1:2, :]
                + cu * cw[2:3, :])
        return (_dot(hn, win_bf_ref[:, 0:d]) * conv).astype(BF16)

    def output(r0, y):
        mix = _dot(y, wout_bf_ref[...])
        o_ref[r0:r0 + sub_rows, :] = h_ref[r0:r0 + sub_rows, :] + _rms(mix, post_ref[...])

    starts = list(range(0, tm, sub_rows))
    pending = [gated_conv(r0) for r0 in starts[:CONV_LOOKAHEAD]]
    for n, r0 in enumerate(starts):
        if n + CONV_LOOKAHEAD < len(starts):
            pending.append(gated_conv(starts[n + CONV_LOOKAHEAD]))
        output(r0, pending.pop(0))


def _conv_mixer(h, pre, w_in, conv_w, w_out, post, ffn_w_gate_up, ffn_w_down, layer, seq_len):
    t, d = h.shape
    tm = CONV_TILE
    n_steps = t // tm
    cast_in, cast_out, cast_shape = _ffn_weight_cast_specs(ffn_w_gate_up, ffn_w_down, layer, n_steps)
    kern = functools.partial(_conv_mixer_kernel, tiles_per_seq=seq_len // tm,
                             sub_rows=CONV_SUB_ROWS)
    return pl.pallas_call(
        kern,
        grid=(n_steps,),
        in_specs=[
            pl.BlockSpec((tm, d), lambda i: (i, 0)),
            _const_spec((1, d)),
            _layer_spec((d, 3 * d), layer),
            _const_spec((CONV_W, d)),
            _layer_spec((d, d), layer),
            _const_spec((1, d)),
            *cast_in,
        ],
        out_specs=[pl.BlockSpec((tm, d), lambda i: (i, 0)), *cast_out],
        out_shape=[jax.ShapeDtypeStruct((t, d), F32), *cast_shape],
        scratch_shapes=[pltpu.VMEM((tm + SUBLANES, d), F32),
                        pltpu.VMEM((d, 3 * d), BF16), pltpu.VMEM((d, d), BF16)],
        compiler_params=_params(("arbitrary",)),
        name="conv_mixer",
    )(h, pre, w_in, conv_w, w_out, post, ffn_w_gate_up, ffn_w_down)


def _ffn_kernel(h_ref, pre_ref, wgu_ref, wd_ref, post_ref, o_ref, *, chunks, sub_rows):
    f = wd_ref.shape[0]
    starts = list(range(0, h_ref.shape[0], sub_rows))
    hs = [h_ref[r0:r0 + sub_rows, :] for r0 in starts]
    hns = [_rms(h, pre_ref[...]).astype(BF16) for h in hs]
    accs = [None] * len(starts)
    for lo, hi in chunks:
        for n, hn in enumerate(hns):
            g = _dot(hn, wgu_ref[:, lo:hi])
            u = _dot(hn, wgu_ref[:, f + lo:f + hi])
            a = (g * jax.nn.sigmoid(g) * u).astype(BF16)
            part = _dot(a, wd_ref[lo:hi, :])
            accs[n] = part if accs[n] is None else accs[n] + part
    for r0, h, acc in zip(starts, hs, accs):
        o_ref[r0:r0 + sub_rows, :] = h + _rms(acc, post_ref[...])


def _ffn_chunks(f):
    step = 768
    return tuple((lo, min(lo + step, f)) for lo in range(0, f, step))


def _ffn(h, pre, w_gate_up, w_down, post):
    t, d = h.shape
    f = w_down.shape[0]
    tm = FFN_TILE
    row = lambda i: (i, 0)
    return pl.pallas_call(
        functools.partial(_ffn_kernel, chunks=_ffn_chunks(f), sub_rows=256),
        grid=(t // tm,),
        in_specs=[
            pl.BlockSpec((tm, d), row),
            _const_spec((1, d)),
            _const_spec((d, 2 * f)),
            _const_spec((f, d)),
            _const_spec((1, d)),
        ],
        out_specs=pl.BlockSpec((tm, d), row),
        out_shape=jax.ShapeDtypeStruct((t, d), F32),
        compiler_params=_params(("parallel",)),
        name="swiglu",
    )(h, pre, w_gate_up, w_down, post)


def _attn_kernel(sink_ref, h_ref, rope_ref, qn_ref, wq_ref, kvn_ref, wkv_ref,
                 wo_ref, post_ref, wgu_src, wd_src, o_ref, wgu_dst, wd_dst,
                 q_ref, k_ref, vt_ref, att_t_ref, wq_bf_ref, wkv_bf_ref, wo_bf_ref,
                 *, tiles_per_seq):
    _cast_ffn_weights(wgu_src, wd_src, wgu_dst, wd_dst)
    tq = q_ref.shape[0]
    kvd = vt_ref.shape[0]
    n_heads = q_ref.shape[1] // HEAD_DIM
    group = n_heads // N_KV_HEADS
    assert group * HEAD_DIM == 2 * LANES, "one unit = the two q lane chunks of a kv head"
    first_tile = (pl.program_id(0) % tiles_per_seq) == 0

    @pl.when(pl.program_id(0) == 0)
    def _():
        wq_bf_ref[...] = wq_ref[...].astype(BF16)
        wkv_bf_ref[...] = wkv_ref[...].astype(BF16)
        wo_bf_ref[...] = wo_ref[...].astype(BF16)

    lo_half = lax.broadcasted_iota(jnp.int32, (BLOCK, LANES), 1) < HEAD_DIM

    def unit_rms(x):
        return x * lax.rsqrt(jnp.mean(x * x, axis=-1, keepdims=True) + EPS)

    n_windows = tq // BLOCK

    def prev_group(j):
        return slice((2 * j) * BF16_ROWS, (2 * j + 1) * BF16_ROWS)

    def cur_group(j):
        return slice((2 * j + 1) * BF16_ROWS, (2 * j + 2) * BF16_ROWS)

    @pl.when(first_tile)
    def _():
        for j in range(BLOCK // BF16_ROWS):
            k_ref[0, prev_group(j), :] = jnp.zeros((BF16_ROWS, k_ref.shape[2]), BF16)
        vt_ref[:, 0:BLOCK] = jnp.zeros((kvd, BLOCK), BF16)

    @pl.when(jnp.logical_not(first_tile))
    def _():
        for j in range(BLOCK // BF16_ROWS):
            k_ref[0, prev_group(j), :] = k_ref[n_windows - 1, cur_group(j), :]
        vt_ref[:, 0:BLOCK] = vt_ref[:, tq:tq + BLOCK]

    def store_k(block, lanes, kb):
        for j in range(BLOCK // BF16_ROWS):
            rows16 = kb[j * BF16_ROWS:(j + 1) * BF16_ROWS]
            k_ref[block - 1, cur_group(j), lanes] = rows16
            if block < n_windows:
                k_ref[block, prev_group(j), lanes] = rows16

    def project_kv(rows_unit, rows_rope, r0):
        n = rows_unit.shape[0]
        kv = _dot((rows_unit * kvn_ref[...]).astype(BF16), wkv_bf_ref[...])
        k = _rope(kv[:, :kvd], rows_rope)
        lo = lax.broadcasted_iota(jnp.int32, (n, LANES), 1) < HEAD_DIM
        for c in range(kvd // LANES):
            pair = k[:, c * LANES:(c + 1) * LANES]
            swapped = pltpu.roll(pair, HEAD_DIM, axis=1)
            for half, dup in enumerate((jnp.where(lo, pair, swapped), jnp.where(lo, swapped, pair))):
                lanes = slice((2 * c + half) * LANES, (2 * c + half + 1) * LANES)
                for t in range(n // BLOCK):
                    store_k(r0 // BLOCK + t, lanes, dup[t * BLOCK:(t + 1) * BLOCK])
        vt_ref[:, r0:r0 + n] = kv[:, kvd:].T.astype(BF16)

    q_gain = qn_ref[...] * (LOG2E / math.sqrt(HEAD_DIM))

    def project(piece):
        rows = slice(piece * ATTN_PIECE, (piece + 1) * ATTN_PIECE)
        h_unit = unit_rms(h_ref[rows, :])
        project_kv(h_unit, rope_ref[rows, :], BLOCK + piece * ATTN_PIECE)
        q = _dot((h_unit * q_gain).astype(BF16), wq_bf_ref[...])
        q_ref[rows, :] = _rope(q, rope_ref[rows, :])

    def output(piece):
        rows = slice(piece * ATTN_PIECE, (piece + 1) * ATTN_PIECE)
        att = att_t_ref[:, rows].T
        proj = _dot(att, wo_bf_ref[...])
        o_ref[rows, :] = h_ref[rows, :] + _rms(proj, post_ref[...])

    unit_heads = (0, 2, 1, 3)
    width = group * BLOCK
    key_j = lax.broadcasted_iota(jnp.int32, (BLOCK, BLOCK), 0)
    query_i = lax.broadcasted_iota(jnp.int32, (BLOCK, BLOCK), 1)
    from_prev_blk = key_j > query_i

    def by_head(fn, *arrays):
        return jnp.concatenate(
            [fn(from_prev_blk, *(a[:, n * BLOCK:(n + 1) * BLOCK] for a in arrays))
             for n in range(group)], axis=1)
    seg = lax.broadcasted_iota(jnp.int32, (1, width), 1) // BLOCK
    ones_rows = jnp.ones((BF16_ROWS, 2 * BLOCK), BF16)

    prev_bias = jnp.where(first_tile, NEG, 0.0).astype(F32)

    def scores(blk, g):
        r0 = blk * BLOCK
        k_win = k_ref[blk, :, g * LANES:(g + 1) * LANES]
        qc0 = q_ref[r0:r0 + BLOCK, (2 * g) * LANES:(2 * g + 1) * LANES]
        qc1 = q_ref[r0:r0 + BLOCK, (2 * g + 1) * LANES:(2 * g + 2) * LANES]
        zero = jnp.zeros_like(qc0)
        q_split = jnp.concatenate(
            [jnp.where(lo_half, qc0, zero), jnp.where(lo_half, qc1, zero),
             jnp.where(lo_half, zero, qc0), jnp.where(lo_half, zero, qc1)], axis=0)
        return _dot_nt(k_win, q_split)

    def attend(blk, g, s_t):
        r0 = blk * BLOCK
        v_win = vt_ref[g * HEAD_DIM:(g + 1) * HEAD_DIM, r0:r0 + 2 * BLOCK]
        groups = [s_t[j * BF16_ROWS:(j + 1) * BF16_ROWS] for j in range(2 * BLOCK // BF16_ROWS)]
        s_prev = jnp.concatenate(groups[0::2], axis=0)
        s_cur = jnp.concatenate(groups[1::2], axis=0)
        if blk == 0:
            s_prev = s_prev + prev_bias
        s_win = by_head(jnp.where, s_prev, s_cur)
        m = jnp.max(s_win, axis=0, keepdims=True)
        e = jnp.exp2(s_win - m)
        p_t = jnp.concatenate([by_head(lambda mask, x: jnp.where(mask, x, 0.0), e),
                               by_head(lambda mask, x: jnp.where(mask, 0.0, x), e)],
                              axis=0).astype(BF16)
        v_aug = jnp.concatenate([v_win, ones_rows], axis=0)
        o_aug = _dot(v_aug, p_t)
        heads = [g * group + u for u in unit_heads]
        sink = sink_ref[heads[-1]] * LOG2E
        for n in range(group - 2, -1, -1):
            sink = jnp.where(seg == n, sink_ref[heads[n]] * LOG2E, sink)
        inv = 1.0 / (o_aug[HEAD_DIM:HEAD_DIM + 1, :] + jnp.exp2(sink - m))
        o_t = (o_aug[0:HEAD_DIM, :] * inv).astype(BF16)
        for n, head in enumerate(heads):
            att_t_ref[head * HEAD_DIM:(head + 1) * HEAD_DIM, r0:r0 + BLOCK] = (
                o_t[:, n * BLOCK:(n + 1) * BLOCK])

    n_pieces = tq // ATTN_PIECE
    units = [(blk, g) for blk in range(tq // BLOCK) for g in range(N_KV_HEADS)]
    per_piece = len(units) // n_pieces
    assert SCORE_LOOKAHEAD <= per_piece
    project(0)
    if n_pieces > 1:
        project(1)
    pending = [scores(*u) for u in units[:SCORE_LOOKAHEAD]]
    for piece in range(n_pieces):
        for n in range(piece * per_piece, (piece + 1) * per_piece):
            if n + SCORE_LOOKAHEAD < len(units):
                pending.append(scores(*units[n + SCORE_LOOKAHEAD]))
            attend(*units[n], pending.pop(0))
        if piece + 2 < n_pieces:
            project(piece + 2)
        if piece >= 1:
            output(piece - 1)
    output(n_pieces - 1)


def _attention(h, q_norm, w_q, kv_norm, w_kv, rope_table, sinks, w_o, post, layer,
               ffn_w_gate_up, ffn_w_down, ffn_layer, seq_len):
    t, d = h.shape
    qd = w_q.shape[2]
    kvd = w_kv.shape[1] // 2
    tq = ATTN_TILE
    tps = seq_len // tq
    row = lambda i: (i, 0)
    n_steps = t // tq
    cast_in, cast_out, cast_shape = _ffn_weight_cast_specs(ffn_w_gate_up, ffn_w_down, ffn_layer,
                                                           n_steps)
    kern = functools.partial(_attn_kernel, tiles_per_seq=tps)
    return pl.pallas_call(
        kern,
        grid=(n_steps,),
        in_specs=[
            pl.BlockSpec(memory_space=pltpu.SMEM),
            pl.BlockSpec((tq, d), row),
            pl.BlockSpec((tq, rope_table.shape[1]), lambda i: (i % tps, 0)),
            _const_spec((1, d)),
            _layer_spec((d, qd), layer),
            _const_spec((1, d)),
            _const_spec((d, 2 * kvd)),
            _layer_spec((qd, d), layer),
            _const_spec((1, d)),
            *cast_in,
        ],
        out_specs=[pl.BlockSpec((tq, d), row), *cast_out],
        out_shape=[jax.ShapeDtypeStruct((t, d), F32), *cast_shape],
        scratch_shapes=[
            pltpu.VMEM((tq, qd), BF16),
            pltpu.VMEM((tq // BLOCK, 2 * BLOCK, 2 * kvd), BF16),
            pltpu.VMEM((kvd, tq + BLOCK), BF16),
            pltpu.VMEM((qd, tq), BF16),
            pltpu.VMEM((d, qd), BF16),
            pltpu.VMEM((d, 2 * kvd), BF16),
            pltpu.VMEM((qd, d), BF16),
        ],
        compiler_params=_params(("arbitrary",)),
        name="swa_attention",
    )(sinks, h, rope_table, q_norm, w_q, kv_norm, w_kv, w_o, post, ffn_w_gate_up, ffn_w_down)


def kernel(x, a_pre_norm, a_w_in, a_conv_w, a_w_out, a_post_norm, ffn_pre_norm, ffn_w_gate_up,
           ffn_w_down, ffn_post_norm, kv_norm, w_kv, b_pre_norm, b_w_q, b_sinks, b_w_o,
           b_post_norm):
    bsz, s, d = x.shape
    n_a = a_w_in.shape[0]
    depth = ffn_w_gate_up.shape[0]
    rope_table = _rope_tables(s)
    vec = lambda g: g.reshape(1, -1)

    h = x.reshape(bsz * s, d)
    for l in range(depth):
        if l < n_a:
            h, w_gu, w_dn = _conv_mixer(h, vec(a_pre_norm[l]), a_w_in, a_conv_w[l], a_w_out,
                                        vec(a_post_norm[l]), ffn_w_gate_up, ffn_w_down, l, s)
        else:
            assert l == n_a, "K/V are projected inside the first attention layer's call only"
            j = l - n_a
            h, w_gu, w_dn = _attention(h, vec(b_pre_norm[j]), b_w_q, vec(kv_norm), w_kv, rope_table,
                                       b_sinks[j], b_w_o, vec(b_post_norm[j]), j,
                                       ffn_w_gate_up, ffn_w_down, l, s)
        h = _ffn(h, vec(ffn_pre_norm[l]), w_gu, w_dn, vec(ffn_post_norm[l]))
    return h.reshape(bsz, s, d)
```

```python
import functools
import math

import jax
import jax.numpy as jnp
import numpy as np
from jax import lax
from jax.experimental import pallas as pl
from jax.experimental.pallas import tpu as pltpu

CONV_W = 3
HEAD_DIM = 64
N_KV_HEADS = 4
WINDOW = 128
BLOCK = 128
ROT_DIM = HEAD_DIM // 4
ROPE_THETA = 500000.0
EPS = 1e-6
NEG = -1e30
LOG2E = math.log2(math.e)

LANES = 128
SUBLANES = 8
BF16_ROWS = 16
ROW_TILE = 512
CONV_TILE = 1024
CONV_SUB_ROWS = 512
CONV_LOOKAHEAD = 1
FFN_TILE = 1024
FFN_SUB_ROWS = 256
ATTN_TILE = 1024
ATTN_PIECE = 256
SCORE_LOOKAHEAD = 2
V7X_VMEM_BYTES = 64 * 1024 * 1024
VMEM_LIMIT = V7X_VMEM_BYTES - 4 * 1024 * 1024

F32 = jnp.float32
BF16 = jnp.bfloat16

assert WINDOW == BLOCK and LANES == 2 * HEAD_DIM


def _rms(x, g):
    r = lax.rsqrt(jnp.mean(x * x, axis=-1, keepdims=True) + EPS)
    return (x * r) * g


def _dot(a, b):
    return jnp.dot(a, b, preferred_element_type=F32)


def _dot_nt(a, b):
    return lax.dot_general(a, b, (((1,), (1,)), ((), ())), preferred_element_type=F32)


def _const_spec(shape):
    return pl.BlockSpec(shape, lambda *_: (0,) * len(shape), pipeline_mode=pl.Buffered(1))


def _layer_spec(shape, layer):
    return pl.BlockSpec((None,) + tuple(shape), lambda *_: (layer,) + (0,) * len(shape),
                        pipeline_mode=pl.Buffered(1))


def _params(semantics):
    return pltpu.CompilerParams(dimension_semantics=semantics, vmem_limit_bytes=VMEM_LIMIT)


def _rope(t, table):
    tb = table.astype(BF16)
    cos_t, sin_up, sin_dn = (tb[:, j * LANES:(j + 1) * LANES] for j in range(3))
    outs = []
    for j in range(t.shape[1] // LANES):
        c = t[:, j * LANES:(j + 1) * LANES].astype(BF16)
        outs.append(c * cos_t
                    + pltpu.roll(c, ROT_DIM // 2, axis=1) * sin_up
                    + pltpu.roll(c, LANES - ROT_DIM // 2, axis=1) * sin_dn)
    return jnp.concatenate(outs, axis=1)


def _rope_tables(seq_len):
    half = ROT_DIM // 2
    f32 = np.float32
    inv_freq = f32(ROPE_THETA) ** (-np.arange(0, ROT_DIM, 2, dtype=f32) / f32(ROT_DIM))
    ang = np.arange(seq_len, dtype=f32)[:, None] * inv_freq[None, :]
    cos, sin = np.cos(ang), np.sin(ang)
    ones = np.ones((seq_len, HEAD_DIM - ROT_DIM), f32)
    zeros = np.zeros((seq_len, HEAD_DIM - ROT_DIM), f32)
    zh = np.zeros((seq_len, half), f32)
    reps = LANES // HEAD_DIM
    cos_t = np.tile(np.concatenate([cos, cos, ones], axis=1), (1, reps))
    sin_up = np.tile(np.concatenate([zh, sin, zeros], axis=1), (1, reps))
    sin_dn = np.tile(np.concatenate([-sin, zh, zeros], axis=1), (1, reps))
    return jnp.asarray(np.concatenate([cos_t, sin_up, sin_dn], axis=1), F32)


def _ffn_weight_cast_specs(w_gate_up, w_down, layer, n_steps):
    _, d, f2 = w_gate_up.shape
    _, f, _ = w_down.shape
    assert d % (n_steps * BF16_ROWS) == 0 and f % (n_steps * BF16_ROWS) == 0
    slab = lambda i: (layer, i, 0)
    in_specs = [pl.BlockSpec((None, d // n_steps, f2), slab),
                pl.BlockSpec((None, f // n_steps, d), slab)]
    out_specs = [pl.BlockSpec((d // n_steps, f2), lambda i: (i, 0)),
                 pl.BlockSpec((f // n_steps, d), lambda i: (i, 0))]
    out_shape = [jax.ShapeDtypeStruct((d, f2), BF16), jax.ShapeDtypeStruct((f, d), BF16)]
    return in_specs, out_specs, out_shape


def _cast_ffn_weights(wgu_src, wd_src, wgu_dst, wd_dst):
    wgu_dst[...] = wgu_src[...].astype(BF16)
    wd_dst[...] = wd_src[...].astype(BF16)


def _conv_mixer_kernel(h_ref, pre_ref, win_ref, cw_ref, wout_ref, post_ref, wgu_src, wd_src,
                       o_ref, wgu_dst, wd_dst, cu_ref, win_bf_ref, wout_bf_ref,
                       *, tiles_per_seq, sub_rows):
    tm, d = h_ref.shape
    i = pl.program_id(0)
    _cast_ffn_weights(wgu_src, wd_src, wgu_dst, wd_dst)

    @pl.when(i == 0)
    def _():
        win_bf_ref[...] = win_ref[...].astype(BF16)
        wout_bf_ref[...] = wout_ref[...].astype(BF16)

    @pl.when(i % tiles_per_seq == 0)
    def _():
        cu_ref[0:SUBLANES, :] = jnp.zeros((SUBLANES, d), F32)

    @pl.when(i % tiles_per_seq != 0)
    def _():
        cu_ref[0:SUBLANES, :] = cu_ref[tm:tm + SUBLANES, :]

    cw = cw_ref[...]

    def gated_conv(r0):
        hn = _rms(h_ref[r0:r0 + sub_rows, :], pre_ref[...]).astype(BF16)
        cu = _dot(hn, win_bf_ref[:, d:2 * d]) * _dot(hn, win_bf_ref[:, 2 * d:3 * d])
        base = SUBLANES + r0
        cu_ref[base:base + sub_rows, :] = cu
        conv = (cu_ref[base - 2:base - 2 + sub_rows, :] * cw[0:1, :]
                + cu_ref[base - 1:base - 1 + sub_rows, :] * cw[1:2, :]
                + cu * cw[2:3, :])
        return (_dot(hn, win_bf_ref[:, 0:d]) * conv).astype(BF16)

    def output(r0, y):
        mix = _dot(y, wout_bf_ref[...])
        o_ref[r0:r0 + sub_rows, :] = h_ref[r0:r0 + sub_rows, :] + _rms(mix, post_ref[...])

    starts = list(range(0, tm, sub_rows))
    pending = [gated_conv(r0) for r0 in starts[:CONV_LOOKAHEAD]]
    for n, r0 in enumerate(starts):
        if n + CONV_LOOKAHEAD < len(starts):
            pending.append(gated_conv(starts[n + CONV_LOOKAHEAD]))
        output(r0, pending.pop(0))


def _conv_mixer(h, pre, w_in, conv_w, w_out, post, ffn_w_gate_up, ffn_w_down, layer, seq_len):
    t, d = h.shape
    tm = CONV_TILE
    n_steps = t // tm
    cast_in, cast_out, cast_shape = _ffn_weight_cast_specs(ffn_w_gate_up, ffn_w_down, layer, n_steps)
    kern = functools.partial(_conv_mixer_kernel, tiles_per_seq=seq_len // tm,
                             sub_rows=CONV_SUB_ROWS)
    return pl.pallas_call(
        kern,
        grid=(n_steps,),
        in_specs=[
            pl.BlockSpec((tm, d), lambda i: (i, 0)),
            _const_spec((1, d)),
            _layer_spec((d, 3 * d), layer),
            _const_spec((CONV_W, d)),
            _layer_spec((d, d), layer),
            _const_spec((1, d)),
            *cast_in,
        ],
        out_specs=[pl.BlockSpec((tm, d), lambda i: (i, 0)), *cast_out],
        out_shape=[jax.ShapeDtypeStruct((t, d), F32), *cast_shape],
        scratch_shapes=[pltpu.VMEM((tm + SUBLANES, d), F32),
                        pltpu.VMEM((d, 3 * d), BF16), pltpu.VMEM((d, d), BF16)],
        compiler_params=_params(("arbitrary",)),
        name="conv_mixer",
    )(h, pre, w_in, conv_w, w_out, post, ffn_w_gate_up, ffn_w_down)


def _ffn_kernel(h_ref, pre_ref, wgu_ref, wd_ref, post_ref, o_ref, *, chunks, sub_rows):
    f = wd_ref.shape[0]
    starts = list(range(0, h_ref.shape[0], sub_rows))
    hs = [h_ref[r0:r0 + sub_rows, :] for r0 in starts]
    hns = [_rms(h, pre_ref[...]).astype(BF16) for h in hs]
    accs = [None] * len(starts)
    for lo, hi in chunks:
        for n, hn in enumerate(hns):
            g = _dot(hn, wgu_ref[:, lo:hi])
            u = _dot(hn, wgu_ref[:, f + lo:f + hi])
            a = (g * jax.nn.sigmoid(g) * u).astype(BF16)
            part = _dot(a, wd_ref[lo:hi, :])
            accs[n] = part if accs[n] is None else accs[n] + part
    for r0, h, acc in zip(starts, hs, accs):
        o_ref[r0:r0 + sub_rows, :] = h + _rms(acc, post_ref[...])


def _ffn_chunks(f):
    step = 768
    return tuple((lo, min(lo + step, f)) for lo in range(0, f, step))


def _ffn(h, pre, w_gate_up, w_down, post):
    t, d = h.shape
    f = w_down.shape[0]
    tm = FFN_TILE
    row = lambda i: (i, 0)
    return pl.pallas_call(
        functools.partial(_ffn_kernel, chunks=_ffn_chunks(f), sub_rows=FFN_SUB_ROWS),
        grid=(t // tm,),
        in_specs=[
            pl.BlockSpec((tm, d), row),
            _const_spec((1, d)),
            _const_spec((d, 2 * f)),
            _const_spec((f, d)),
            _const_spec((1, d)),
        ],
        out_specs=pl.BlockSpec((tm, d), row),
        out_shape=jax.ShapeDtypeStruct((t, d), F32),
        compiler_params=_params(("parallel",)),
        name="swiglu",
    )(h, pre, w_gate_up, w_down, post)


def _attn_kernel(sink_ref, h_ref, rope_ref, qn_ref, wq_ref, kvn_ref, wkv_ref,
                 wo_ref, post_ref, wgu_src, wd_src, o_ref, wgu_dst, wd_dst,
                 q_ref, k_ref, vt_ref, att_t_ref, wq_bf_ref, wkv_bf_ref, wo_bf_ref,
                 *, tiles_per_seq):
    _cast_ffn_weights(wgu_src, wd_src, wgu_dst, wd_dst)
    tq = q_ref.shape[0]
    kvd = vt_ref.shape[0]
    n_heads = q_ref.shape[1] // HEAD_DIM
    group = n_heads // N_KV_HEADS
    assert group * HEAD_DIM == 2 * LANES, "one unit = the two q lane chunks of a kv head"
    first_tile = (pl.program_id(0) % tiles_per_seq) == 0

    @pl.when(pl.program_id(0) == 0)
    def _():
        wq_bf_ref[...] = wq_ref[...].astype(BF16)
        wkv_bf_ref[...] = wkv_ref[...].astype(BF16)
        wo_bf_ref[...] = wo_ref[...].astype(BF16)

    lo_half = lax.broadcasted_iota(jnp.int32, (BLOCK, LANES), 1) < HEAD_DIM

    def unit_rms(x):
        return x * lax.rsqrt(jnp.mean(x * x, axis=-1, keepdims=True) + EPS)

    n_windows = tq // BLOCK

    def prev_group(j):
        return slice((2 * j) * BF16_ROWS, (2 * j + 1) * BF16_ROWS)

    def cur_group(j):
        return slice((2 * j + 1) * BF16_ROWS, (2 * j + 2) * BF16_ROWS)

    @pl.when(first_tile)
    def _():
        for j in range(BLOCK // BF16_ROWS):
            k_ref[0, prev_group(j), :] = jnp.zeros((BF16_ROWS, k_ref.shape[2]), BF16)
        vt_ref[:, 0:BLOCK] = jnp.zeros((kvd, BLOCK), BF16)

    @pl.when(jnp.logical_not(first_tile))
    def _():
        for j in range(BLOCK // BF16_ROWS):
            k_ref[0, prev_group(j), :] = k_ref[n_windows - 1, cur_group(j), :]
        vt_ref[:, 0:BLOCK] = vt_ref[:, tq:tq + BLOCK]

    def store_k(block, lanes, kb):
        for j in range(BLOCK // BF16_ROWS):
            rows16 = kb[j * BF16_ROWS:(j + 1) * BF16_ROWS]
            k_ref[block - 1, cur_group(j), lanes] = rows16
            if block < n_windows:
                k_ref[block, prev_group(j), lanes] = rows16

    def project_kv(rows_unit, rows_rope, r0):
        n = rows_unit.shape[0]
        kv = _dot((rows_unit * kvn_ref[...]).astype(BF16), wkv_bf_ref[...])
        k = _rope(kv[:, :kvd], rows_rope)
        lo = lax.broadcasted_iota(jnp.int32, (n, LANES), 1) < HEAD_DIM
        for c in range(kvd // LANES):
            pair = k[:, c * LANES:(c + 1) * LANES]
            swapped = pltpu.roll(pair, HEAD_DIM, axis=1)
            for half, dup in enumerate((jnp.where(lo, pair, swapped), jnp.where(lo, swapped, pair))):
                lanes = slice((2 * c + half) * LANES, (2 * c + half + 1) * LANES)
                for t in range(n // BLOCK):
                    store_k(r0 // BLOCK + t, lanes, dup[t * BLOCK:(t + 1) * BLOCK])
        vt_ref[:, r0:r0 + n] = kv[:, kvd:].T.astype(BF16)

    q_gain = qn_ref[...] * (LOG2E / math.sqrt(HEAD_DIM))

    def project(piece):
        rows = slice(piece * ATTN_PIECE, (piece + 1) * ATTN_PIECE)
        h_unit = unit_rms(h_ref[rows, :])
        project_kv(h_unit, rope_ref[rows, :], BLOCK + piece * ATTN_PIECE)
        q = _dot((h_unit * q_gain).astype(BF16), wq_bf_ref[...])
        q_ref[rows, :] = _rope(q, rope_ref[rows, :])

    def output(piece):
        rows = slice(piece * ATTN_PIECE, (piece + 1) * ATTN_PIECE)
        att = att_t_ref[:, rows].T
        proj = _dot(att, wo_bf_ref[...])
        o_ref[rows, :] = h_ref[rows, :] + _rms(proj, post_ref[...])

    unit_heads = (0, 2, 1, 3)
    width = group * BLOCK
    key_j = lax.broadcasted_iota(jnp.int32, (BLOCK, BLOCK), 0)
    query_i = lax.broadcasted_iota(jnp.int32, (BLOCK, BLOCK), 1)
    from_prev_blk = key_j > query_i

    def by_head(fn, *arrays):
        return jnp.concatenate(
            [fn(from_prev_blk, *(a[:, n * BLOCK:(n + 1) * BLOCK] for a in arrays))
             for n in range(group)], axis=1)
    seg = lax.broadcasted_iota(jnp.int32, (1, width), 1) // BLOCK
    ones_rows = jnp.ones((BF16_ROWS, 2 * BLOCK), BF16)

    prev_bias = jnp.where(first_tile, NEG, 0.0).astype(F32)

    def scores(blk, g):
        r0 = blk * BLOCK
        k_win = k_ref[blk, :, g * LANES:(g + 1) * LANES]
        qc0 = q_ref[r0:r0 + BLOCK, (2 * g) * LANES:(2 * g + 1) * LANES]
        qc1 = q_ref[r0:r0 + BLOCK, (2 * g + 1) * LANES:(2 * g + 2) * LANES]
        zero = jnp.zeros_like(qc0)
        q_split = jnp.concatenate(
            [jnp.where(lo_half, qc0, zero), jnp.where(lo_half, qc1, zero),
             jnp.where(lo_half, zero, qc0), jnp.where(lo_half, zero, qc1)], axis=0)
        return _dot_nt(k_win, q_split)

    def attend(blk, g, s_t):
        r0 = blk * BLOCK
        v_win = vt_ref[g * HEAD_DIM:(g + 1) * HEAD_DIM, r0:r0 + 2 * BLOCK]
        groups = [s_t[j * BF16_ROWS:(j + 1) * BF16_ROWS] for j in range(2 * BLOCK // BF16_ROWS)]
        s_prev = jnp.concatenate(groups[0::2], axis=0)
        s_cur = jnp.concatenate(groups[1::2], axis=0)
        if blk == 0:
            s_prev = s_prev + prev_bias
        s_win = by_head(jnp.where, s_prev, s_cur)
        m = jnp.max(s_win, axis=0, keepdims=True)
        e = jnp.exp2(s_win - m)
        p_t = jnp.concatenate([by_head(lambda mask, x: jnp.where(mask, x, 0.0), e),
                               by_head(lambda mask, x: jnp.where(mask, 0.0, x), e)],
                              axis=0).astype(BF16)
        v_aug = jnp.concatenate([v_win, ones_rows], axis=0)
        o_aug = _dot(v_aug, p_t)
        heads = [g * group + u for u in unit_heads]
        sink = sink_ref[heads[-1]] * LOG2E
        for n in range(group - 2, -1, -1):
            sink = jnp.where(seg == n, sink_ref[heads[n]] * LOG2E, sink)
        inv = 1.0 / (o_aug[HEAD_DIM:HEAD_DIM + 1, :] + jnp.exp2(sink - m))
        o_t = (o_aug[0:HEAD_DIM, :] * inv).astype(BF16)
        for n, head in enumerate(heads):
            att_t_ref[head * HEAD_DIM:(head + 1) * HEAD_DIM, r0:r0 + BLOCK] = (
                o_t[:, n * BLOCK:(n + 1) * BLOCK])

    n_pieces = tq // ATTN_PIECE
    units = [(blk, g) for blk in range(tq // BLOCK) for g in range(N_KV_HEADS)]
    per_piece = len(units) // n_pieces
    assert SCORE_LOOKAHEAD <= per_piece
    project(0)
    if n_pieces > 1:
        project(1)
    pending = [scores(*u) for u in units[:SCORE_LOOKAHEAD]]
    for piece in range(n_pieces):
        for n in range(piece * per_piece, (piece + 1) * per_piece):
            if n + SCORE_LOOKAHEAD < len(units):
                pending.append(scores(*units[n + SCORE_LOOKAHEAD]))
            attend(*units[n], pending.pop(0))
        if piece + 2 < n_pieces:
            project(piece + 2)
        if piece >= 1:
            output(piece - 1)
    output(n_pieces - 1)


def _attention(h, q_norm, w_q, kv_norm, w_kv, rope_table, sinks, w_o, post, layer,
               ffn_w_gate_up, ffn_w_down, ffn_layer, seq_len):
    t, d = h.shape
    qd = w_q.shape[2]
    kvd = w_kv.shape[1] // 2
    tq = ATTN_TILE
    tps = seq_len // tq
    row = lambda i: (i, 0)
    n_steps = t // tq
    cast_in, cast_out, cast_shape = _ffn_weight_cast_specs(ffn_w_gate_up, ffn_w_down, ffn_layer,
                                                           n_steps)
    kern = functools.partial(_attn_kernel, tiles_per_seq=tps)
    return pl.pallas_call(
        kern,
        grid=(n_steps,),
        in_specs=[
            pl.BlockSpec(memory_space=pltpu.SMEM),
            pl.BlockSpec((tq, d), row),
            pl.BlockSpec((tq, rope_table.shape[1]), lambda i: (i % tps, 0)),
            _const_spec((1, d)),
            _layer_spec((d, qd), layer),
            _const_spec((1, d)),
            _const_spec((d, 2 * kvd)),
            _layer_spec((qd, d), layer),
            _const_spec((1, d)),
            *cast_in,
        ],
        out_specs=[pl.BlockSpec((tq, d), row), *cast_out],
        out_shape=[jax.ShapeDtypeStruct((t, d), F32), *cast_shape],
        scratch_shapes=[
            pltpu.VMEM((tq, qd), BF16),
            pltpu.VMEM((tq // BLOCK, 2 * BLOCK, 2 * kvd), BF16),
            pltpu.VMEM((kvd, tq + BLOCK), BF16),
            pltpu.VMEM((qd, tq), BF16),
            pltpu.VMEM((d, qd), BF16),
            pltpu.VMEM((d, 2 * kvd), BF16),
            pltpu.VMEM((qd, d), BF16),
        ],
        compiler_params=_params(("arbitrary",)),
        name="swa_attention",
    )(sinks, h, rope_table, q_norm, w_q, kv_norm, w_kv, w_o, post, ffn_w_gate_up, ffn_w_down)


def kernel(x, a_pre_norm, a_w_in, a_conv_w, a_w_out, a_post_norm, ffn_pre_norm, ffn_w_gate_up,
           ffn_w_down, ffn_post_norm, kv_norm, w_kv, b_pre_norm, b_w_q, b_sinks, b_w_o,
           b_post_norm):
    bsz, s, d = x.shape
    n_a = a_w_in.shape[0]
    depth = ffn_w_gate_up.shape[0]
    rope_table = _rope_tables(s)
    vec = lambda g: g.reshape(1, -1)

    h = x.reshape(bsz * s, d)
    for l in range(depth):
        if l < n_a:
            h, w_gu, w_dn = _conv_mixer(h, vec(a_pre_norm[l]), a_w_in, a_conv_w[l], a_w_out,
                                        vec(a_post_norm[l]), ffn_w_gate_up, ffn_w_down, l, s)
        else:
            assert l == n_a, "K/V are projected inside the first attention layer's call only"
            j = l - n_a
            h, w_gu, w_dn = _attention(h, vec(b_pre_norm[j]), b_w_q, vec(kv_norm), w_kv, rope_table,
                                       b_sinks[j], b_w_o, vec(b_post_norm[j]), j,
                                       ffn_w_gate_up, ffn_w_down, l, s)
        h = _ffn(h, vec(ffn_pre_norm[l]), w_gu, w_dn, vec(ffn_post_norm[l]))
    return h.reshape(bsz, s, d)
```
